```python
import math
import jax, jax.numpy as jnp
from jax import lax
import numpy as np

D_MODEL = 2048
BATCH = 4
SEQ = 8192
DEPTH = 1
DEC_BATCH = 4
DEC_SEQ = 2048
PAST_LEN = 128

N_HEADS = 8
HEAD_DIM = 64
ATTN_W = N_HEADS * 2 * HEAD_DIM
SSM_W = D_MODEL // 2
GROUP_SIZE = 16
N_GROUPS = SSM_W // GROUP_SIZE
STATE = 64
IN_COLS = 3 * ATTN_W + SSM_W + 2 * D_MODEL
D_FF = 4 * D_MODEL
N_MEM = 256
X_HEADS = 4
X_HEAD_DIM = D_MODEL // X_HEADS
Q_BLOCK = 128
ROPE_THETA = 10000.0
EPS = 1e-6

kernel_name = "hybrid_diffattn_s5_encoder"


def rms_norm(x, g):
    xf = x.astype(jnp.float32)
    y = xf * lax.rsqrt(jnp.mean(xf * xf, axis=-1, keepdims=True) + EPS)
    return (y * g.astype(jnp.float32)).astype(x.dtype)


def rope(t):
    L = t.shape[1]
    half = HEAD_DIM // 2
    inv = ROPE_THETA ** (-jnp.arange(0, HEAD_DIM, 2, dtype=jnp.float32) / HEAD_DIM)
    ang = jnp.arange(L, dtype=jnp.float32)[:, None] * inv[None, :]
    cos = jnp.cos(ang)[None, :, None, None, :].astype(t.dtype)
    sin = jnp.sin(ang)[None, :, None, None, :].astype(t.dtype)
    t1, t2 = t[..., :half], t[..., half:]
    return jnp.concatenate([t1 * cos - t2 * sin, t2 * cos + t1 * sin], axis=-1)


def diff_attention(q, k, v, lam):
    B_, L = q.shape[0], q.shape[1]
    nblk = L // Q_BLOCK
    qb = q.reshape(B_, nblk, Q_BLOCK, N_HEADS, 2, HEAD_DIM).transpose(1, 0, 2, 3, 4, 5)

    def block(qi):
        s = jnp.einsum('bqhmd,bkhmd->bhmqk', qi, k, preferred_element_type=jnp.float32)
        p = jax.nn.softmax(s, axis=-1)
        p_diff = (p[:, :, 0] - lam * p[:, :, 1]).astype(v.dtype)
        return jnp.einsum('bhqk,bkhe->bqhe', p_diff, v)

    out = lax.map(block, qb)
    return out.transpose(1, 0, 2, 3, 4).reshape(B_, L, N_HEADS, 2 * HEAD_DIM)


def _combine(e_i, e_j):
    ai_re, ai_im, bi_re, bi_im = e_i
    aj_re, aj_im, bj_re, bj_im = e_j
    a_re = aj_re * ai_re - aj_im * ai_im
    a_im = aj_re * ai_im + aj_im * ai_re
    b_re = aj_re * bi_re - aj_im * bi_im + bj_re
    b_im = aj_re * bi_im + aj_im * bi_re + bj_im
    return (a_re, a_im, b_re, b_im)


def s5_bidirectional(u, lam_re, lam_im, log_dt, b_re, b_im, c_re, c_im, d_skip):
    B_, L, _ = u.shape
    uf = u.astype(jnp.float32).reshape(B_, L, N_GROUPS, GROUP_SIZE)
    y = d_skip.astype(jnp.float32).reshape(N_GROUPS, GROUP_SIZE) * uf
    for direction in range(2):
        lr = lam_re[direction].astype(jnp.float32)
        li = lam_im[direction].astype(jnp.float32)
        dt = jnp.exp(log_dt[direction].astype(jnp.float32))[:, None]
        mag = jnp.exp(lr * dt)
        ab_re = mag * jnp.cos(li * dt)
        ab_im = mag * jnp.sin(li * dt)
        den = lr * lr + li * li
        n_re, n_im = ab_re - 1.0, ab_im
        k_re = (n_re * lr + n_im * li) / den
        k_im = (n_im * lr - n_re * li) / den
        br = b_re[direction].astype(jnp.float32)
        bi = b_im[direction].astype(jnp.float32)
        bb_re = k_re[..., None] * br - k_im[..., None] * bi
        bb_im = k_re[..., None] * bi + k_im[..., None] * br
        bu_re = jnp.einsum('blgc,gpc->lbgp', uf, bb_re)
        bu_im = jnp.einsum('blgc,gpc->lbgp', uf, bb_im)
        if direction == 1:
            bu_re = jnp.flip(bu_re, axis=0)
            bu_im = jnp.flip(bu_im, axis=0)
        a_re = jnp.broadcast_to(ab_re[None, None], (L, 1, N_GROUPS, STATE))
        a_im = jnp.broadcast_to(ab_im[None, None], (L, 1, N_GROUPS, STATE))
        _, _, x_re, x_im = lax.associative_scan(_combine, (a_re, a_im, bu_re, bu_im), axis=0)
        if direction == 1:
            x_re = jnp.flip(x_re, axis=0)
            x_im = jnp.flip(x_im, axis=0)
        y = y + jnp.einsum('lbgp,gcp->blgc', x_re, c_re[direction].astype(jnp.float32)) \
              - jnp.einsum('lbgp,gcp->blgc', x_im, c_im[direction].astype(jnp.float32))
    return y.reshape(B_, L, SSM_W).astype(u.dtype)


def encode(x, mem, w):
    (norm_mix, w_in, diff_lambda, subln, w_attn_proj,
     ssm_lambda_re, ssm_lambda_im, ssm_log_dt, ssm_b_re, ssm_b_im, ssm_c_re, ssm_c_im, ssm_d,
     w_glu_a, w_glu_b, w_mix_out,
     norm_cross, norm_mem, w_q_cross, w_kv_cross, w_o_cross,
     norm_mlp, w_mlp_up, w_mlp_down, norm_final) = w
    B_, L, _ = x.shape
    M = mem.shape[1]
    for layer in range(DEPTH):
        xn = rms_norm(x, norm_mix[layer])
        proj = xn @ w_in[layer]
        q, k, v, u, gates = jnp.split(
            proj, [ATTN_W, 2 * ATTN_W, 3 * ATTN_W, 3 * ATTN_W + SSM_W], axis=-1)
        q = rope(q.reshape(B_, L, N_HEADS, 2, HEAD_DIM)) * (HEAD_DIM ** -0.5)
        k = rope(k.reshape(B_, L, N_HEADS, 2, HEAD_DIM))
        v = v.reshape(B_, L, N_HEADS, 2 * HEAD_DIM)
        lam_init = 0.8 - 0.6 * math.exp(-0.3 * layer)
        dl = diff_lambda[layer].astype(jnp.float32)
        lam = jnp.exp(jnp.sum(dl[0] * dl[1])) - jnp.exp(jnp.sum(dl[2] * dl[3])) + lam_init
        attn = diff_attention(q, k, v, lam)
        attn = rms_norm(attn, subln[layer]) * (1.0 - lam_init)
        y_attn = attn.reshape(B_, L, ATTN_W) @ w_attn_proj[layer]

        s = s5_bidirectional(u, ssm_lambda_re[layer], ssm_lambda_im[layer], ssm_log_dt[layer],
                             ssm_b_re[layer], ssm_b_im[layer], ssm_c_re[layer], ssm_c_im[layer],
                             ssm_d[layer])
        z = jax.nn.gelu(s)
        y_ssm = (z @ w_glu_a[layer]) * jax.nn.sigmoid(z @ w_glu_b[layer])

        g = jax.nn.sigmoid(gates)
        g_attn, g_ssm = g[..., :D_MODEL], g[..., D_MODEL:]
        x = x + (g_attn * y_attn + g_ssm * y_ssm) @ w_mix_out[layer]

        hn = rms_norm(x, norm_cross[layer])
        mn = rms_norm(mem, norm_mem[layer])
        qc = (hn @ w_q_cross[layer]).reshape(B_, L, X_HEADS, X_HEAD_DIM) * (X_HEAD_DIM ** -0.5)
        kv = (mn @ w_kv_cross[layer]).reshape(B_, M, 2, X_HEADS, X_HEAD_DIM)
        sc = jnp.einsum('bqhd,bkhd->bhqk', qc, kv[:, :, 0], preferred_element_type=jnp.float32)
        pc = jax.nn.softmax(sc, axis=-1).astype(x.dtype)
        oc = jnp.einsum('bhqk,bkhd->bqhd', pc, kv[:, :, 1]).reshape(B_, L, D_MODEL)
        x = x + oc @ w_o_cross[layer]

        hm = rms_norm(x, norm_mlp[layer])
        x = x + jnp.square(jax.nn.relu(hm @ w_mlp_up[layer])) @ w_mlp_down[layer]
    return rms_norm(x, norm_final)


def setup_inputs(seed: int = 0) -> dict:
    key = jax.random.key(seed)
    ks = iter(jax.random.split(key, 40))
    f32 = jnp.float32

    def nrm(shape, scale):
        return jax.random.normal(next(ks), shape, f32) * scale

    def gain(shape):
        return 1.0 + 0.02 * jax.random.normal(next(ks), shape, f32)

    Dd = DEPTH
    P, G, GS = STATE, N_GROUPS, GROUP_SIZE
    inp = {}
    inp["x_prompt"] = nrm((BATCH, SEQ, D_MODEL), 1.0)
    inp["x_sample"] = nrm((DEC_BATCH, DEC_SEQ, D_MODEL), 1.0)
    inp["mem_prompt"] = nrm((BATCH, N_MEM, D_MODEL), 1.0)
    inp["mem_sample"] = nrm((DEC_BATCH, N_MEM, D_MODEL), 1.0)
    inp["norm_mix"] = gain((Dd, D_MODEL))
    inp["w_in"] = nrm((Dd, D_MODEL, IN_COLS), D_MODEL ** -0.5)
    inp["diff_lambda"] = nrm((Dd, 4, HEAD_DIM), 0.1)
    inp["subln"] = gain((Dd, 2 * HEAD_DIM))
    inp["w_attn_proj"] = nrm((Dd, ATTN_W, D_MODEL), ATTN_W ** -0.5)
    inp["ssm_lambda_re"] = -0.5 + nrm((Dd, 2, G, P), 0.01)
    inp["ssm_lambda_im"] = math.pi * jnp.arange(P, dtype=f32) + nrm((Dd, 2, G, P), 0.01)
    inp["ssm_log_dt"] = jax.random.uniform(next(ks), (Dd, 2, G), f32,
                                           math.log(0.001), math.log(0.1))
    inp["ssm_b_re"] = nrm((Dd, 2, G, P, GS), (2.0 * GS) ** -0.5)
    inp["ssm_b_im"] = nrm((Dd, 2, G, P, GS), (2.0 * GS) ** -0.5)
    inp["ssm_c_re"] = nrm((Dd, 2, G, GS, P), (2.0 * P) ** -0.5)
    inp["ssm_c_im"] = nrm((Dd, 2, G, GS, P), (2.0 * P) ** -0.5)
    inp["ssm_d"] = nrm((Dd, SSM_W), 1.0)
    inp["w_glu_a"] = nrm((Dd, SSM_W, D_MODEL), SSM_W ** -0.5)
    inp["w_glu_b"] = nrm((Dd, SSM_W, D_MODEL), SSM_W ** -0.5)
    inp["w_mix_out"] = nrm((Dd, D_MODEL, D_MODEL), D_MODEL ** -0.5)
    inp["norm_cross"] = gain((Dd, D_MODEL))
    inp["norm_mem"] = gain((Dd, D_MODEL))
    inp["w_q_cross"] = nrm((Dd, D_MODEL, D_MODEL), D_MODEL ** -0.5)
    inp["w_kv_cross"] = nrm((Dd, D_MODEL, 2 * D_MODEL), D_MODEL ** -0.5)
    inp["w_o_cross"] = nrm((Dd, D_MODEL, D_MODEL), D_MODEL ** -0.5)
    inp["norm_mlp"] = gain((Dd, D_MODEL))
    inp["w_mlp_up"] = nrm((Dd, D_MODEL, D_FF), D_MODEL ** -0.5)
    inp["w_mlp_down"] = nrm((Dd, D_FF, D_MODEL), D_FF ** -0.5)
    inp["norm_final"] = gain((D_MODEL,))
    return inp


def reference(x_prompt, x_sample, mem_prompt, mem_sample,
              norm_mix, w_in, diff_lambda, subln, w_attn_proj,
              ssm_lambda_re, ssm_lambda_im, ssm_log_dt, ssm_b_re, ssm_b_im,
              ssm_c_re, ssm_c_im, ssm_d, w_glu_a, w_glu_b, w_mix_out,
              norm_cross, norm_mem, w_q_cross, w_kv_cross, w_o_cross,
              norm_mlp, w_mlp_up, w_mlp_down, norm_final):
    w = (norm_mix, w_in, diff_lambda, subln, w_attn_proj,
         ssm_lambda_re, ssm_lambda_im, ssm_log_dt, ssm_b_re, ssm_b_im, ssm_c_re, ssm_c_im, ssm_d,
         w_glu_a, w_glu_b, w_mix_out,
         norm_cross, norm_mem, w_q_cross, w_kv_cross, w_o_cross,
         norm_mlp, w_mlp_up, w_mlp_down, norm_final)
    y_prompt = encode(x_prompt, mem_prompt, w)
    y_sample = encode(x_sample, mem_sample, w)
    return (y_prompt, y_sample)
```

```python
import functools
import math

import jax
import jax.numpy as jnp
from jax import lax
from jax.experimental import pallas as pl
from jax.experimental.pallas import tpu as pltpu

D_MODEL = 2048
N_HEADS = 8
HEAD_DIM = 64
ATTN_W = N_HEADS * 2 * HEAD_DIM
SSM_W = D_MODEL // 2
GROUP_SIZE = 16
N_GROUPS = SSM_W // GROUP_SIZE
STATE = 64
IN_COLS = 3 * ATTN_W + SSM_W + 2 * D_MODEL
D_FF = 4 * D_MODEL
X_HEADS = 4
X_HEAD_DIM = D_MODEL // X_HEADS
ROPE_THETA = 10000.0
EPS = 1e-6
LAM_INIT = 0.8 - 0.6 * math.exp(-0.3 * 0)

CHUNK = 16
GROUP_BLOCK = 4
CW = GROUP_BLOCK * STATE
LANES = 128
VMEM_LIMIT = 56 * 1024 * 1024

F32 = jnp.float32
BF16 = jnp.bfloat16
HI = lax.Precision.HIGHEST


def _params(*sem):
    return pltpu.CompilerParams(dimension_semantics=sem, vmem_limit_bytes=VMEM_LIMIT)


def _rms(x, g):
    return x * lax.rsqrt(jnp.mean(x * x, axis=-1, keepdims=True) + EPS) * g


def _in_proj_kernel(x_ref, g_ref, w_ref, cos_ref, sin_ref,
                    q_ref, k_ref, v_ref, u_ref, gate_ref, xn_ref):
    j = pl.program_id(1)

    @pl.when(j == 0)
    def _():
        xn_ref[...] = _rms(x_ref[...], g_ref[...]).astype(BF16)

    acc = jnp.dot(xn_ref[...], w_ref[...], preferred_element_type=F32)

    def rope(scale):
        cos = cos_ref[...]
        sin = sin_ref[...]
        outs = []
        for c in range(acc.shape[1] // LANES):
            xc = acc[:, c * LANES:(c + 1) * LANES]
            outs.append((xc * cos + pltpu.roll(xc, LANES // 2, axis=1) * sin) * scale)
        return jnp.concatenate(outs, axis=1).astype(BF16)

    @pl.when(j == 0)
    def _():
        q_ref[...] = rope(HEAD_DIM ** -0.5)

    @pl.when(j == 1)
    def _():
        k_ref[...] = rope(1.0)

    @pl.when(j == 2)
    def _():
        v_ref[...] = acc.astype(BF16)

    @pl.when(j == 3)
    def _():
        u_ref[...] = acc.astype(BF16)

    @pl.when(j >= 4)
    def _():
        gate_ref[...] = acc.astype(BF16)


def _in_proj(x2, g, w, cos_t, sin_t, seq):
    n = x2.shape[0]
    tm = min(512, seq)
    tn = ATTN_W
    nj = IN_COLS // tn
    pos_blocks = seq // tm
    row = lambda i, j: (i, 0)
    out_sd = lambda cols: jax.ShapeDtypeStruct((n, cols), BF16)
    return pl.pallas_call(
        _in_proj_kernel,
        grid=(n // tm, nj),
        in_specs=[
            pl.BlockSpec((tm, D_MODEL), row),
            pl.BlockSpec((1, D_MODEL), lambda i, j: (0, 0)),
            pl.BlockSpec((D_MODEL, tn), lambda i, j: (0, j)),
            pl.BlockSpec((tm, LANES), lambda i, j: (i % pos_blocks, 0)),
            pl.BlockSpec((tm, LANES), lambda i, j: (i % pos_blocks, 0)),
        ],
        out_specs=[
            pl.BlockSpec((tm, tn), row),
            pl.BlockSpec((tm, tn), row),
            pl.BlockSpec((tm, tn), row),
            pl.BlockSpec((tm, tn), row),
            pl.BlockSpec((tm, tn), lambda i, j: (i, jnp.maximum(j - 4, 0))),
        ],
        out_shape=[out_sd(ATTN_W), out_sd(ATTN_W), out_sd(ATTN_W), out_sd(SSM_W),
                   out_sd(2 * D_MODEL)],
        scratch_shapes=[pltpu.VMEM((tm, D_MODEL), BF16)],
        compiler_params=_params("parallel", "arbitrary"),
        name="in_proj",
    )(x2, g, w, cos_t, sin_t)


def _diff_attn_kernel(dl_ref, sub_ref, q_ref, k_ref, v_ref, o_ref, *, tk):
    q = q_ref[0]
    tq = q.shape[0]
    seq = k_ref.shape[1]
    lane = lax.broadcasted_iota(jnp.int32, q.shape, 1)
    first = ((lane // (HEAD_DIM // 2)) % 2) == 0
    zero = jnp.zeros_like(q)
    q_a = jnp.where(first, q, zero)
    q_b = jnp.where(first, zero, q)
    nt = (((1,), (1,)), ((), ()))

    def body(c, carry):
        start = pl.multiple_of(c * tk, tk)
        kc = k_ref[0, pl.ds(start, tk), :]
        vc = v_ref[0, pl.ds(start, tk), :]
        new = []
        for qm, (m, l, a) in zip((q_a, q_b), carry):
            s = lax.dot_general(qm, kc, nt, preferred_element_type=F32)
            m_new = jnp.maximum(m, jnp.max(s, axis=-1, keepdims=True))
            p = jnp.exp(s - m_new)
            alpha = jnp.exp(m - m_new)
            l_new = alpha * l + jnp.sum(p, axis=-1, keepdims=True)
            a_new = alpha * a + jnp.dot(p.astype(BF16), vc, preferred_element_type=F32)
            new.append((m_new, l_new, a_new))
        return tuple(new)

    init = tuple((jnp.full((tq, 1), -jnp.inf, F32), jnp.zeros((tq, 1), F32),
                  jnp.zeros((tq, 2 * HEAD_DIM), F32)) for _ in range(2))
    (_, l1, a1), (_, l2, a2) = lax.fori_loop(0, seq // tk, body, init)

    dl = dl_ref[...]
    lam = (jnp.exp(jnp.sum(dl[0:1] * dl[1:2], axis=-1, keepdims=True))
           - jnp.exp(jnp.sum(dl[2:3] * dl[3:4], axis=-1, keepdims=True)) + LAM_INIT)
    o = a1 / l1 - lam * (a2 / l2)
    o_ref[0] = (_rms(o, sub_ref[...]) * (1.0 - LAM_INIT)).astype(BF16)


def _diff_attn(q3, k3, v3, dl, sub):
    b, seq, _ = q3.shape
    tq = min(256, seq)
    tk = min(512, seq)
    hw = 2 * HEAD_DIM
    return pl.pallas_call(
        functools.partial(_diff_attn_kernel, tk=tk),
        grid=(b, N_HEADS, seq // tq),
        in_specs=[
            pl.BlockSpec((4, HEAD_DIM), lambda b_, h, i: (0, 0)),
            pl.BlockSpec((1, hw), lambda b_, h, i: (0, 0)),
            pl.BlockSpec((1, tq, hw), lambda b_, h, i: (b_, i, h)),
            pl.BlockSpec((1, seq, hw), lambda b_, h, i: (b_, 0, h)),
            pl.BlockSpec((1, seq, hw), lambda b_, h, i: (b_, 0, h)),
        ],
        out_specs=pl.BlockSpec((1, tq, hw), lambda b_, h, i: (b_, i, h)),
        out_shape=jax.ShapeDtypeStruct((b, seq, ATTN_W), BF16),
        compiler_params=_params("parallel", "parallel", "arbitrary"),
        name="diff_attn",
    )(dl, sub, q3, k3, v3)


def _ssm_kernel(u_ref, wi_ref, wb_ref, wc_ref, a_ref, z_ref, s_ref, *, rb):
    rows = u_ref.shape[0]
    n_tiles = rows // 8

    def summarise(r, _):
        rs = pl.ds(pl.multiple_of(r * rb, rb), rb)
        s_ref[rs, :] = jnp.dot(u_ref[rs, :], wb_ref[0], preferred_element_type=F32)
        return 0

    lax.fori_loop(0, rows // rb, summarise, 0)

    a = a_ref[0]
    af_re, af_im, ab_re, ab_im = (a[i:i + 1, :] for i in range(4))
    low = lax.broadcasted_iota(jnp.int32, (8, CW), 0) < 4

    def cmul_add(ar, ai, hr, hi, tr, ti):
        return ar * hr - ai * hi + tr, ar * hi + ai * hr + ti

    def half_swap(x):
        return pltpu.roll(x, 4, axis=0)

    def scan(i, carry):
        hf_re, hf_im, hb_re, hb_im = carry
        rs = pl.ds(pl.multiple_of(i * 8, 8), 8)
        t_re = s_ref[rs, 0:CW]
        t_im = s_ref[rs, CW:2 * CW]
        m_re, m_im = cmul_add(af_re, af_im, hf_re, hf_im, t_re, t_im)
        m_re, m_im = half_swap(m_re), half_swap(m_im)
        s_ref[rs, 0:CW] = jnp.where(low, hf_re, m_re)
        s_ref[rs, CW:2 * CW] = jnp.where(low, hf_im, m_im)
        n_re, n_im = cmul_add(af_re, af_im, m_re, m_im, t_re, t_im)
        hf_re, hf_im = half_swap(n_re), half_swap(n_im)
        rs = pl.ds(pl.multiple_of((n_tiles - 1 - i) * 8, 8), 8)
        t_re = s_ref[rs, 2 * CW:3 * CW]
        t_im = s_ref[rs, 3 * CW:4 * CW]
        m_re, m_im = cmul_add(ab_re, ab_im, hb_re, hb_im, t_re, t_im)
        m_re, m_im = half_swap(m_re), half_swap(m_im)
        s_ref[rs, 2 * CW:3 * CW] = jnp.where(low, m_re, hb_re)
        s_ref[rs, 3 * CW:4 * CW] = jnp.where(low, m_im, hb_im)
        n_re, n_im = cmul_add(ab_re, ab_im, m_re, m_im, t_re, t_im)
        hb_re, hb_im = half_swap(n_re), half_swap(n_im)
        return hf_re, hf_im, hb_re, hb_im

    zero = jnp.zeros((8, CW), F32)
    lax.fori_loop(0, n_tiles, scan, (zero, zero, zero, zero))

    gw = CHUNK * GROUP_SIZE

    def emit(r, _):
        rs = pl.ds(pl.multiple_of(r * rb, rb), rb)
        h = s_ref[rs, :].astype(BF16)
        for g in range(GROUP_BLOCK):
            cols = slice(g * gw, (g + 1) * gw)
            y = jnp.dot(u_ref[rs, cols], wi_ref[0, g], preferred_element_type=F32)
            y = y + jnp.dot(h, wc_ref[0, :, cols], preferred_element_type=F32)
            z = 0.5 * y * (1.0 + jnp.tanh(0.7978845608028654 * (y + 0.044715 * (y * y * y))))
            z_ref[rs, cols] = z.astype(BF16)
        return 0

    lax.fori_loop(0, rows // rb, emit, 0)


def _ssm(ut, wi, wb, wc, a16):
    rows = ut.shape[0]
    rb = min(256, rows)
    bw = GROUP_BLOCK * CHUNK * GROUP_SIZE
    return pl.pallas_call(
        functools.partial(_ssm_kernel, rb=rb),
        grid=(N_GROUPS // GROUP_BLOCK,),
        in_specs=[
            pl.BlockSpec((rows, bw), lambda g: (0, g)),
            pl.BlockSpec((1, GROUP_BLOCK, CHUNK * GROUP_SIZE, CHUNK * GROUP_SIZE),
                         lambda g: (g, 0, 0, 0)),
            pl.BlockSpec((1, bw, 4 * CW), lambda g: (g, 0, 0)),
            pl.BlockSpec((1, 4 * CW, bw), lambda g: (g, 0, 0)),
            pl.BlockSpec((1, 4, CW), lambda g: (g, 0, 0)),
        ],
        out_specs=pl.BlockSpec((rows, bw), lambda g: (0, g)),
        out_shape=jax.ShapeDtypeStruct(ut.shape, BF16),
        scratch_shapes=[pltpu.VMEM((rows, 4 * CW), F32)],
        compiler_params=_params("parallel"),
        name="s5_scan",
    )(ut, wi, wb, wc, a16)


def _ssm_operators(lam_re, lam_im, log_dt, b_re, b_im, c_re, c_im, d_skip):
    t_ = CHUNK
    g_, p_, c_ = N_GROUPS, STATE, GROUP_SIZE
    lr, li = lam_re.astype(F32), lam_im.astype(F32)
    dt = jnp.exp(log_dt.astype(F32))[..., None]
    mag = jnp.exp(lr * dt)
    ab_re, ab_im = mag * jnp.cos(li * dt), mag * jnp.sin(li * dt)
    den = lr * lr + li * li
    n_re, n_im = ab_re - 1.0, ab_im
    k_re = (n_re * lr + n_im * li) / den
    k_im = (n_im * lr - n_re * li) / den
    br, bi = b_re.astype(F32), b_im.astype(F32)
    bb_re = k_re[..., None] * br - k_im[..., None] * bi
    bb_im = k_re[..., None] * bi + k_im[..., None] * br
    cr, ci = c_re.astype(F32), c_im.astype(F32)

    ks = jnp.arange(t_ + 1, dtype=F32)[:, None, None, None]
    pmag = jnp.exp(ks * (lr * dt))
    pw_re, pw_im = pmag * jnp.cos(ks * (li * dt)), pmag * jnp.sin(ks * (li * dt))

    ca_re = cr[None] * pw_re[:t_, :, :, None, :] - ci[None] * pw_im[:t_, :, :, None, :]
    ca_im = cr[None] * pw_im[:t_, :, :, None, :] + ci[None] * pw_re[:t_, :, :, None, :]
    kern = (jnp.einsum('tdgop,dgpc->tdgoc', ca_re, bb_re, precision=HI)
            - jnp.einsum('tdgop,dgpc->tdgoc', ca_im, bb_im, precision=HI))
    tt = jnp.arange(t_)
    lag = tt[None, :] - tt[:, None]
    k_f = kern[jnp.clip(lag, 0, t_ - 1), 0] * (lag >= 0)[..., None, None, None].astype(F32)
    k_b = kern[jnp.clip(-lag, 0, t_ - 1), 1] * (lag <= 0)[..., None, None, None].astype(F32)
    w_intra = (k_f + k_b).transpose(2, 0, 4, 1, 3)
    skip = (jnp.eye(t_, dtype=F32)[None, :, None, :, None]
            * jnp.eye(c_, dtype=F32)[None, None, :, None, :]
            * d_skip.astype(F32).reshape(g_, 1, c_, 1, 1))
    w_intra = (w_intra + skip).reshape(g_ // GROUP_BLOCK, GROUP_BLOCK, t_ * c_, t_ * c_)

    eye_g = jnp.eye(GROUP_BLOCK, dtype=F32)

    def summ(pw_r, pw_i, d):
        re = pw_r[..., None] * bb_re[d][None] - pw_i[..., None] * bb_im[d][None]
        im = pw_r[..., None] * bb_im[d][None] + pw_i[..., None] * bb_re[d][None]
        return re.transpose(1, 0, 3, 2), im.transpose(1, 0, 3, 2)

    f_re, f_im = summ(pw_re[:t_, 0][::-1], pw_im[:t_, 0][::-1], 0)
    r_re, r_im = summ(pw_re[:t_, 1], pw_im[:t_, 1], 1)
    wb = jnp.stack([f_re, f_im, r_re, r_im], axis=3)
    wb = wb.reshape(g_ // GROUP_BLOCK, GROUP_BLOCK, t_ * c_, 4, p_)
    wb = jnp.einsum('bgkcp,gh->bgkchp', wb, eye_g)
    wb = wb.reshape(g_ // GROUP_BLOCK, GROUP_BLOCK * t_ * c_, 4 * CW)

    def readout(pw_r, pw_i, d):
        re = cr[d][None] * pw_r[:, :, None, :] - ci[d][None] * pw_i[:, :, None, :]
        im = cr[d][None] * pw_i[:, :, None, :] + ci[d][None] * pw_r[:, :, None, :]
        return re.transpose(1, 3, 0, 2), -im.transpose(1, 3, 0, 2)

    of_re, of_im = readout(pw_re[1:, 0], pw_im[1:, 0], 0)
    ob_re, ob_im = readout(pw_re[1:, 1][::-1], pw_im[1:, 1][::-1], 1)
    wc = jnp.stack([of_re, of_im, ob_re, ob_im], axis=1)
    wc = wc.reshape(g_ // GROUP_BLOCK, GROUP_BLOCK, 4, p_, t_ * c_)
    wc = jnp.einsum('bgcpk,gh->bcgphk', wc, eye_g)
    wc = wc.reshape(g_ // GROUP_BLOCK, 4 * CW, GROUP_BLOCK * t_ * c_)

    a16 = jnp.stack([pw_re[t_, 0], pw_im[t_, 0], pw_re[t_, 1], pw_im[t_, 1]], axis=0)
    a16 = a16.reshape(4, g_ // GROUP_BLOCK, CW).transpose(1, 0, 2)
    return w_intra.astype(BF16), wb.astype(BF16), wc.astype(BF16), a16


def _merge_kernel(att_ref, z_ref, ga_ref, gs_ref, wp_ref, wa_ref, wb_ref, o_ref):
    z = z_ref[...]
    y_attn = jnp.dot(att_ref[...], wp_ref[...], preferred_element_type=F32)
    y_ssm = (jnp.dot(z, wa_ref[...], preferred_element_type=F32)
             * jax.nn.sigmoid(jnp.dot(z, wb_ref[...], preferred_element_type=F32)))
    g_a = jax.nn.sigmoid(ga_ref[...].astype(F32))
    g_s = jax.nn.sigmoid(gs_ref[...].astype(F32))
    o_ref[...] = (g_a * y_attn + g_s * y_ssm).astype(BF16)


def _merge(att, z, gates, wp, wa, wb):
    n = att.shape[0]
    tm = min(1024, n)
    tn = 512
    nj = D_MODEL // tn
    return pl.pallas_call(
        _merge_kernel,
        grid=(n // tm, nj),
        in_specs=[
            pl.BlockSpec((tm, ATTN_W), lambda i, j: (i, 0)),
            pl.BlockSpec((tm, SSM_W), lambda i, j: (i, 0)),
            pl.BlockSpec((tm, tn), lambda i, j: (i, j)),
            pl.BlockSpec((tm, tn), lambda i, j: (i, j + nj)),
            pl.BlockSpec((ATTN_W, tn), lambda i, j: (0, j)),
            pl.BlockSpec((SSM_W, tn), lambda i, j: (0, j)),
            pl.BlockSpec((SSM_W, tn), lambda i, j: (0, j)),
        ],
        out_specs=pl.BlockSpec((tm, tn), lambda i, j: (i, j)),
        out_shape=jax.ShapeDtypeStruct((n, D_MODEL), BF16),
        compiler_params=_params("parallel", "arbitrary"),
        name="merge",
    )(att, z, gates, gates, wp, wa, wb)


def _mm_res_kernel(a_ref, w_ref, r_ref, o_ref):
    o_ref[...] = r_ref[...] + jnp.dot(a_ref[...], w_ref[...], preferred_element_type=F32)


def _mm_res(a, w, res):
    n, kdim = a.shape
    cols = w.shape[1]
    tm = min(1024, n)
    tn = 512
    return pl.pallas_call(
        _mm_res_kernel,
        grid=(n // tm, cols // tn),
        in_specs=[
            pl.BlockSpec((tm, kdim), lambda i, j: (i, 0)),
            pl.BlockSpec((kdim, tn), lambda i, j: (0, j)),
            pl.BlockSpec((tm, tn), lambda i, j: (i, j)),
        ],
        out_specs=pl.BlockSpec((tm, tn), lambda i, j: (i, j)),
        out_shape=jax.ShapeDtypeStruct((n, cols), F32),
        compiler_params=_params("parallel", "arbitrary"),
        name="matmul_residual",
    )(a, w, res)


def _norm_mm_kernel(x_ref, g_ref, w_ref, o_ref, xn_ref, *, scale):
    @pl.when(pl.program_id(1) == 0)
    def _():
        xn_ref[...] = _rms(x_ref[...], g_ref[...]).astype(BF16)

    acc = jnp.dot(xn_ref[...], w_ref[...], preferred_element_type=F32)
    o_ref[...] = (acc * scale).astype(BF16)


def _norm_mm(x, g, w, scale):
    n, kdim = x.shape
    cols = w.shape[1]
    tm = min(512, n)
    tn = 1024
    return pl.pallas_call(
        functools.partial(_norm_mm_kernel, scale=scale),
        grid=(n // tm, cols // tn),
        in_specs=[
            pl.BlockSpec((tm, kdim), lambda i, j: (i, 0)),
            pl.BlockSpec((1, kdim), lambda i, j: (0, 0)),
            pl.BlockSpec((kdim, tn), lambda i, j: (0, j)),
        ],
        out_specs=pl.BlockSpec((tm, tn), lambda i, j: (i, j)),
        out_shape=jax.ShapeDtypeStruct((n, cols), BF16),
        scratch_shapes=[pltpu.VMEM((tm, kdim), BF16)],
        compiler_params=_params("parallel", "arbitrary"),
        name="norm_matmul",
    )(x, g, w)


def _cross_attn_kernel(q_ref, kv_ref, o_ref):
    nt = (((1,), (1,)), ((), ()))
    for h in range(X_HEADS):
        cols = slice(h * X_HEAD_DIM, (h + 1) * X_HEAD_DIM)
        vcols = slice(D_MODEL + h * X_HEAD_DIM, D_MODEL + (h + 1) * X_HEAD_DIM)
        s = lax.dot_general(q_ref[0, :, cols], kv_ref[0, :, cols], nt, preferred_element_type=F32)
        p = jnp.exp(s - jnp.max(s, axis=-1, keepdims=True))
        l = jnp.sum(p, axis=-1, keepdims=True)
        o = jnp.dot(p.astype(BF16), kv_ref[0, :, vcols], preferred_element_type=F32)
        o_ref[0, :, cols] = (o / l).astype(BF16)


def _cross_attn(qc3, kv3):
    b, seq, _ = qc3.shape
    m = kv3.shape[1]
    tq = min(512, seq)
    return pl.pallas_call(
        _cross_attn_kernel,
        grid=(b, seq // tq),
        in_specs=[
            pl.BlockSpec((1, tq, D_MODEL), lambda b_, i: (b_, i, 0)),
            pl.BlockSpec((1, m, 2 * D_MODEL), lambda b_, i: (b_, 0, 0)),
        ],
        out_specs=pl.BlockSpec((1, tq, D_MODEL), lambda b_, i: (b_, i, 0)),
        out_shape=jax.ShapeDtypeStruct((b, seq, D_MODEL), BF16),
        compiler_params=_params("parallel", "arbitrary"),
        name="cross_attn",
    )(qc3, kv3)


def _mlp_kernel(x_ref, g_ref, wu_ref, wd_ref, gf_ref, o_ref, hn_ref, acc_ref):
    f = pl.program_id(1)

    @pl.when(f == 0)
    def _():
        hn_ref[...] = _rms(x_ref[...], g_ref[...]).astype(BF16)
        acc_ref[...] = jnp.zeros_like(acc_ref)

    h = jnp.maximum(jnp.dot(hn_ref[...], wu_ref[...], preferred_element_type=F32), 0.0)
    acc_ref[...] += jnp.dot((h * h).astype(BF16), wd_ref[...], preferred_element_type=F32)

    @pl.when(f == pl.num_programs(1) - 1)
    def _():
        o_ref[...] = _rms(x_ref[...] + acc_ref[...], gf_ref[...])


def _mlp(x, g, wu, wd, gf):
    n = x.shape[0]
    tm = min(512, n)
    tf = 512
    return pl.pallas_call(
        _mlp_kernel,
        grid=(n // tm, D_FF // tf),
        in_specs=[
            pl.BlockSpec((tm, D_MODEL), lambda i, f: (i, 0)),
            pl.BlockSpec((1, D_MODEL), lambda i, f: (0, 0)),
            pl.BlockSpec((D_MODEL, tf), lambda i, f: (0, f)),
            pl.BlockSpec((tf, D_MODEL), lambda i, f: (f, 0)),
            pl.BlockSpec((1, D_MODEL), lambda i, f: (0, 0)),
        ],
        out_specs=pl.BlockSpec((tm, D_MODEL), lambda i, f: (i, 0)),
        out_shape=jax.ShapeDtypeStruct((n, D_MODEL), F32),
        scratch_shapes=[pltpu.VMEM((tm, D_MODEL), BF16), pltpu.VMEM((tm, D_MODEL), F32)],
        compiler_params=_params("parallel", "arbitrary"),
        name="mlp_final_norm",
    )(x, g, wu, wd, gf)


def _rope_tables(seq):
    half = HEAD_DIM // 2
    inv = ROPE_THETA ** (-jnp.arange(0, HEAD_DIM, 2, dtype=F32) / HEAD_DIM)
    ang = jnp.arange(seq, dtype=F32)[:, None] * inv[None, :]
    cos, sin = jnp.cos(ang), jnp.sin(ang)
    cos_t = jnp.tile(cos, (1, LANES // half))
    sin_t = jnp.concatenate([-sin, -sin, sin, sin], axis=1)
    return cos_t, sin_t


def _qk_perm():
    half = HEAD_DIM // 2
    idx = []
    for h in range(N_HEADS):
        for part in range(2):
            for m in range(2):
                base = h * 2 * HEAD_DIM + m * HEAD_DIM + part * half
                idx.extend(range(base, base + half))
    return jnp.asarray(idx, dtype=jnp.int32)


def _prepare(w):
    (norm_mix, w_in, diff_lambda, subln, w_attn_proj,
     lam_re, lam_im, log_dt, b_re, b_im, c_re, c_im, d_skip,
     w_glu_a, w_glu_b, w_mix_out,
     norm_cross, norm_mem, w_q_cross, w_kv_cross, w_o_cross,
     norm_mlp, w_mlp_up, w_mlp_down, norm_final) = w
    perm = _qk_perm()
    w_in0 = w_in[0]
    w_in_p = jnp.concatenate([w_in0[:, perm], w_in0[:, ATTN_W + perm], w_in0[:, 2 * ATTN_W:]],
                             axis=1).astype(BF16)
    row = lambda v: v.astype(F32).reshape(1, -1)
    return dict(
        norm_mix=row(norm_mix[0]), w_in=w_in_p, diff_lambda=diff_lambda[0].astype(F32),
        subln=row(subln[0]), w_attn_proj=w_attn_proj[0].astype(BF16),
        ssm=_ssm_operators(lam_re[0], lam_im[0], log_dt[0], b_re[0], b_im[0], c_re[0], c_im[0],
                           d_skip[0]),
        w_glu_a=w_glu_a[0].astype(BF16), w_glu_b=w_glu_b[0].astype(BF16),
        w_mix_out=w_mix_out[0].astype(BF16),
        norm_cross=row(norm_cross[0]), norm_mem=row(norm_mem[0]),
        w_q_cross=w_q_cross[0].astype(BF16), w_kv_cross=w_kv_cross[0].astype(BF16),
        w_o_cross=w_o_cross[0].astype(BF16),
        norm_mlp=row(norm_mlp[0]), w_mlp_up=w_mlp_up[0].astype(BF16),
        w_mlp_down=w_mlp_down[0].astype(BF16), norm_final=row(norm_final),
    )


def _encode(x, mem, p):
    b, seq, _ = x.shape
    assert b == 4 and seq % (2 * CHUNK) == 0
    n = b * seq
    nc = seq // CHUNK
    x2 = x.reshape(n, D_MODEL)
    cos_t, sin_t = _rope_tables(seq)

    q, k, v, u, gates = _in_proj(x2, p["norm_mix"], p["w_in"], cos_t, sin_t, seq)
    att = _diff_attn(q.reshape(b, seq, ATTN_W), k.reshape(b, seq, ATTN_W),
                     v.reshape(b, seq, ATTN_W), p["diff_lambda"], p["subln"])

    ut = u.reshape(b, nc, CHUNK, N_GROUPS, GROUP_SIZE).transpose(1, 0, 3, 2, 4).reshape(nc * b, -1)
    zt = _ssm(ut, *p["ssm"])
    z = zt.reshape(nc, b, N_GROUPS, CHUNK, GROUP_SIZE).transpose(1, 0, 3, 2, 4).reshape(n, SSM_W)

    mixed = _merge(att.reshape(n, ATTN_W), z, gates, p["w_attn_proj"], p["w_glu_a"], p["w_glu_b"])
    x2 = _mm_res(mixed, p["w_mix_out"], x2)

    qc = _norm_mm(x2, p["norm_cross"], p["w_q_cross"], X_HEAD_DIM ** -0.5)
    m_tok = mem.shape[1]
    kv = _norm_mm(mem.reshape(b * m_tok, D_MODEL), p["norm_mem"], p["w_kv_cross"], 1.0)
    oc = _cross_attn(qc.reshape(b, seq, D_MODEL), kv.reshape(b, m_tok, 2 * D_MODEL))
    x2 = _mm_res(oc.reshape(n, D_MODEL), p["w_o_cross"], x2)

    out = _mlp(x2, p["norm_mlp"], p["w_mlp_up"], p["w_mlp_down"], p["norm_final"])
    return out.reshape(b, seq, D_MODEL)


def kernel(x_prompt, x_sample, mem_prompt, mem_sample, norm_mix, w_in, diff_lambda, subln, w_attn_proj, ssm_lambda_re, ssm_lambda_im, ssm_log_dt, ssm_b_re, ssm_b_im, ssm_c_re, ssm_c_im, ssm_d, w_glu_a, w_glu_b, w_mix_out, norm_cross, norm_mem, w_q_cross, w_kv_cross, w_o_cross, norm_mlp, w_mlp_up, w_mlp_down, norm_final):
    p = _prepare((norm_mix, w_in, diff_lambda, subln, w_attn_proj,
                  ssm_lambda_re, ssm_lambda_im, ssm_log_dt, ssm_b_re, ssm_b_im, ssm_c_re, ssm_c_im,
                  ssm_d, w_glu_a, w_glu_b, w_mix_out,
                  norm_cross, norm_mem, w_q_cross, w_kv_cross, w_o_cross,
                  norm_mlp, w_mlp_up, w_mlp_down, norm_final))
    return (_encode(x_prompt, mem_prompt, p), _encode(x_sample, mem_sample, p))
```

```python
import functools
import math

import jax
import jax.numpy as jnp
from jax import lax
from jax.experimental import pallas as pl
from jax.experimental.pallas import tpu as pltpu

D_MODEL = 2048
N_HEADS = 8
HEAD_DIM = 64
ATTN_W = N_HEADS * 2 * HEAD_DIM
SSM_W = D_MODEL // 2
GROUP_SIZE = 16
N_GROUPS = SSM_W // GROUP_SIZE
STATE = 64
IN_COLS = 3 * ATTN_W + SSM_W + 2 * D_MODEL
D_FF = 4 * D_MODEL
X_HEADS = 4
X_HEAD_DIM = D_MODEL // X_HEADS
ROPE_THETA = 10000.0
EPS = 1e-6
LAM_INIT = 0.8 - 0.6 * math.exp(-0.3 * 0)
Q_SCALE = HEAD_DIM ** -0.5 * math.log2(math.e)

CHUNK = 16
GROUP_BLOCK = 4
CW = GROUP_BLOCK * STATE
LANES = 128
VMEM_LIMIT = 56 * 1024 * 1024

F32 = jnp.float32
BF16 = jnp.bfloat16
HI = lax.Precision.HIGHEST


def _params(*sem):
    return pltpu.CompilerParams(dimension_semantics=sem, vmem_limit_bytes=VMEM_LIMIT)


def _rms(x, g):
    return x * lax.rsqrt(jnp.mean(x * x, axis=-1, keepdims=True) + EPS) * g


def _in_proj_kernel(x_ref, g_ref, w_ref, cos_ref, sin_ref,
                    q_ref, kt_ref, v_ref, u_ref, gate_ref, xn_ref):
    j = pl.program_id(1)

    @pl.when(j == 0)
    def _():
        xn_ref[...] = _rms(x_ref[...], g_ref[...]).astype(BF16)

    acc = jnp.dot(xn_ref[...], w_ref[...], preferred_element_type=F32)

    def rope(scale):
        cos = cos_ref[...]
        sin = sin_ref[...]
        outs = []
        for c in range(acc.shape[1] // LANES):
            xc = acc[:, c * LANES:(c + 1) * LANES]
            outs.append((xc * cos + pltpu.roll(xc, LANES // 2, axis=1) * sin) * scale)
        return jnp.concatenate(outs, axis=1)

    @pl.when(j == 0)
    def _():
        q_ref[...] = rope(Q_SCALE).astype(BF16)

    @pl.when(j == 1)
    def _():
        kt_ref[...] = rope(1.0).T.astype(BF16)

    @pl.when(j == 2)
    def _():
        v_ref[...] = acc.astype(BF16)

    @pl.when(j == 3)
    def _():
        u_ref[...] = acc.astype(BF16)

    @pl.when(j >= 4)
    def _():
        gate_ref[...] = acc.astype(BF16)


def _in_proj(x2, g, w, cos_t, sin_t, seq):
    n = x2.shape[0]
    tm = min(512, seq)
    tn = ATTN_W
    nj = IN_COLS // tn
    pos_blocks = seq // tm
    row = lambda i, j: (i, 0)
    out_sd = lambda cols: jax.ShapeDtypeStruct((n, cols), BF16)
    return pl.pallas_call(
        _in_proj_kernel,
        grid=(n // tm, nj),
        in_specs=[
            pl.BlockSpec((tm, D_MODEL), row),
            pl.BlockSpec((1, D_MODEL), lambda i, j: (0, 0)),
            pl.BlockSpec((D_MODEL, tn), lambda i, j: (0, j)),
            pl.BlockSpec((tm, LANES), lambda i, j: (i % pos_blocks, 0)),
            pl.BlockSpec((tm, LANES), lambda i, j: (i % pos_blocks, 0)),
        ],
        out_specs=[
            pl.BlockSpec((tm, tn), row),
            pl.BlockSpec((tn, tm), lambda i, j: (0, i)),
            pl.BlockSpec((tm, tn), row),
            pl.BlockSpec((tm, tn), row),
            pl.BlockSpec((tm, tn), lambda i, j: (i, jnp.maximum(j - 4, 0))),
        ],
        out_shape=[out_sd(ATTN_W), jax.ShapeDtypeStruct((ATTN_W, n), BF16), out_sd(ATTN_W),
                   out_sd(SSM_W), out_sd(2 * D_MODEL)],
        scratch_shapes=[pltpu.VMEM((tm, D_MODEL), BF16)],
        compiler_params=_params("parallel", "arbitrary"),
        name="in_proj",
    )(x2, g, w, cos_t, sin_t)


def _diff_attn_kernel(dl_ref, sub_ref, q_ref, kt_ref, v_ref, o_ref, s_ref, m_ref, acc_ref, *, tk):
    q = q_ref[0]
    seq = v_ref.shape[1]
    nk = seq // tk
    hw = 2 * HEAD_DIM
    lane = lax.broadcasted_iota(jnp.int32, q.shape, 1)
    first = ((lane // (HEAD_DIM // 2)) % 2) == 0
    zero = jnp.zeros_like(q)
    q_parts = (jnp.where(first, q, zero), jnp.where(first, zero, q))
    ones = jnp.ones((tk, hw), BF16)

    def scores(c, slot):
        kc = kt_ref[:, pl.ds(pl.multiple_of(c * tk, tk), tk)]
        for comp in range(2):
            s_ref[slot, comp] = jnp.dot(q_parts[comp], kc, preferred_element_type=F32)

    def accumulate(c, slot):
        vx = jnp.concatenate([v_ref[0, pl.ds(pl.multiple_of(c * tk, tk), tk), :], ones], axis=1)
        for comp in range(2):
            s = s_ref[slot, comp]
            m_old = m_ref[comp]
            m_new = jnp.maximum(m_old, jnp.max(s, axis=-1, keepdims=True))
            p = jnp.exp2(s - m_new).astype(BF16)
            m_ref[comp] = m_new
            acc_ref[comp] = (jnp.exp2(m_old - m_new) * acc_ref[comp]
                             + jnp.dot(p, vx, preferred_element_type=F32))

    m_ref[...] = jnp.full(m_ref.shape, -jnp.inf, F32)
    acc_ref[...] = jnp.zeros(acc_ref.shape, F32)
    scores(0, 0)

    def pair(i, _):
        scores(2 * i + 1, 1)
        accumulate(2 * i, 0)
        scores(jnp.minimum(2 * i + 2, nk - 1), 0)
        accumulate(2 * i + 1, 1)
        return 0

    lax.fori_loop(0, nk // 2, pair, 0)

    dl = dl_ref[...]
    lam = (jnp.exp(jnp.sum(dl[0:1] * dl[1:2], axis=-1, keepdims=True))
           - jnp.exp(jnp.sum(dl[2:3] * dl[3:4], axis=-1, keepdims=True)) + LAM_INIT)
    o = (acc_ref[0, :, :hw] / acc_ref[0, :, hw:]
         - lam * (acc_ref[1, :, :hw] / acc_ref[1, :, hw:]))
    o_ref[0] = (_rms(o, sub_ref[...]) * (1.0 - LAM_INIT)).astype(BF16)


def _diff_attn(q3, kt, v3, dl, sub):
    b, seq, _ = q3.shape
    tq = min(512, seq)
    tk = min(1024, seq // 2)
    hw = 2 * HEAD_DIM
    return pl.pallas_call(
        functools.partial(_diff_attn_kernel, tk=tk),
        grid=(b, N_HEADS, seq // tq),
        in_specs=[
            pl.BlockSpec((4, HEAD_DIM), lambda b_, h, i: (0, 0)),
            pl.BlockSpec((1, hw), lambda b_, h, i: (0, 0)),
            pl.BlockSpec((1, tq, hw), lambda b_, h, i: (b_, i, h)),
            pl.BlockSpec((hw, seq), lambda b_, h, i: (h, b_)),
            pl.BlockSpec((1, seq, hw), lambda b_, h, i: (b_, 0, h)),
        ],
        out_specs=pl.BlockSpec((1, tq, hw), lambda b_, h, i: (b_, i, h)),
        out_shape=jax.ShapeDtypeStruct((b, seq, ATTN_W), BF16),
        scratch_shapes=[pltpu.VMEM((2, 2, tq, tk), F32), pltpu.VMEM((2, tq, 1), F32),
                        pltpu.VMEM((2, tq, 2 * hw), F32)],
        compiler_params=_params("parallel", "parallel", "arbitrary"),
        name="diff_attn",
    )(dl, sub, q3, kt, v3)


def _ssm_kernel(u_ref, wi_ref, wb_ref, wc_ref, a_ref, z_ref, s_ref, *, rb):
    rows = u_ref.shape[0]
    n_tiles = rows // 8

    def summarise(r, _):
        rs = pl.ds(pl.multiple_of(r * rb, rb), rb)
        s_ref[rs, :] = jnp.dot(u_ref[rs, :], wb_ref[0], preferred_element_type=F32)
        return 0

    lax.fori_loop(0, rows // rb, summarise, 0)

    a = a_ref[0]
    af_re, af_im, ab_re, ab_im = (a[i:i + 1, :] for i in range(4))
    low = lax.broadcasted_iota(jnp.int32, (8, CW), 0) < 4

    def cmul_add(ar, ai, hr, hi, tr, ti):
        return ar * hr - ai * hi + tr, ar * hi + ai * hr + ti

    def half_swap(x):
        return pltpu.roll(x, 4, axis=0)

    def scan(i, carry):
        hf_re, hf_im, hb_re, hb_im = carry
        rs = pl.ds(pl.multiple_of(i * 8, 8), 8)
        t_re = s_ref[rs, 0:CW]
        t_im = s_ref[rs, CW:2 * CW]
        m_re, m_im = cmul_add(af_re, af_im, hf_re, hf_im, t_re, t_im)
        m_re, m_im = half_swap(m_re), half_swap(m_im)
        s_ref[rs, 0:CW] = jnp.where(low, hf_re, m_re)
        s_ref[rs, CW:2 * CW] = jnp.where(low, hf_im, m_im)
        n_re, n_im = cmul_add(af_re, af_im, m_re, m_im, t_re, t_im)
        hf_re, hf_im = half_swap(n_re), half_swap(n_im)
        rs = pl.ds(pl.multiple_of((n_tiles - 1 - i) * 8, 8), 8)
        t_re = s_ref[rs, 2 * CW:3 * CW]
        t_im = s_ref[rs, 3 * CW:4 * CW]
        m_re, m_im = cmul_add(ab_re, ab_im, hb_re, hb_im, t_re, t_im)
        m_re, m_im = half_swap(m_re), half_swap(m_im)
        s_ref[rs, 2 * CW:3 * CW] = jnp.where(low, m_re, hb_re)
        s_ref[rs, 3 * CW:4 * CW] = jnp.where(low, m_im, hb_im)
        n_re, n_im = cmul_add(ab_re, ab_im, m_re, m_im, t_re, t_im)
        hb_re, hb_im = half_swap(n_re), half_swap(n_im)
        return hf_re, hf_im, hb_re, hb_im

    zero = jnp.zeros((8, CW), F32)
    lax.fori_loop(0, n_tiles, scan, (zero, zero, zero, zero))

    gw = CHUNK * GROUP_SIZE

    def emit(r, _):
        rs = pl.ds(pl.multiple_of(r * rb, rb), rb)
        h = s_ref[rs, :].astype(BF16)
        for g in range(GROUP_BLOCK):
            cols = slice(g * gw, (g + 1) * gw)
            y = jnp.dot(u_ref[rs, cols], wi_ref[0, g], preferred_element_type=F32)
            y = y + jnp.dot(h, wc_ref[0, :, cols], preferred_element_type=F32)
            z = 0.5 * y * (1.0 + jnp.tanh(0.7978845608028654 * (y + 0.044715 * (y * y * y))))
            z_ref[rs, cols] = z.astype(BF16)
        return 0

    lax.fori_loop(0, rows // rb, emit, 0)


def _ssm(ut, wi, wb, wc, a16):
    rows = ut.shape[0]
    rb = min(256, rows)
    bw = GROUP_BLOCK * CHUNK * GROUP_SIZE
    return pl.pallas_call(
        functools.partial(_ssm_kernel, rb=rb),
        grid=(N_GROUPS // GROUP_BLOCK,),
        in_specs=[
            pl.BlockSpec((rows, bw), lambda g: (0, g)),
            pl.BlockSpec((1, GROUP_BLOCK, CHUNK * GROUP_SIZE, CHUNK * GROUP_SIZE),
                         lambda g: (g, 0, 0, 0)),
            pl.BlockSpec((1, bw, 4 * CW), lambda g: (g, 0, 0)),
            pl.BlockSpec((1, 4 * CW, bw), lambda g: (g, 0, 0)),
            pl.BlockSpec((1, 4, CW), lambda g: (g, 0, 0)),
        ],
        out_specs=pl.BlockSpec((rows, bw), lambda g: (0, g)),
        out_shape=jax.ShapeDtypeStruct(ut.shape, BF16),
        scratch_shapes=[pltpu.VMEM((rows, 4 * CW), F32)],
        compiler_params=_params("parallel"),
        name="s5_scan",
    )(ut, wi, wb, wc, a16)


def _ssm_operators(lam_re, lam_im, log_dt, b_re, b_im, c_re, c_im, d_skip):
    t_ = CHUNK
    g_, p_, c_ = N_GROUPS, STATE, GROUP_SIZE
    lr, li = lam_re.astype(F32), lam_im.astype(F32)
    dt = jnp.exp(log_dt.astype(F32))[..., None]
    mag = jnp.exp(lr * dt)
    ab_re, ab_im = mag * jnp.cos(li * dt), mag * jnp.sin(li * dt)
    den = lr * lr + li * li
    n_re, n_im = ab_re - 1.0, ab_im
    k_re = (n_re * lr + n_im * li) / den
    k_im = (n_im * lr - n_re * li) / den
    br, bi = b_re.astype(F32), b_im.astype(F32)
    bb_re = k_re[..., None] * br - k_im[..., None] * bi
    bb_im = k_re[..., None] * bi + k_im[..., None] * br
    cr, ci = c_re.astype(F32), c_im.astype(F32)

    ks = jnp.arange(t_ + 1, dtype=F32)[:, None, None, None]
    pmag = jnp.exp(ks * (lr * dt))
    pw_re, pw_im = pmag * jnp.cos(ks * (li * dt)), pmag * jnp.sin(ks * (li * dt))

    ca_re = cr[None] * pw_re[:t_, :, :, None, :] - ci[None] * pw_im[:t_, :, :, None, :]
    ca_im = cr[None] * pw_im[:t_, :, :, None, :] + ci[None] * pw_re[:t_, :, :, None, :]
    kern = (jnp.einsum('tdgop,dgpc->tdgoc', ca_re, bb_re, precision=HI)
            - jnp.einsum('tdgop,dgpc->tdgoc', ca_im, bb_im, precision=HI))
    tt = jnp.arange(t_)
    lag = tt[None, :] - tt[:, None]
    k_f = kern[jnp.clip(lag, 0, t_ - 1), 0] * (lag >= 0)[..., None, None, None].astype(F32)
    k_b = kern[jnp.clip(-lag, 0, t_ - 1), 1] * (lag <= 0)[..., None, None, None].astype(F32)
    w_intra = (k_f + k_b).transpose(2, 0, 4, 1, 3)
    skip = (jnp.eye(t_, dtype=F32)[None, :, None, :, None]
            * jnp.eye(c_, dtype=F32)[None, None, :, None, :]
            * d_skip.astype(F32).reshape(g_, 1, c_, 1, 1))
    w_intra = (w_intra + skip).reshape(g_ // GROUP_BLOCK, GROUP_BLOCK, t_ * c_, t_ * c_)

    eye_g = jnp.eye(GROUP_BLOCK, dtype=F32)

    def summ(pw_r, pw_i, d):
        re = pw_r[..., None] * bb_re[d][None] - pw_i[..., None] * bb_im[d][None]
        im = pw_r[..., None] * bb_im[d][None] + pw_i[..., None] * bb_re[d][None]
        return re.transpose(1, 0, 3, 2), im.transpose(1, 0, 3, 2)

    f_re, f_im = summ(pw_re[:t_, 0][::-1], pw_im[:t_, 0][::-1], 0)
    r_re, r_im = summ(pw_re[:t_, 1], pw_im[:t_, 1], 1)
    wb = jnp.stack([f_re, f_im, r_re, r_im], axis=3)
    wb = wb.reshape(g_ // GROUP_BLOCK, GROUP_BLOCK, t_ * c_, 4, p_)
    wb = jnp.einsum('bgkcp,gh->bgkchp', wb, eye_g)
    wb = wb.reshape(g_ // GROUP_BLOCK, GROUP_BLOCK * t_ * c_, 4 * CW)

    def readout(pw_r, pw_i, d):
        re = cr[d][None] * pw_r[:, :, None, :] - ci[d][None] * pw_i[:, :, None, :]
        im = cr[d][None] * pw_i[:, :, None, :] + ci[d][None] * pw_r[:, :, None, :]
        return re.transpose(1, 3, 0, 2), -im.transpose(1, 3, 0, 2)

    of_re, of_im = readout(pw_re[1:, 0], pw_im[1:, 0], 0)
    ob_re, ob_im = readout(pw_re[1:, 1][::-1], pw_im[1:, 1][::-1], 1)
    wc = jnp.stack([of_re, of_im, ob_re, ob_im], axis=1)
    wc = wc.reshape(g_ // GROUP_BLOCK, GROUP_BLOCK, 4, p_, t_ * c_)
    wc = jnp.einsum('bgcpk,gh->bcgphk', wc, eye_g)
    wc = wc.reshape(g_ // GROUP_BLOCK, 4 * CW, GROUP_BLOCK * t_ * c_)

    a16 = jnp.stack([pw_re[t_, 0], pw_im[t_, 0], pw_re[t_, 1], pw_im[t_, 1]], axis=0)
    a16 = a16.reshape(4, g_ // GROUP_BLOCK, CW).transpose(1, 0, 2)
    return w_intra.astype(BF16), wb.astype(BF16), wc.astype(BF16), a16


def _merge_kernel(att_ref, z_ref, ga_ref, gs_ref, wp_ref, wa_ref, wb_ref, o_ref):
    z = z_ref[...]
    y_attn = jnp.dot(att_ref[...], wp_ref[...], preferred_element_type=F32)
    y_ssm = (jnp.dot(z, wa_ref[...], preferred_element_type=F32)
             * jax.nn.sigmoid(jnp.dot(z, wb_ref[...], preferred_element_type=F32)))
    g_a = jax.nn.sigmoid(ga_ref[...].astype(F32))
    g_s = jax.nn.sigmoid(gs_ref[...].astype(F32))
    o_ref[...] = (g_a * y_attn + g_s * y_ssm).astype(BF16)


def _merge(att, z, gates, wp, wa, wb):
    n = att.shape[0]
    tm = min(1024, n)
    tn = 512
    nj = D_MODEL // tn
    return pl.pallas_call(
        _merge_kernel,
        grid=(n // tm, nj),
        in_specs=[
            pl.BlockSpec((tm, ATTN_W), lambda i, j: (i, 0)),
            pl.BlockSpec((tm, SSM_W), lambda i, j: (i, 0)),
            pl.BlockSpec((tm, tn), lambda i, j: (i, j)),
            pl.BlockSpec((tm, tn), lambda i, j: (i, j + nj)),
            pl.BlockSpec((ATTN_W, tn), lambda i, j: (0, j)),
            pl.BlockSpec((SSM_W, tn), lambda i, j: (0, j)),
            pl.BlockSpec((SSM_W, tn), lambda i, j: (0, j)),
        ],
        out_specs=pl.BlockSpec((tm, tn), lambda i, j: (i, j)),
        out_shape=jax.ShapeDtypeStruct((n, D_MODEL), BF16),
        compiler_params=_params("parallel", "arbitrary"),
        name="merge",
    )(att, z, gates, gates, wp, wa, wb)


def _mm_res_kernel(a_ref, w_ref, r_ref, o_ref):
    o_ref[...] = r_ref[...] + jnp.dot(a_ref[...], w_ref[...], preferred_element_type=F32)


def _mm_res(a, w, res):
    n, kdim = a.shape
    cols = w.shape[1]
    tm = min(1024, n)
    tn = 512
    return pl.pallas_call(
        _mm_res_kernel,
        grid=(n // tm, cols // tn),
        in_specs=[
            pl.BlockSpec((tm, kdim), lambda i, j: (i, 0)),
            pl.BlockSpec((kdim, tn), lambda i, j: (0, j)),
            pl.BlockSpec((tm, tn), lambda i, j: (i, j)),
        ],
        out_specs=pl.BlockSpec((tm, tn), lambda i, j: (i, j)),
        out_shape=jax.ShapeDtypeStruct((n, cols), F32),
        compiler_params=_params("parallel", "arbitrary"),
        name="matmul_residual",
    )(a, w, res)


def _norm_mm_kernel(x_ref, g_ref, w_ref, o_ref, xn_ref, *, scale):
    @pl.when(pl.program_id(1) == 0)
    def _():
        xn_ref[...] = _rms(x_ref[...], g_ref[...]).astype(BF16)

    acc = jnp.dot(xn_ref[...], w_ref[...], preferred_element_type=F32)
    o_ref[...] = (acc * scale).astype(BF16)


def _norm_mm(x, g, w, scale):
    n, kdim = x.shape
    cols = w.shape[1]
    tm = min(512, n)
    tn = 1024
    return pl.pallas_call(
        functools.partial(_norm_mm_kernel, scale=scale),
        grid=(n // tm, cols // tn),
        in_specs=[
            pl.BlockSpec((tm, kdim), lambda i, j: (i, 0)),
            pl.BlockSpec((1, kdim), lambda i, j: (0, 0)),
            pl.BlockSpec((kdim, tn), lambda i, j: (0, j)),
        ],
        out_specs=pl.BlockSpec((tm, tn), lambda i, j: (i, j)),
        out_shape=jax.ShapeDtypeStruct((n, cols), BF16),
        scratch_shapes=[pltpu.VMEM((tm, kdim), BF16)],
        compiler_params=_params("parallel", "arbitrary"),
        name="norm_matmul",
    )(x, g, w)


def _cross_attn_kernel(q_ref, kv_ref, o_ref):
    nt = (((1,), (1,)), ((), ()))
    for h in range(X_HEADS):
        cols = slice(h * X_HEAD_DIM, (h + 1) * X_HEAD_DIM)
        vcols = slice(D_MODEL + h * X_HEAD_DIM, D_MODEL + (h + 1) * X_HEAD_DIM)
        s = lax.dot_general(q_ref[0, :, cols], kv_ref[0, :, cols], nt, preferred_element_type=F32)
        p = jnp.exp(s - jnp.max(s, axis=-1, keepdims=True))
        l = jnp.sum(p, axis=-1, keepdims=True)
        o = jnp.dot(p.astype(BF16), kv_ref[0, :, vcols], preferred_element_type=F32)
        o_ref[0, :, cols] = (o / l).astype(BF16)


def _cross_attn(qc3, kv3):
    b, seq, _ = qc3.shape
    m = kv3.shape[1]
    tq = min(512, seq)
    return pl.pallas_call(
        _cross_attn_kernel,
        grid=(b, seq // tq),
        in_specs=[
            pl.BlockSpec((1, tq, D_MODEL), lambda b_, i: (b_, i, 0)),
            pl.BlockSpec((1, m, 2 * D_MODEL), lambda b_, i: (b_, 0, 0)),
        ],
        out_specs=pl.BlockSpec((1, tq, D_MODEL), lambda b_, i: (b_, i, 0)),
        out_shape=jax.ShapeDtypeStruct((b, seq, D_MODEL), BF16),
        compiler_params=_params("parallel", "arbitrary"),
        name="cross_attn",
    )(qc3, kv3)


def _mlp_kernel(x_ref, g_ref, wu_ref, wd_ref, gf_ref, o_ref, hn_ref, acc_ref):
    f = pl.program_id(1)

    @pl.when(f == 0)
    def _():
        hn_ref[...] = _rms(x_ref[...], g_ref[...]).astype(BF16)
        acc_ref[...] = jnp.zeros_like(acc_ref)

    h = jnp.maximum(jnp.dot(hn_ref[...], wu_ref[...], preferred_element_type=F32), 0.0)
    acc_ref[...] += jnp.dot((h * h).astype(BF16), wd_ref[...], preferred_element_type=F32)

    @pl.when(f == pl.num_programs(1) - 1)
    def _():
        o_ref[...] = _rms(x_ref[...] + acc_ref[...], gf_ref[...])


def _mlp(x, g, wu, wd, gf):
    n = x.shape[0]
    tm = min(512, n)
    tf = 512
    return pl.pallas_call(
        _mlp_kernel,
        grid=(n // tm, D_FF // tf),
        in_specs=[
            pl.BlockSpec((tm, D_MODEL), lambda i, f: (i, 0)),
            pl.BlockSpec((1, D_MODEL), lambda i, f: (0, 0)),
            pl.BlockSpec((D_MODEL, tf), lambda i, f: (0, f)),
            pl.BlockSpec((tf, D_MODEL), lambda i, f: (f, 0)),
            pl.BlockSpec((1, D_MODEL), lambda i, f: (0, 0)),
        ],
        out_specs=pl.BlockSpec((tm, D_MODEL), lambda i, f: (i, 0)),
        out_shape=jax.ShapeDtypeStruct((n, D_MODEL), F32),
        scratch_shapes=[pltpu.VMEM((tm, D_MODEL), BF16), pltpu.VMEM((tm, D_MODEL), F32)],
        compiler_params=_params("parallel", "arbitrary"),
        name="mlp_final_norm",
    )(x, g, wu, wd, gf)


def _rope_tables(seq):
    half = HEAD_DIM // 2
    inv = ROPE_THETA ** (-jnp.arange(0, HEAD_DIM, 2, dtype=F32) / HEAD_DIM)
    ang = jnp.arange(seq, dtype=F32)[:, None] * inv[None, :]
    cos, sin = jnp.cos(ang), jnp.sin(ang)
    cos_t = jnp.tile(cos, (1, LANES // half))
    sin_t = jnp.concatenate([-sin, -sin, sin, sin], axis=1)
    return cos_t, sin_t


def _qk_perm():
    half = HEAD_DIM // 2
    idx = []
    for h in range(N_HEADS):
        for part in range(2):
            for m in range(2):
                base = h * 2 * HEAD_DIM + m * HEAD_DIM + part * half
                idx.extend(range(base, base + half))
    return jnp.asarray(idx, dtype=jnp.int32)


def _prepare(w):
    (norm_mix, w_in, diff_lambda, subln, w_attn_proj,
     lam_re, lam_im, log_dt, b_re, b_im, c_re, c_im, d_skip,
     w_glu_a, w_glu_b, w_mix_out,
     norm_cross, norm_mem, w_q_cross, w_kv_cross, w_o_cross,
     norm_mlp, w_mlp_up, w_mlp_down, norm_final) = w
    perm = _qk_perm()
    w_in0 = w_in[0]
    w_in_p = jnp.concatenate([w_in0[:, perm], w_in0[:, ATTN_W + perm], w_in0[:, 2 * ATTN_W:]],
                             axis=1).astype(BF16)
    row = lambda v: v.astype(F32).reshape(1, -1)
    return dict(
        norm_mix=row(norm_mix[0]), w_in=w_in_p, diff_lambda=diff_lambda[0].astype(F32),
        subln=row(subln[0]), w_attn_proj=w_attn_proj[0].astype(BF16),
        ssm=_ssm_operators(lam_re[0], lam_im[0], log_dt[0], b_re[0], b_im[0], c_re[0], c_im[0],
                           d_skip[0]),
        w_glu_a=w_glu_a[0].astype(BF16), w_glu_b=w_glu_b[0].astype(BF16),
        w_mix_out=w_mix_out[0].astype(BF16),
        norm_cross=row(norm_cross[0]), norm_mem=row(norm_mem[0]),
        w_q_cross=w_q_cross[0].astype(BF16), w_kv_cross=w_kv_cross[0].astype(BF16),
        w_o_cross=w_o_cross[0].astype(BF16),
        norm_mlp=row(norm_mlp[0]), w_mlp_up=w_mlp_up[0].astype(BF16),
        w_mlp_down=w_mlp_down[0].astype(BF16), norm_final=row(norm_final),
    )


def _encode(x, mem, p):
    b, seq, _ = x.shape
    assert b == 4 and seq % (2 * CHUNK) == 0
    n = b * seq
    nc = seq // CHUNK
    x2 = x.reshape(n, D_MODEL)
    cos_t, sin_t = _rope_tables(seq)

    q, kt, v, u, gates = _in_proj(x2, p["norm_mix"], p["w_in"], cos_t, sin_t, seq)
    att = _diff_attn(q.reshape(b, seq, ATTN_W), kt, v.reshape(b, seq, ATTN_W),
                     p["diff_lambda"], p["subln"])

    ut = u.reshape(b, nc, CHUNK, N_GROUPS, GROUP_SIZE).transpose(1, 0, 3, 2, 4).reshape(nc * b, -1)
    zt = _ssm(ut, *p["ssm"])
    z = zt.reshape(nc, b, N_GROUPS, CHUNK, GROUP_SIZE).transpose(1, 0, 3, 2, 4).reshape(n, SSM_W)

    mixed = _merge(att.reshape(n, ATTN_W), z, gates, p["w_attn_proj"], p["w_glu_a"], p["w_glu_b"])
    x2 = _mm_res(mixed, p["w_mix_out"], x2)

    qc = _norm_mm(x2, p["norm_cross"], p["w_q_cross"], X_HEAD_DIM ** -0.5)
    m_tok = mem.shape[1]
    kv = _norm_mm(mem.reshape(b * m_tok, D_MODEL), p["norm_mem"], p["w_kv_cross"], 1.0)
    oc = _cross_attn(qc.reshape(b, seq, D_MODEL), kv.reshape(b, m_tok, 2 * D_MODEL))
    x2 = _mm_res(oc.reshape(n, D_MODEL), p["w_o_cross"], x2)

    out = _mlp(x2, p["norm_mlp"], p["w_mlp_up"], p["w_mlp_down"], p["norm_final"])
    return out.reshape(b, seq, D_MODEL)


def kernel(x_prompt, x_sample, mem_prompt, mem_sample, norm_mix, w_in, diff_lambda, subln, w_attn_proj, ssm_lambda_re, ssm_lambda_im, ssm_log_dt, ssm_b_re, ssm_b_im, ssm_c_re, ssm_c_im, ssm_d, w_glu_a, w_glu_b, w_mix_out, norm_cross, norm_mem, w_q_cross, w_kv_cross, w_o_cross, norm_mlp, w_mlp_up, w_mlp_down, norm_final):
    p = _prepare((norm_mix, w_in, diff_lambda, subln, w_attn_proj,
                  ssm_lambda_re, ssm_lambda_im, ssm_log_dt, ssm_b_re, ssm_b_im, ssm_c_re, ssm_c_im,
                  ssm_d, w_glu_a, w_glu_b, w_mix_out,
                  norm_cross, norm_mem, w_q_cross, w_kv_cross, w_o_cross,
                  norm_mlp, w_mlp_up, w_mlp_down, norm_final))
    return (_encode(x_prompt, mem_prompt, p), _encode(x_sample, mem_sample, p))
```

```python
import functools
import math

import jax
import jax.numpy as jnp
from jax import lax
from jax.experimental import pallas as pl
from jax.experimental.pallas import tpu as pltpu

D_MODEL = 2048
N_HEADS = 8
HEAD_DIM = 64
ATTN_W = N_HEADS * 2 * HEAD_DIM
SSM_W = D_MODEL // 2
GROUP_SIZE = 16
N_GROUPS = SSM_W // GROUP_SIZE
STATE = 64
IN_COLS = 3 * ATTN_W + SSM_W + 2 * D_MODEL
D_FF = 4 * D_MODEL
X_HEADS = 4
X_HEAD_DIM = D_MODEL // X_HEADS
ROPE_THETA = 10000.0
EPS = 1e-6
LAM_INIT = 0.8 - 0.6 * math.exp(-0.3 * 0)
Q_SCALE = HEAD_DIM ** -0.5 * math.log2(math.e)

CHUNK = 16
GROUP_BLOCK = 8
LANES = 128
VMEM_LIMIT = 56 * 1024 * 1024

F32 = jnp.float32
BF16 = jnp.bfloat16
HI = lax.Precision.HIGHEST


def _params(*sem):
    return pltpu.CompilerParams(dimension_semantics=sem, vmem_limit_bytes=VMEM_LIMIT)


def _rms(x, g):
    return x * lax.rsqrt(jnp.mean(x * x, axis=-1, keepdims=True) + EPS) * g


def _in_proj_kernel(x_ref, g_ref, w_ref, cos_ref, sin_ref,
                    q_ref, kt_ref, v_ref, u_ref, gate_ref, xn_ref):
    j = pl.program_id(1)

    @pl.when(j == 0)
    def _():
        xn_ref[...] = _rms(x_ref[...], g_ref[...]).astype(BF16)

    acc = jnp.dot(xn_ref[...], w_ref[...], preferred_element_type=F32)

    def rope(scale):
        cos = cos_ref[...]
        sin = sin_ref[...]
        outs = []
        for c in range(acc.shape[1] // LANES):
            xc = acc[:, c * LANES:(c + 1) * LANES]
            outs.append((xc * cos + pltpu.roll(xc, LANES // 2, axis=1) * sin) * scale)
        return jnp.concatenate(outs, axis=1)

    @pl.when(j == 0)
    def _():
        q_ref[...] = rope(Q_SCALE).astype(BF16)

    @pl.when(j == 1)
    def _():
        kt_ref[...] = rope(1.0).T.astype(BF16)

    @pl.when(j == 2)
    def _():
        v_ref[...] = acc.astype(BF16)

    @pl.when(j == 3)
    def _():
        u_ref[...] = acc

    @pl.when(j >= 4)
    def _():
        gate_ref[...] = acc.astype(BF16)


def _in_proj(x2, g, w, cos_t, sin_t, seq):
    n = x2.shape[0]
    tm = min(512, seq)
    tn = ATTN_W
    nj = IN_COLS // tn
    pos_blocks = seq // tm
    row = lambda i, j: (i, 0)
    out_sd = lambda cols: jax.ShapeDtypeStruct((n, cols), BF16)
    return pl.pallas_call(
        _in_proj_kernel,
        grid=(n // tm, nj),
        in_specs=[
            pl.BlockSpec((tm, D_MODEL), row),
            pl.BlockSpec((1, D_MODEL), lambda i, j: (0, 0)),
            pl.BlockSpec((D_MODEL, tn), lambda i, j: (0, j)),
            pl.BlockSpec((tm, LANES), lambda i, j: (i % pos_blocks, 0)),
            pl.BlockSpec((tm, LANES), lambda i, j: (i % pos_blocks, 0)),
        ],
        out_specs=[
            pl.BlockSpec((tm, tn), row),
            pl.BlockSpec((tn, tm), lambda i, j: (0, i)),
            pl.BlockSpec((tm, tn), row),
            pl.BlockSpec((tm, tn), row),
            pl.BlockSpec((tm, tn), lambda i, j: (i, jnp.maximum(j - 4, 0))),
        ],
        out_shape=[out_sd(ATTN_W), jax.ShapeDtypeStruct((ATTN_W, n), BF16), out_sd(ATTN_W),
                   jax.ShapeDtypeStruct((n, SSM_W), F32), out_sd(2 * D_MODEL)],
        scratch_shapes=[pltpu.VMEM((tm, D_MODEL), BF16)],
        compiler_params=_params("parallel", "arbitrary"),
        name="in_proj",
    )(x2, g, w, cos_t, sin_t)


def _diff_attn_kernel(dl_ref, sub_ref, q_ref, kt_ref, v_ref, o_ref, s_ref, m_ref, acc_ref, *, tk):
    q = q_ref[0]
    seq = v_ref.shape[1]
    nk = seq // tk
    hw = 2 * HEAD_DIM
    lane = lax.broadcasted_iota(jnp.int32, q.shape, 1)
    first = ((lane // (HEAD_DIM // 2)) % 2) == 0
    zero = jnp.zeros_like(q)
    q_parts = (jnp.where(first, q, zero), jnp.where(first, zero, q))
    ones = jnp.ones((tk, hw), BF16)

    def scores(c, slot):
        kc = kt_ref[:, pl.ds(pl.multiple_of(c * tk, tk), tk)]
        for comp in range(2):
            s_ref[slot, comp] = jnp.dot(q_parts[comp], kc, preferred_element_type=F32)

    def accumulate(c, slot):
        vx = jnp.concatenate([v_ref[0, pl.ds(pl.multiple_of(c * tk, tk), tk), :], ones], axis=1)
        for comp in range(2):
            s = s_ref[slot, comp]
            m_old = m_ref[comp]
            m_new = jnp.maximum(m_old, jnp.max(s, axis=-1, keepdims=True))
            p = jnp.exp2(s - m_new).astype(BF16)
            m_ref[comp] = m_new
            acc_ref[comp] = (jnp.exp2(m_old - m_new) * acc_ref[comp]
                             + jnp.dot(p, vx, preferred_element_type=F32))

    m_ref[...] = jnp.full(m_ref.shape, -jnp.inf, F32)
    acc_ref[...] = jnp.zeros(acc_ref.shape, F32)
    scores(0, 0)

    def pair(i, _):
        scores(2 * i + 1, 1)
        accumulate(2 * i, 0)
        scores(jnp.minimum(2 * i + 2, nk - 1), 0)
        accumulate(2 * i + 1, 1)
        return 0

    lax.fori_loop(0, nk // 2, pair, 0)

    dl = dl_ref[...]
    lam = (jnp.exp(jnp.sum(dl[0:1] * dl[1:2], axis=-1, keepdims=True))
           - jnp.exp(jnp.sum(dl[2:3] * dl[3:4], axis=-1, keepdims=True)) + LAM_INIT)
    o = (acc_ref[0, :, :hw] / acc_ref[0, :, hw:]
         - lam * (acc_ref[1, :, :hw] / acc_ref[1, :, hw:]))
    o_ref[0] = (_rms(o, sub_ref[...]) * (1.0 - LAM_INIT)).astype(BF16)


def _diff_attn(q3, kt, v3, dl, sub):
    b, seq, _ = q3.shape
    tq = min(512, seq)
    tk = min(1024, seq // 2)
    hw = 2 * HEAD_DIM
    return pl.pallas_call(
        functools.partial(_diff_attn_kernel, tk=tk),
        grid=(b, N_HEADS, seq // tq),
        in_specs=[
            pl.BlockSpec((4, HEAD_DIM), lambda b_, h, i: (0, 0)),
            pl.BlockSpec((1, hw), lambda b_, h, i: (0, 0)),
            pl.BlockSpec((1, tq, hw), lambda b_, h, i: (b_, i, h)),
            pl.BlockSpec((hw, seq), lambda b_, h, i: (h, b_)),
            pl.BlockSpec((1, seq, hw), lambda b_, h, i: (b_, 0, h)),
        ],
        out_specs=pl.BlockSpec((1, tq, hw), lambda b_, h, i: (b_, i, h)),
        out_shape=jax.ShapeDtypeStruct((b, seq, ATTN_W), BF16),
        scratch_shapes=[pltpu.VMEM((2, 2, tq, tk), F32), pltpu.VMEM((2, tq, 1), F32),
                        pltpu.VMEM((2, tq, 2 * hw), F32)],
        compiler_params=_params("parallel", "parallel", "arbitrary"),
        name="diff_attn",
    )(dl, sub, q3, kt, v3)


def _slot_transpose(vs):
    lane = lax.broadcasted_iota(jnp.int32, vs[0].shape, 1)
    for d in (4, 2, 1):
        keep = ((lane // GROUP_SIZE) & d) == 0
        new = list(vs)
        for i in range(8):
            if i & d == 0:
                lo, hi = vs[i], vs[i + d]
                new[i] = jnp.where(keep, lo, pltpu.roll(hi, d * GROUP_SIZE, axis=1))
                new[i + d] = jnp.where(keep, pltpu.roll(lo, LANES - d * GROUP_SIZE, axis=1), hi)
        vs = new
    return vs


def _ssm_kernel(u_ref, wi_ref, wb_ref, wc_ref, ap_ref, pt_ref, z_ref, ut_ref, s_ref, zt_ref, *, rb):
    nc = u_ref.shape[0]
    n_tiles = nc // 8
    gw = CHUNK * GROUP_SIZE
    pw = 2 * gw

    def to_chunk_major(r, _):
        rs = pl.ds(pl.multiple_of(r * rb, rb), rb)
        for th in range(CHUNK // 8):
            by_group = _slot_transpose([u_ref[rs, th * 8 + tl, :] for tl in range(8)])
            for g in range(GROUP_BLOCK):
                ut_ref[rs, g * gw + th * LANES:g * gw + (th + 1) * LANES] = by_group[g]
        return 0

    lax.fori_loop(0, nc // rb, to_chunk_major, 0)

    for q in range(GROUP_BLOCK // 2):
        cols = slice(q * pw, (q + 1) * pw)
        s_ref[:, cols] = jnp.dot(ut_ref[:, cols].astype(BF16), wb_ref[0, q],
                                 preferred_element_type=F32)

    row = lax.broadcasted_iota(jnp.int32, (8, LANES), 0)

    def shift_down(x, k):
        return jnp.where(row >= k, pltpu.roll(x, k, axis=0), 0.0)

    def shift_up(x, k):
        return jnp.where(row < 8 - k, pltpu.roll(x, 8 - k, axis=0), 0.0)

    def scan(i, carry):
        new = []
        for q in range(GROUP_BLOCK // 2):
            for d in range(2):
                h_re, h_im = carry[q * 2 + d]
                tile = i if d == 0 else n_tiles - 1 - i
                rs = pl.ds(pl.multiple_of(tile * 8, 8), 8)
                c_re = slice(q * pw + d * 2 * LANES, q * pw + d * 2 * LANES + LANES)
                c_im = slice(q * pw + d * 2 * LANES + LANES, q * pw + (d + 1) * 2 * LANES)
                shift = shift_down if d == 0 else shift_up
                j = (q * 2 + d) * 2
                x_re, x_im = s_ref[rs, c_re], s_ref[rs, c_im]
                for ki, k in enumerate((1, 2, 4)):
                    a_re, a_im = ap_ref[0, j * 4 + ki], ap_ref[0, (j + 1) * 4 + ki]
                    y_re, y_im = shift(x_re, k), shift(x_im, k)
                    x_re, x_im = (x_re + a_re * y_re - a_im * y_im,
                                  x_im + a_re * y_im + a_im * y_re)
                p_re, p_im = pt_ref[0, j], pt_ref[0, j + 1]
                s_ref[rs, c_re] = p_re * h_re - p_im * h_im + shift(x_re, 1)
                s_ref[rs, c_im] = p_re * h_im + p_im * h_re + shift(x_im, 1)
                a_re, a_im = ap_ref[0, j * 4 + 3], ap_ref[0, (j + 1) * 4 + 3]
                edge = slice(7, 8) if d == 0 else slice(0, 1)
                new.append((a_re * h_re - a_im * h_im + jnp.broadcast_to(x_re[edge], (8, LANES)),
                            a_re * h_im + a_im * h_re + jnp.broadcast_to(x_im[edge], (8, LANES))))
        return tuple(new)

    zero = jnp.zeros((8, LANES), F32)
    lax.fori_loop(0, n_tiles, scan, tuple((zero, zero) for _ in range(GROUP_BLOCK)))

    for q in range(GROUP_BLOCK // 2):
        cols = slice(q * pw, (q + 1) * pw)
        carried = jnp.dot(s_ref[:, cols].astype(BF16), wc_ref[0, q], preferred_element_type=F32)
        for g2 in range(2):
            g = 2 * q + g2
            gc = slice(g * gw, (g + 1) * gw)
            y = carried[:, g2 * gw:(g2 + 1) * gw] + jnp.dot(
                ut_ref[:, gc].astype(BF16), wi_ref[0, g], preferred_element_type=F32)
            ut_ref[:, gc] = 0.5 * y * (1.0 + jnp.tanh(0.7978845608028654 * (y + 0.044715 * (y * y * y))))

    def to_token_major(r, _):
        rs = pl.ds(pl.multiple_of(r * rb, rb), rb)
        for th in range(CHUNK // 8):
            by_token = _slot_transpose(
                [ut_ref[rs, g * gw + th * LANES:g * gw + (th + 1) * LANES] for g in range(GROUP_BLOCK)])
            for tl in range(8):
                zt_ref[rs, th * 8 + tl, :] = by_token[tl]
        return 0

    lax.fori_loop(0, nc // rb, to_token_major, 0)
    z_ref[...] = zt_ref[...].astype(BF16)


def _ssm(u3, b, wi, wb, wc, ap, pt):
    nc = u3.shape[0] // b
    cw = GROUP_BLOCK * CHUNK * GROUP_SIZE
    blk = pl.BlockSpec((nc, CHUNK, LANES), lambda g, b_: (b_, 0, g))
    whole = lambda a: pl.BlockSpec((1,) + a.shape[1:], lambda g, b_: (g,) + (0,) * (a.ndim - 1))
    return pl.pallas_call(
        functools.partial(_ssm_kernel, rb=min(64, nc)),
        grid=(N_GROUPS // GROUP_BLOCK, b),
        in_specs=[blk, whole(wi), whole(wb), whole(wc), whole(ap), whole(pt)],
        out_specs=blk,
        out_shape=jax.ShapeDtypeStruct(u3.shape, BF16),
        scratch_shapes=[pltpu.VMEM((nc, cw), F32), pltpu.VMEM((nc, cw), F32),
                        pltpu.VMEM((nc, CHUNK, LANES), F32)],
        compiler_params=_params("parallel", "arbitrary"),
        name="s5_scan",
    )(u3, wi, wb, wc, ap, pt)


def _ssm_operators(lam_re, lam_im, log_dt, b_re, b_im, c_re, c_im, d_skip):
    t_ = CHUNK
    g_, p_, c_ = N_GROUPS, STATE, GROUP_SIZE
    nb, npair = g_ // GROUP_BLOCK, g_ // 2
    lr, li = lam_re.astype(F32), lam_im.astype(F32)
    dt = jnp.exp(log_dt.astype(F32))[..., None]
    mag = jnp.exp(lr * dt)
    ab_re, ab_im = mag * jnp.cos(li * dt), mag * jnp.sin(li * dt)
    den = lr * lr + li * li
    n_re, n_im = ab_re - 1.0, ab_im
    k_re = (n_re * lr + n_im * li) / den
    k_im = (n_im * lr - n_re * li) / den
    br, bi = b_re.astype(F32), b_im.astype(F32)
    bb_re = k_re[..., None] * br - k_im[..., None] * bi
    bb_im = k_re[..., None] * bi + k_im[..., None] * br
    cr, ci = c_re.astype(F32), c_im.astype(F32)

    def cpow(e):
        m = jnp.exp(e * (lr * dt))
        return m * jnp.cos(e * (li * dt)), m * jnp.sin(e * (li * dt))

    pw_re, pw_im = cpow(jnp.arange(t_ + 1, dtype=F32)[:, None, None, None])

    ca_re = cr[None] * pw_re[:t_, :, :, None, :] - ci[None] * pw_im[:t_, :, :, None, :]
    ca_im = cr[None] * pw_im[:t_, :, :, None, :] + ci[None] * pw_re[:t_, :, :, None, :]
    kern = (jnp.einsum('tdgop,dgpc->tdgoc', ca_re, bb_re, precision=HI)
            - jnp.einsum('tdgop,dgpc->tdgoc', ca_im, bb_im, precision=HI))
    tt = jnp.arange(t_)
    lag = tt[None, :] - tt[:, None]
    k_f = kern[jnp.clip(lag, 0, t_ - 1), 0] * (lag >= 0)[..., None, None, None].astype(F32)
    k_b = kern[jnp.clip(-lag, 0, t_ - 1), 1] * (lag <= 0)[..., None, None, None].astype(F32)
    w_intra = (k_f + k_b).transpose(2, 0, 4, 1, 3)
    skip = (jnp.eye(t_, dtype=F32)[None, :, None, :, None]
            * jnp.eye(c_, dtype=F32)[None, None, :, None, :]
            * d_skip.astype(F32).reshape(g_, 1, c_, 1, 1))
    w_intra = (w_intra + skip).reshape(nb, GROUP_BLOCK, t_ * c_, t_ * c_)

    eye2 = jnp.eye(2, dtype=F32)

    def summ(pw_r, pw_i, d):
        re = pw_r[..., None] * bb_re[d][None] - pw_i[..., None] * bb_im[d][None]
        im = pw_r[..., None] * bb_im[d][None] + pw_i[..., None] * bb_re[d][None]
        return re.transpose(1, 0, 3, 2), im.transpose(1, 0, 3, 2)

    f_re, f_im = summ(pw_re[:t_, 0][::-1], pw_im[:t_, 0][::-1], 0)
    r_re, r_im = summ(pw_re[:t_, 1], pw_im[:t_, 1], 1)
    wb = jnp.stack([f_re, f_im, r_re, r_im], axis=3)
    wb = wb.reshape(npair, 2, t_ * c_, 4, p_)
    wb = jnp.einsum('qgkcp,gh->qgkchp', wb, eye2)
    wb = wb.reshape(nb, GROUP_BLOCK // 2, 2 * t_ * c_, 4 * 2 * p_)

    def readout(pw_r, pw_i, d):
        re = cr[d][None] * pw_r[:, :, None, :] - ci[d][None] * pw_i[:, :, None, :]
        im = cr[d][None] * pw_i[:, :, None, :] + ci[d][None] * pw_r[:, :, None, :]
        return re.transpose(1, 3, 0, 2), -im.transpose(1, 3, 0, 2)

    of_re, of_im = readout(pw_re[1:, 0], pw_im[1:, 0], 0)
    ob_re, ob_im = readout(pw_re[1:, 1][::-1], pw_im[1:, 1][::-1], 1)
    wc = jnp.stack([of_re, of_im, ob_re, ob_im], axis=1)
    wc = wc.reshape(npair, 2, 4, p_, t_ * c_)
    wc = jnp.einsum('qgcpk,gh->qcgphk', wc, eye2)
    wc = wc.reshape(nb, GROUP_BLOCK // 2, 4 * 2 * p_, 2 * t_ * c_)

    def table(e):
        re, im = cpow(e)
        tab = jnp.stack([re, im], axis=2)
        k = tab.shape[0]
        tab = tab.reshape(k, 2, 2, nb, GROUP_BLOCK // 2, 2 * p_).transpose(3, 4, 1, 2, 0, 5)
        return tab.reshape(nb, (GROUP_BLOCK // 2) * 4, k, 2 * p_)

    step = jnp.asarray([1.0, 2.0, 4.0, 8.0], F32) * t_
    ap = table(jnp.broadcast_to(step[:, None, None, None], (4, 2, 1, 1)))
    ap = jnp.broadcast_to(ap[:, :, :, None, :], ap.shape[:3] + (8, 2 * p_)).reshape(nb, -1, 8, 2 * p_)
    j8 = jnp.arange(8, dtype=F32) * t_
    pt = table(jnp.stack([j8, j8[::-1]], axis=1)[:, :, None, None])
    return w_intra.astype(BF16), wb.astype(BF16), wc.astype(BF16), ap, pt


def _merge_kernel(att_ref, z_ref, ga_ref, gs_ref, wp_ref, wa_ref, wb_ref, o_ref):
    z = z_ref[...]
    y_attn = jnp.dot(att_ref[...], wp_ref[...], preferred_element_type=F32)
    y_ssm = (jnp.dot(z, wa_ref[...], preferred_element_type=F32)
             * jax.nn.sigmoid(jnp.dot(z, wb_ref[...], preferred_element_type=F32)))
    g_a = jax.nn.sigmoid(ga_ref[...].astype(F32))
    g_s = jax.nn.sigmoid(gs_ref[...].astype(F32))
    o_ref[...] = (g_a * y_attn + g_s * y_ssm).astype(BF16)


def _merge(att, z, gates, wp, wa, wb):
    n = att.shape[0]
    tm = min(1024, n)
    tn = 512
    nj = D_MODEL // tn
    return pl.pallas_call(
        _merge_kernel,
        grid=(n // tm, nj),
        in_specs=[
            pl.BlockSpec((tm, ATTN_W), lambda i, j: (i, 0)),
            pl.BlockSpec((tm, SSM_W), lambda i, j: (i, 0)),
            pl.BlockSpec((tm, tn), lambda i, j: (i, j)),
            pl.BlockSpec((tm, tn), lambda i, j: (i, j + nj)),
            pl.BlockSpec((ATTN_W, tn), lambda i, j: (0, j)),
            pl.BlockSpec((SSM_W, tn), lambda i, j: (0, j)),
            pl.BlockSpec((SSM_W, tn), lambda i, j: (0, j)),
        ],
        out_specs=pl.BlockSpec((tm, tn), lambda i, j: (i, j)),
        out_shape=jax.ShapeDtypeStruct((n, D_MODEL), BF16),
        compiler_params=_params("parallel", "arbitrary"),
        name="merge",
    )(att, z, gates, gates, wp, wa, wb)


def _mm_res_kernel(a_ref, w_ref, r_ref, o_ref):
    o_ref[...] = r_ref[...] + jnp.dot(a_ref[...], w_ref[...], preferred_element_type=F32)


def _mm_res(a, w, res):
    n, kdim = a.shape
    cols = w.shape[1]
    tm = min(1024, n)
    tn = 512
    return pl.pallas_call(
        _mm_res_kernel,
        grid=(n // tm, cols // tn),
        in_specs=[
            pl.BlockSpec((tm, kdim), lambda i, j: (i, 0)),
            pl.BlockSpec((kdim, tn), lambda i, j: (0, j)),
            pl.BlockSpec((tm, tn), lambda i, j: (i, j)),
        ],
        out_specs=pl.BlockSpec((tm, tn), lambda i, j: (i, j)),
        out_shape=jax.ShapeDtypeStruct((n, cols), F32),
        compiler_params=_params("parallel", "arbitrary"),
        name="matmul_residual",
    )(a, w, res)


def _norm_mm_kernel(x_ref, g_ref, w_ref, o_ref, xn_ref, *, scale):
    @pl.when(pl.program_id(1) == 0)
    def _():
        xn_ref[...] = _rms(x_ref[...], g_ref[...]).astype(BF16)

    acc = jnp.dot(xn_ref[...], w_ref[...], preferred_element_type=F32)
    o_ref[...] = (acc * scale).astype(BF16)


def _norm_mm(x, g, w, scale):
    n, kdim = x.shape
    cols = w.shape[1]
    tm = min(512, n)
    tn = 1024
    return pl.pallas_call(
        functools.partial(_norm_mm_kernel, scale=scale),
        grid=(n // tm, cols // tn),
        in_specs=[
            pl.BlockSpec((tm, kdim), lambda i, j: (i, 0)),
            pl.BlockSpec((1, kdim), lambda i, j: (0, 0)),
            pl.BlockSpec((kdim, tn), lambda i, j: (0, j)),
        ],
        out_specs=pl.BlockSpec((tm, tn), lambda i, j: (i, j)),
        out_shape=jax.ShapeDtypeStruct((n, cols), BF16),
        scratch_shapes=[pltpu.VMEM((tm, kdim), BF16)],
        compiler_params=_params("parallel", "arbitrary"),
        name="norm_matmul",
    )(x, g, w)


def _cross_attn_kernel(q_ref, kv_ref, o_ref):
    nt = (((1,), (1,)), ((), ()))
    for h in range(X_HEADS):
        cols = slice(h * X_HEAD_DIM, (h + 1) * X_HEAD_DIM)
        vcols = slice(D_MODEL + h * X_HEAD_DIM, D_MODEL + (h + 1) * X_HEAD_DIM)
        s = lax.dot_general(q_ref[0, :, cols], kv_ref[0, :, cols], nt, preferred_element_type=F32)
        p = jnp.exp(s - jnp.max(s, axis=-1, keepdims=True))
        l = jnp.sum(p, axis=-1, keepdims=True)
        o = jnp.dot(p.astype(BF16), kv_ref[0, :, vcols], preferred_element_type=F32)
        o_ref[0, :, cols] = (o / l).astype(BF16)


def _cross_attn(qc3, kv3):
    b, seq, _ = qc3.shape
    m = kv3.shape[1]
    tq = min(512, seq)
    return pl.pallas_call(
        _cross_attn_kernel,
        grid=(b, seq // tq),
        in_specs=[
            pl.BlockSpec((1, tq, D_MODEL), lambda b_, i: (b_, i, 0)),
            pl.BlockSpec((1, m, 2 * D_MODEL), lambda b_, i: (b_, 0, 0)),
        ],
        out_specs=pl.BlockSpec((1, tq, D_MODEL), lambda b_, i: (b_, i, 0)),
        out_shape=jax.ShapeDtypeStruct((b, seq, D_MODEL), BF16),
        compiler_params=_params("parallel", "arbitrary"),
        name="cross_attn",
    )(qc3, kv3)


def _mlp_kernel(x_ref, g_ref, wu_ref, wd_ref, gf_ref, o_ref, hn_ref, acc_ref):
    f = pl.program_id(1)

    @pl.when(f == 0)
    def _():
        hn_ref[...] = _rms(x_ref[...], g_ref[...]).astype(BF16)
        acc_ref[...] = jnp.zeros_like(acc_ref)

    h = jnp.maximum(jnp.dot(hn_ref[...], wu_ref[...], preferred_element_type=F32), 0.0)
    acc_ref[...] += jnp.dot((h * h).astype(BF16), wd_ref[...], preferred_element_type=F32)

    @pl.when(f == pl.num_programs(1) - 1)
    def _():
        o_ref[...] = _rms(x_ref[...] + acc_ref[...], gf_ref[...])


def _mlp(x, g, wu, wd, gf):
    n = x.shape[0]
    tm = min(512, n)
    tf = 512
    return pl.pallas_call(
        _mlp_kernel,
        grid=(n // tm, D_FF // tf),
        in_specs=[
            pl.BlockSpec((tm, D_MODEL), lambda i, f: (i, 0)),
            pl.BlockSpec((1, D_MODEL), lambda i, f: (0, 0)),
            pl.BlockSpec((D_MODEL, tf), lambda i, f: (0, f)),
            pl.BlockSpec((tf, D_MODEL), lambda i, f: (f, 0)),
            pl.BlockSpec((1, D_MODEL), lambda i, f: (0, 0)),
        ],
        out_specs=pl.BlockSpec((tm, D_MODEL), lambda i, f: (i, 0)),
        out_shape=jax.ShapeDtypeStruct((n, D_MODEL), F32),
        scratch_shapes=[pltpu.VMEM((tm, D_MODEL), BF16), pltpu.VMEM((tm, D_MODEL), F32)],
        compiler_params=_params("parallel", "arbitrary"),
        name="mlp_final_norm",
    )(x, g, wu, wd, gf)


def _rope_tables(seq):
    half = HEAD_DIM // 2
    inv = ROPE_THETA ** (-jnp.arange(0, HEAD_DIM, 2, dtype=F32) / HEAD_DIM)
    ang = jnp.arange(seq, dtype=F32)[:, None] * inv[None, :]
    cos, sin = jnp.cos(ang), jnp.sin(ang)
    cos_t = jnp.tile(cos, (1, LANES // half))
    sin_t = jnp.concatenate([-sin, -sin, sin, sin], axis=1)
    return cos_t, sin_t


def _qk_perm():
    half = HEAD_DIM // 2
    idx = []
    for h in range(N_HEADS):
        for part in range(2):
            for m in range(2):
                base = h * 2 * HEAD_DIM + m * HEAD_DIM + part * half
                idx.extend(range(base, base + half))
    return jnp.asarray(idx, dtype=jnp.int32)


def _prepare(w):
    (norm_mix, w_in, diff_lambda, subln, w_attn_proj,
     lam_re, lam_im, log_dt, b_re, b_im, c_re, c_im, d_skip,
     w_glu_a, w_glu_b, w_mix_out,
     norm_cross, norm_mem, w_q_cross, w_kv_cross, w_o_cross,
     norm_mlp, w_mlp_up, w_mlp_down, norm_final) = w
    perm = _qk_perm()
    w_in0 = w_in[0]
    w_in_p = jnp.concatenate([w_in0[:, perm], w_in0[:, ATTN_W + perm], w_in0[:, 2 * ATTN_W:]],
                             axis=1).astype(BF16)
    row = lambda v: v.astype(F32).reshape(1, -1)
    return dict(
        norm_mix=row(norm_mix[0]), w_in=w_in_p, diff_lambda=diff_lambda[0].astype(F32),
        subln=row(subln[0]), w_attn_proj=w_attn_proj[0].astype(BF16),
        ssm=_ssm_operators(lam_re[0], lam_im[0], log_dt[0], b_re[0], b_im[0], c_re[0], c_im[0],
                           d_skip[0]),
        w_glu_a=w_glu_a[0].astype(BF16), w_glu_b=w_glu_b[0].astype(BF16),
        w_mix_out=w_mix_out[0].astype(BF16),
        norm_cross=row(norm_cross[0]), norm_mem=row(norm_mem[0]),
        w_q_cross=w_q_cross[0].astype(BF16), w_kv_cross=w_kv_cross[0].astype(BF16),
        w_o_cross=w_o_cross[0].astype(BF16),
        norm_mlp=row(norm_mlp[0]), w_mlp_up=w_mlp_up[0].astype(BF16),
        w_mlp_down=w_mlp_down[0].astype(BF16), norm_final=row(norm_final),
    )


def _encode(x, mem, p):
    b, seq, _ = x.shape
    assert seq % (8 * CHUNK) == 0
    n = b * seq
    nc = seq // CHUNK
    x2 = x.reshape(n, D_MODEL)
    cos_t, sin_t = _rope_tables(seq)

    q, kt, v, u, gates = _in_proj(x2, p["norm_mix"], p["w_in"], cos_t, sin_t, seq)
    att = _diff_attn(q.reshape(b, seq, ATTN_W), kt, v.reshape(b, seq, ATTN_W),
                     p["diff_lambda"], p["subln"])

    z = _ssm(u.reshape(b * nc, CHUNK, SSM_W), b, *p["ssm"]).reshape(n, SSM_W)

    mixed = _merge(att.reshape(n, ATTN_W), z, gates, p["w_attn_proj"], p["w_glu_a"], p["w_glu_b"])
    x2 = _mm_res(mixed, p["w_mix_out"], x2)

    qc = _norm_mm(x2, p["norm_cross"], p["w_q_cross"], X_HEAD_DIM ** -0.5)
    m_tok = mem.shape[1]
    kv = _norm_mm(mem.reshape(b * m_tok, D_MODEL), p["norm_mem"], p["w_kv_cross"], 1.0)
    oc = _cross_attn(qc.reshape(b, seq, D_MODEL), kv.reshape(b, m_tok, 2 * D_MODEL))
    x2 = _mm_res(oc.reshape(n, D_MODEL), p["w_o_cross"], x2)

    out = _mlp(x2, p["norm_mlp"], p["w_mlp_up"], p["w_mlp_down"], p["norm_final"])
    return out.reshape(b, seq, D_MODEL)


def kernel(x_prompt, x_sample, mem_prompt, mem_sample, norm_mix, w_in, diff_lambda, subln, w_attn_proj, ssm_lambda_re, ssm_lambda_im, ssm_log_dt, ssm_b_re, ssm_b_im, ssm_c_re, ssm_c_im, ssm_d, w_glu_a, w_glu_b, w_mix_out, norm_cross, norm_mem, w_q_cross, w_kv_cross, w_o_cross, norm_mlp, w_mlp_up, w_mlp_down, norm_final):
    p = _prepare((norm_mix, w_in, diff_lambda, subln, w_attn_proj,
                  ssm_lambda_re, ssm_lambda_im, ssm_log_dt, ssm_b_re, ssm_b_im, ssm_c_re, ssm_c_im,
                  ssm_d, w_glu_a, w_glu_b, w_mix_out,
                  norm_cross, norm_mem, w_q_cross, w_kv_cross, w_o_cross,
                  norm_mlp, w_mlp_up, w_mlp_down, norm_final))
    return (_encode(x_prompt, mem_prompt, p), _encode(x_sample, mem_sample, p))
```

```python
import functools
import math

import jax
import jax.numpy as jnp
from jax import lax
from jax.experimental import pallas as pl
from jax.experimental.pallas import tpu as pltpu

D_MODEL = 2048
N_HEADS = 8
HEAD_DIM = 64
ATTN_W = N_HEADS * 2 * HEAD_DIM
SSM_W = D_MODEL // 2
GROUP_SIZE = 16
N_GROUPS = SSM_W // GROUP_SIZE
STATE = 64
IN_COLS = 3 * ATTN_W + SSM_W + 2 * D_MODEL
D_FF = 4 * D_MODEL
X_HEADS = 4
X_HEAD_DIM = D_MODEL // X_HEADS
ROPE_THETA = 10000.0
EPS = 1e-6
LAM_INIT = 0.8 - 0.6 * math.exp(-0.3 * 0)
Q_SCALE = HEAD_DIM ** -0.5 * math.log2(math.e)

CHUNK = 16
GROUP_BLOCK = 8
LANES = 128
VMEM_LIMIT = 56 * 1024 * 1024

F32 = jnp.float32
BF16 = jnp.bfloat16
HI = lax.Precision.HIGHEST


def _params(*sem):
    return pltpu.CompilerParams(dimension_semantics=sem, vmem_limit_bytes=VMEM_LIMIT)


def _rms(x, g):
    return x * lax.rsqrt(jnp.mean(x * x, axis=-1, keepdims=True) + EPS) * g


def _in_proj_kernel(x_ref, g_ref, w_ref, cos_ref, sin_ref,
                    q_ref, kt_ref, v_ref, u_ref, gate_ref, xn_ref):
    j = pl.program_id(1)

    @pl.when(j == 0)
    def _():
        xn_ref[...] = _rms(x_ref[...], g_ref[...]).astype(BF16)

    acc = jnp.dot(xn_ref[...], w_ref[...], preferred_element_type=F32)

    def rope(scale):
        cos = cos_ref[...]
        sin = sin_ref[...]
        outs = []
        for c in range(acc.shape[1] // LANES):
            xc = acc[:, c * LANES:(c + 1) * LANES]
            outs.append((xc * cos + pltpu.roll(xc, LANES // 2, axis=1) * sin) * scale)
        return jnp.concatenate(outs, axis=1)

    @pl.when(j == 0)
    def _():
        q_ref[...] = rope(Q_SCALE).astype(BF16)

    @pl.when(j == 1)
    def _():
        kt_ref[...] = rope(1.0).T.astype(BF16)

    @pl.when(j == 2)
    def _():
        v_ref[...] = acc.astype(BF16)

    @pl.when(j == 3)
    def _():
        u_ref[...] = acc

    @pl.when(j >= 4)
    def _():
        gate_ref[...] = acc.astype(BF16)


def _in_proj(x2, g, w, cos_t, sin_t, seq):
    n = x2.shape[0]
    tm = min(512, seq)
    tn = ATTN_W
    nj = IN_COLS // tn
    pos_blocks = seq // tm
    row = lambda i, j: (i, 0)
    out_sd = lambda cols: jax.ShapeDtypeStruct((n, cols), BF16)
    return pl.pallas_call(
        _in_proj_kernel,
        grid=(n // tm, nj),
        in_specs=[
            pl.BlockSpec((tm, D_MODEL), row),
            pl.BlockSpec((1, D_MODEL), lambda i, j: (0, 0)),
            pl.BlockSpec((D_MODEL, tn), lambda i, j: (0, j)),
            pl.BlockSpec((tm, LANES), lambda i, j: (i % pos_blocks, 0)),
            pl.BlockSpec((tm, LANES), lambda i, j: (i % pos_blocks, 0)),
        ],
        out_specs=[
            pl.BlockSpec((tm, tn), row),
            pl.BlockSpec((tn, tm), lambda i, j: (0, i)),
            pl.BlockSpec((tm, tn), row),
            pl.BlockSpec((tm, tn), row),
            pl.BlockSpec((tm, tn), lambda i, j: (i, jnp.maximum(j - 4, 0))),
        ],
        out_shape=[out_sd(ATTN_W), jax.ShapeDtypeStruct((ATTN_W, n), BF16), out_sd(ATTN_W),
                   jax.ShapeDtypeStruct((n, SSM_W), F32), out_sd(2 * D_MODEL)],
        scratch_shapes=[pltpu.VMEM((tm, D_MODEL), BF16)],
        compiler_params=_params("parallel", "arbitrary"),
        name="in_proj",
    )(x2, g, w, cos_t, sin_t)


def _diff_attn_kernel(dl_ref, sub_ref, q_ref, kt_ref, v_ref, o_ref, s_ref, mx_ref, m_ref, acc_ref,
                      *, tq, tk, unroll):
    seq = v_ref.shape[1]
    nk = seq // tk
    steps = (seq // tq) * nk
    hw = 2 * HEAD_DIM
    lane = lax.broadcasted_iota(jnp.int32, (tq, hw), 1)
    first = ((lane // (HEAD_DIM // 2)) % 2) == 0
    ones = jnp.ones((tk, hw), BF16)
    dl = dl_ref[...]
    lam = (jnp.exp(jnp.sum(dl[0:1] * dl[1:2], axis=-1, keepdims=True))
           - jnp.exp(jnp.sum(dl[2:3] * dl[3:4], axis=-1, keepdims=True)) + LAM_INIT)

    def scores(f, slot):
        f = jnp.minimum(f, steps - 1)
        q = q_ref[0, pl.ds(pl.multiple_of((f // nk) * tq, tq), tq), :]
        kc = kt_ref[:, pl.ds(pl.multiple_of((f % nk) * tk, tk), tk)]
        zero = jnp.zeros_like(q)
        for comp, qm in enumerate((jnp.where(first, q, zero), jnp.where(first, zero, q))):
            s = jnp.dot(qm, kc, preferred_element_type=F32)
            s_ref[slot, comp] = s
            mx_ref[slot, comp] = jnp.max(s, axis=-1, keepdims=True)

    def accumulate(f, slot, emit):
        c = f % nk
        vx = jnp.concatenate([v_ref[0, pl.ds(pl.multiple_of(c * tk, tk), tk), :], ones], axis=1)
        acc = []
        for comp in range(2):
            m_old = jnp.where(c == 0, -jnp.inf, m_ref[comp])
            m_new = jnp.maximum(m_old, mx_ref[slot, comp])
            m_ref[comp] = m_new
            p = jnp.exp2(s_ref[slot, comp] - m_new).astype(BF16)
            acc.append(jnp.exp2(m_old - m_new) * acc_ref[comp]
                       + jnp.dot(p, vx, preferred_element_type=F32))
            acc_ref[comp] = acc[comp]
        if emit:
            o = acc[0][:, :hw] / acc[0][:, hw:] - lam * (acc[1][:, :hw] / acc[1][:, hw:])
            rows = pl.ds(pl.multiple_of((f // nk) * tq, tq), tq)
            o_ref[0, rows, :] = (_rms(o, sub_ref[...]) * (1.0 - LAM_INIT)).astype(BF16)

    m_ref[...] = jnp.full(m_ref.shape, -jnp.inf, F32)
    acc_ref[...] = jnp.zeros(acc_ref.shape, F32)
    scores(0, 0)

    def body(i, _):
        for k in range(unroll):
            f = i * unroll + k
            scores(f + 1, (k + 1) % 2)
            accumulate(f, k % 2, emit=(k + 1) % min(unroll, nk) == 0)
        return 0

    lax.fori_loop(0, steps // unroll, body, 0)


def _diff_attn(q3, kt, v3, dl, sub):
    b, seq, _ = q3.shape
    tq = min(512, seq)
    tk = min(1024, seq // 2)
    nk = seq // tk
    unroll = 4 if ((seq // tq) * nk) % 4 == 0 else 2
    assert nk % unroll == 0 or unroll % nk == 0
    hw = 2 * HEAD_DIM
    blk = pl.BlockSpec((1, seq, hw), lambda b_, h: (b_, 0, h))
    return pl.pallas_call(
        functools.partial(_diff_attn_kernel, tq=tq, tk=tk, unroll=unroll),
        grid=(b, N_HEADS),
        in_specs=[
            pl.BlockSpec((4, HEAD_DIM), lambda b_, h: (0, 0)),
            pl.BlockSpec((1, hw), lambda b_, h: (0, 0)),
            blk,
            pl.BlockSpec((hw, seq), lambda b_, h: (h, b_)),
            blk,
        ],
        out_specs=blk,
        out_shape=jax.ShapeDtypeStruct((b, seq, ATTN_W), BF16),
        scratch_shapes=[pltpu.VMEM((2, 2, tq, tk), F32), pltpu.VMEM((2, 2, tq, 1), F32),
                        pltpu.VMEM((2, tq, 1), F32), pltpu.VMEM((2, tq, 2 * hw), F32)],
        compiler_params=_params("parallel", "arbitrary"),
        name="diff_attn",
    )(dl, sub, q3, kt, v3)


def _slot_transpose(vs):
    lane = lax.broadcasted_iota(jnp.int32, vs[0].shape, 1)
    for d in (4, 2, 1):
        keep = ((lane // GROUP_SIZE) & d) == 0
        new = list(vs)
        for i in range(8):
            if i & d == 0:
                lo, hi = vs[i], vs[i + d]
                new[i] = jnp.where(keep, lo, pltpu.roll(hi, d * GROUP_SIZE, axis=1))
                new[i + d] = jnp.where(keep, pltpu.roll(lo, LANES - d * GROUP_SIZE, axis=1), hi)
        vs = new
    return vs


def _ssm_kernel(u_ref, wi_ref, wb_ref, wc_ref, ap_ref, pt_ref, z_ref, ut_ref, s_ref, zt_ref, *, rb):
    nc = u_ref.shape[0]
    n_tiles = nc // 8
    gw = CHUNK * GROUP_SIZE
    pw = 2 * gw

    def to_chunk_major(r, _):
        rs = pl.ds(pl.multiple_of(r * rb, rb), rb)
        for th in range(CHUNK // 8):
            by_group = _slot_transpose([u_ref[rs, th * 8 + tl, :] for tl in range(8)])
            for g in range(GROUP_BLOCK):
                ut_ref[rs, g * gw + th * LANES:g * gw + (th + 1) * LANES] = by_group[g]
        return 0

    lax.fori_loop(0, nc // rb, to_chunk_major, 0)

    for q in range(GROUP_BLOCK // 2):
        cols = slice(q * pw, (q + 1) * pw)
        s_ref[:, cols] = jnp.dot(ut_ref[:, cols].astype(BF16), wb_ref[0, q],
                                 preferred_element_type=F32)

    row = lax.broadcasted_iota(jnp.int32, (8, LANES), 0)

    def shift_down(x, k):
        return jnp.where(row >= k, pltpu.roll(x, k, axis=0), 0.0)

    def shift_up(x, k):
        return jnp.where(row < 8 - k, pltpu.roll(x, 8 - k, axis=0), 0.0)

    def scan(i, carry):
        new = []
        for q in range(GROUP_BLOCK // 2):
            for d in range(2):
                h_re, h_im = carry[q * 2 + d]
                tile = i if d == 0 else n_tiles - 1 - i
                rs = pl.ds(pl.multiple_of(tile * 8, 8), 8)
                c_re = slice(q * pw + d * 2 * LANES, q * pw + d * 2 * LANES + LANES)
                c_im = slice(q * pw + d * 2 * LANES + LANES, q * pw + (d + 1) * 2 * LANES)
                shift = shift_down if d == 0 else shift_up
                j = (q * 2 + d) * 2
                x_re, x_im = s_ref[rs, c_re], s_ref[rs, c_im]
                for ki, k in enumerate((1, 2, 4)):
                    a_re, a_im = ap_ref[0, j * 4 + ki], ap_ref[0, (j + 1) * 4 + ki]
                    y_re, y_im = shift(x_re, k), shift(x_im, k)
                    x_re, x_im = (x_re + a_re * y_re - a_im * y_im,
                                  x_im + a_re * y_im + a_im * y_re)
                p_re, p_im = pt_ref[0, j], pt_ref[0, j + 1]
                s_ref[rs, c_re] = p_re * h_re - p_im * h_im + shift(x_re, 1)
                s_ref[rs, c_im] = p_re * h_im + p_im * h_re + shift(x_im, 1)
                a_re, a_im = ap_ref[0, j * 4 + 3], ap_ref[0, (j + 1) * 4 + 3]
                edge = slice(7, 8) if d == 0 else slice(0, 1)
                new.append((a_re * h_re - a_im * h_im + jnp.broadcast_to(x_re[edge], (8, LANES)),
                            a_re * h_im + a_im * h_re + jnp.broadcast_to(x_im[edge], (8, LANES))))
        return tuple(new)

    zero = jnp.zeros((8, LANES), F32)
    lax.fori_loop(0, n_tiles, scan, tuple((zero, zero) for _ in range(GROUP_BLOCK)))

    for q in range(GROUP_BLOCK // 2):
        cols = slice(q * pw, (q + 1) * pw)
        carried = jnp.dot(s_ref[:, cols].astype(BF16), wc_ref[0, q], preferred_element_type=F32)
        for g2 in range(2):
            g = 2 * q + g2
            gc = slice(g * gw, (g + 1) * gw)
            y = carried[:, g2 * gw:(g2 + 1) * gw] + jnp.dot(
                ut_ref[:, gc].astype(BF16), wi_ref[0, g], preferred_element_type=F32)
            ut_ref[:, gc] = 0.5 * y * (1.0 + jnp.tanh(0.7978845608028654 * (y + 0.044715 * (y * y * y))))

    def to_token_major(r, _):
        rs = pl.ds(pl.multiple_of(r * rb, rb), rb)
        for th in range(CHUNK // 8):
            by_token = _slot_transpose(
                [ut_ref[rs, g * gw + th * LANES:g * gw + (th + 1) * LANES] for g in range(GROUP_BLOCK)])
            for tl in range(8):
                zt_ref[rs, th * 8 + tl, :] = by_token[tl]
        return 0

    lax.fori_loop(0, nc // rb, to_token_major, 0)
    z_ref[...] = zt_ref[...].astype(BF16)


def _ssm(u3, b, wi, wb, wc, ap, pt):
    nc = u3.shape[0] // b
    cw = GROUP_BLOCK * CHUNK * GROUP_SIZE
    blk = pl.BlockSpec((nc, CHUNK, LANES), lambda g, b_: (b_, 0, g))
    whole = lambda a: pl.BlockSpec((1,) + a.shape[1:], lambda g, b_: (g,) + (0,) * (a.ndim - 1))
    return pl.pallas_call(
        functools.partial(_ssm_kernel, rb=min(64, nc)),
        grid=(N_GROUPS // GROUP_BLOCK, b),
        in_specs=[blk, whole(wi), whole(wb), whole(wc), whole(ap), whole(pt)],
        out_specs=blk,
        out_shape=jax.ShapeDtypeStruct(u3.shape, BF16),
        scratch_shapes=[pltpu.VMEM((nc, cw), F32), pltpu.VMEM((nc, cw), F32),
                        pltpu.VMEM((nc, CHUNK, LANES), F32)],
        compiler_params=_params("parallel", "arbitrary"),
        name="s5_scan",
    )(u3, wi, wb, wc, ap, pt)


def _ssm_operators(lam_re, lam_im, log_dt, b_re, b_im, c_re, c_im, d_skip):
    t_ = CHUNK
    g_, p_, c_ = N_GROUPS, STATE, GROUP_SIZE
    nb, npair = g_ // GROUP_BLOCK, g_ // 2
    lr, li = lam_re.astype(F32), lam_im.astype(F32)
    dt = jnp.exp(log_dt.astype(F32))[..., None]
    mag = jnp.exp(lr * dt)
    ab_re, ab_im = mag * jnp.cos(li * dt), mag * jnp.sin(li * dt)
    den = lr * lr + li * li
    n_re, n_im = ab_re - 1.0, ab_im
    k_re = (n_re * lr + n_im * li) / den
    k_im = (n_im * lr - n_re * li) / den
    br, bi = b_re.astype(F32), b_im.astype(F32)
    bb_re = k_re[..., None] * br - k_im[..., None] * bi
    bb_im = k_re[..., None] * bi + k_im[..., None] * br
    cr, ci = c_re.astype(F32), c_im.astype(F32)

    def cpow(e):
        m = jnp.exp(e * (lr * dt))
        return m * jnp.cos(e * (li * dt)), m * jnp.sin(e * (li * dt))

    pw_re, pw_im = cpow(jnp.arange(t_ + 1, dtype=F32)[:, None, None, None])

    ca_re = cr[None] * pw_re[:t_, :, :, None, :] - ci[None] * pw_im[:t_, :, :, None, :]
    ca_im = cr[None] * pw_im[:t_, :, :, None, :] + ci[None] * pw_re[:t_, :, :, None, :]
    kern = (jnp.einsum('tdgop,dgpc->tdgoc', ca_re, bb_re, precision=HI)
            - jnp.einsum('tdgop,dgpc->tdgoc', ca_im, bb_im, precision=HI))
    tt = jnp.arange(t_)
    lag = tt[None, :] - tt[:, None]
    k_f = kern[jnp.clip(lag, 0, t_ - 1), 0] * (lag >= 0)[..., None, None, None].astype(F32)
    k_b = kern[jnp.clip(-lag, 0, t_ - 1), 1] * (lag <= 0)[..., None, None, None].astype(F32)
    w_intra = (k_f + k_b).transpose(2, 0, 4, 1, 3)
    skip = (jnp.eye(t_, dtype=F32)[None, :, None, :, None]
            * jnp.eye(c_, dtype=F32)[None, None, :, None, :]
            * d_skip.astype(F32).reshape(g_, 1, c_, 1, 1))
    w_intra = (w_intra + skip).reshape(nb, GROUP_BLOCK, t_ * c_, t_ * c_)

    eye2 = jnp.eye(2, dtype=F32)

    def summ(pw_r, pw_i, d):
        re = pw_r[..., None] * bb_re[d][None] - pw_i[..., None] * bb_im[d][None]
        im = pw_r[..., None] * bb_im[d][None] + pw_i[..., None] * bb_re[d][None]
        return re.transpose(1, 0, 3, 2), im.transpose(1, 0, 3, 2)

    f_re, f_im = summ(pw_re[:t_, 0][::-1], pw_im[:t_, 0][::-1], 0)
    r_re, r_im = summ(pw_re[:t_, 1], pw_im[:t_, 1], 1)
    wb = jnp.stack([f_re, f_im, r_re, r_im], axis=3)
    wb = wb.reshape(npair, 2, t_ * c_, 4, p_)
    wb = jnp.einsum('qgkcp,gh->qgkchp', wb, eye2)
    wb = wb.reshape(nb, GROUP_BLOCK // 2, 2 * t_ * c_, 4 * 2 * p_)

    def readout(pw_r, pw_i, d):
        re = cr[d][None] * pw_r[:, :, None, :] - ci[d][None] * pw_i[:, :, None, :]
        im = cr[d][None] * pw_i[:, :, None, :] + ci[d][None] * pw_r[:, :, None, :]
        return re.transpose(1, 3, 0, 2), -im.transpose(1, 3, 0, 2)

    of_re, of_im = readout(pw_re[1:, 0], pw_im[1:, 0], 0)
    ob_re, ob_im = readout(pw_re[1:, 1][::-1], pw_im[1:, 1][::-1], 1)
    wc = jnp.stack([of_re, of_im, ob_re, ob_im], axis=1)
    wc = wc.reshape(npair, 2, 4, p_, t_ * c_)
    wc = jnp.einsum('qgcpk,gh->qcgphk', wc, eye2)
    wc = wc.reshape(nb, GROUP_BLOCK // 2, 4 * 2 * p_, 2 * t_ * c_)

    def table(e):
        re, im = cpow(e)
        tab = jnp.stack([re, im], axis=2)
        k = tab.shape[0]
        tab = tab.reshape(k, 2, 2, nb, GROUP_BLOCK // 2, 2 * p_).transpose(3, 4, 1, 2, 0, 5)
        return tab.reshape(nb, (GROUP_BLOCK // 2) * 4, k, 2 * p_)

    step = jnp.asarray([1.0, 2.0, 4.0, 8.0], F32) * t_
    ap = table(jnp.broadcast_to(step[:, None, None, None], (4, 2, 1, 1)))
    ap = jnp.broadcast_to(ap[:, :, :, None, :], ap.shape[:3] + (8, 2 * p_)).reshape(nb, -1, 8, 2 * p_)
    j8 = jnp.arange(8, dtype=F32) * t_
    pt = table(jnp.stack([j8, j8[::-1]], axis=1)[:, :, None, None])
    return w_intra.astype(BF16), wb.astype(BF16), wc.astype(BF16), ap, pt


def _merge_kernel(att_ref, z_ref, ga_ref, gs_ref, wp_ref, wa_ref, wb_ref, o_ref):
    z = z_ref[...]
    y_attn = jnp.dot(att_ref[...], wp_ref[...], preferred_element_type=F32)
    y_ssm = (jnp.dot(z, wa_ref[...], preferred_element_type=F32)
             * jax.nn.sigmoid(jnp.dot(z, wb_ref[...], preferred_element_type=F32)))
    g_a = jax.nn.sigmoid(ga_ref[...].astype(F32))
    g_s = jax.nn.sigmoid(gs_ref[...].astype(F32))
    o_ref[...] = (g_a * y_attn + g_s * y_ssm).astype(BF16)


def _merge(att, z, gates, wp, wa, wb):
    n = att.shape[0]
    tm = min(1024, n)
    tn = 512
    nj = D_MODEL // tn
    return pl.pallas_call(
        _merge_kernel,
        grid=(n // tm, nj),
        in_specs=[
            pl.BlockSpec((tm, ATTN_W), lambda i, j: (i, 0)),
            pl.BlockSpec((tm, SSM_W), lambda i, j: (i, 0)),
            pl.BlockSpec((tm, tn), lambda i, j: (i, j)),
            pl.BlockSpec((tm, tn), lambda i, j: (i, j + nj)),
            pl.BlockSpec((ATTN_W, tn), lambda i, j: (0, j)),
            pl.BlockSpec((SSM_W, tn), lambda i, j: (0, j)),
            pl.BlockSpec((SSM_W, tn), lambda i, j: (0, j)),
        ],
        out_specs=pl.BlockSpec((tm, tn), lambda i, j: (i, j)),
        out_shape=jax.ShapeDtypeStruct((n, D_MODEL), BF16),
        compiler_params=_params("parallel", "arbitrary"),
        name="merge",
    )(att, z, gates, gates, wp, wa, wb)


def _mm_res_kernel(a_ref, w_ref, r_ref, o_ref):
    o_ref[...] = r_ref[...] + jnp.dot(a_ref[...], w_ref[...], preferred_element_type=F32)


def _mm_res(a, w, res):
    n, kdim = a.shape
    cols = w.shape[1]
    tm = min(1024, n)
    tn = 512
    return pl.pallas_call(
        _mm_res_kernel,
        grid=(n // tm, cols // tn),
        in_specs=[
            pl.BlockSpec((tm, kdim), lambda i, j: (i, 0)),
            pl.BlockSpec((kdim, tn), lambda i, j: (0, j)),
            pl.BlockSpec((tm, tn), lambda i, j: (i, j)),
        ],
        out_specs=pl.BlockSpec((tm, tn), lambda i, j: (i, j)),
        out_shape=jax.ShapeDtypeStruct((n, cols), F32),
        compiler_params=_params("parallel", "arbitrary"),
        name="matmul_residual",
    )(a, w, res)


def _norm_mm_kernel(x_ref, g_ref, w_ref, o_ref, xn_ref, *, scale):
    @pl.when(pl.program_id(1) == 0)
    def _():
        xn_ref[...] = _rms(x_ref[...], g_ref[...]).astype(BF16)

    acc = jnp.dot(xn_ref[...], w_ref[...], preferred_element_type=F32)
    o_ref[...] = (acc * scale).astype(BF16)


def _norm_mm(x, g, w, scale):
    n, kdim = x.shape
    cols = w.shape[1]
    tm = min(512, n)
    tn = 1024
    return pl.pallas_call(
        functools.partial(_norm_mm_kernel, scale=scale),
        grid=(n // tm, cols // tn),
        in_specs=[
            pl.BlockSpec((tm, kdim), lambda i, j: (i, 0)),
            pl.BlockSpec((1, kdim), lambda i, j: (0, 0)),
            pl.BlockSpec((kdim, tn), lambda i, j: (0, j)),
        ],
        out_specs=pl.BlockSpec((tm, tn), lambda i, j: (i, j)),
        out_shape=jax.ShapeDtypeStruct((n, cols), BF16),
        scratch_shapes=[pltpu.VMEM((tm, kdim), BF16)],
        compiler_params=_params("parallel", "arbitrary"),
        name="norm_matmul",
    )(x, g, w)


def _cross_attn_kernel(q_ref, kv_ref, o_ref):
    nt = (((1,), (1,)), ((), ()))
    for h in range(X_HEADS):
        cols = slice(h * X_HEAD_DIM, (h + 1) * X_HEAD_DIM)
        vcols = slice(D_MODEL + h * X_HEAD_DIM, D_MODEL + (h + 1) * X_HEAD_DIM)
        s = lax.dot_general(q_ref[0, :, cols], kv_ref[0, :, cols], nt, preferred_element_type=F32)
        p = jnp.exp(s - jnp.max(s, axis=-1, keepdims=True))
        l = jnp.sum(p, axis=-1, keepdims=True)
        o = jnp.dot(p.astype(BF16), kv_ref[0, :, vcols], preferred_element_type=F32)
        o_ref[0, :, cols] = (o / l).astype(BF16)


def _cross_attn(qc3, kv3):
    b, seq, _ = qc3.shape
    m = kv3.shape[1]
    tq = min(512, seq)
    return pl.pallas_call(
        _cross_attn_kernel,
        grid=(b, seq // tq),
        in_specs=[
            pl.BlockSpec((1, tq, D_MODEL), lambda b_, i: (b_, i, 0)),
            pl.BlockSpec((1, m, 2 * D_MODEL), lambda b_, i: (b_, 0, 0)),
        ],
        out_specs=pl.BlockSpec((1, tq, D_MODEL), lambda b_, i: (b_, i, 0)),
        out_shape=jax.ShapeDtypeStruct((b, seq, D_MODEL), BF16),
        compiler_params=_params("parallel", "arbitrary"),
        name="cross_attn",
    )(qc3, kv3)


def _mlp_kernel(x_ref, g_ref, wu_ref, wd_ref, gf_ref, o_ref, hn_ref, acc_ref):
    f = pl.program_id(1)

    @pl.when(f == 0)
    def _():
        hn_ref[...] = _rms(x_ref[...], g_ref[...]).astype(BF16)
        acc_ref[...] = jnp.zeros_like(acc_ref)

    h = jnp.maximum(jnp.dot(hn_ref[...], wu_ref[...], preferred_element_type=F32), 0.0)
    acc_ref[...] += jnp.dot((h * h).astype(BF16), wd_ref[...], preferred_element_type=F32)

    @pl.when(f == pl.num_programs(1) - 1)
    def _():
        o_ref[...] = _rms(x_ref[...] + acc_ref[...], gf_ref[...])


def _mlp(x, g, wu, wd, gf):
    n = x.shape[0]
    tm = min(512, n)
    tf = 512
    return pl.pallas_call(
        _mlp_kernel,
        grid=(n // tm, D_FF // tf),
        in_specs=[
            pl.BlockSpec((tm, D_MODEL), lambda i, f: (i, 0)),
            pl.BlockSpec((1, D_MODEL), lambda i, f: (0, 0)),
            pl.BlockSpec((D_MODEL, tf), lambda i, f: (0, f)),
            pl.BlockSpec((tf, D_MODEL), lambda i, f: (f, 0)),
            pl.BlockSpec((1, D_MODEL), lambda i, f: (0, 0)),
        ],
        out_specs=pl.BlockSpec((tm, D_MODEL), lambda i, f: (i, 0)),
        out_shape=jax.ShapeDtypeStruct((n, D_MODEL), F32),
        scratch_shapes=[pltpu.VMEM((tm, D_MODEL), BF16), pltpu.VMEM((tm, D_MODEL), F32)],
        compiler_params=_params("parallel", "arbitrary"),
        name="mlp_final_norm",
    )(x, g, wu, wd, gf)


def _rope_tables(seq):
    half = HEAD_DIM // 2
    inv = ROPE_THETA ** (-jnp.arange(0, HEAD_DIM, 2, dtype=F32) / HEAD_DIM)
    ang = jnp.arange(seq, dtype=F32)[:, None] * inv[None, :]
    cos, sin = jnp.cos(ang), jnp.sin(ang)
    cos_t = jnp.tile(cos, (1, LANES // half))
    sin_t = jnp.concatenate([-sin, -sin, sin, sin], axis=1)
    return cos_t, sin_t


def _qk_perm():
    half = HEAD_DIM // 2
    idx = []
    for h in range(N_HEADS):
        for part in range(2):
            for m in range(2):
                base = h * 2 * HEAD_DIM + m * HEAD_DIM + part * half
                idx.extend(range(base, base + half))
    return jnp.asarray(idx, dtype=jnp.int32)


def _prepare(w):
    (norm_mix, w_in, diff_lambda, subln, w_attn_proj,
     lam_re, lam_im, log_dt, b_re, b_im, c_re, c_im, d_skip,
     w_glu_a, w_glu_b, w_mix_out,
     norm_cross, norm_mem, w_q_cross, w_kv_cross, w_o_cross,
     norm_mlp, w_mlp_up, w_mlp_down, norm_final) = w
    perm = _qk_perm()
    w_in0 = w_in[0]
    w_in_p = jnp.concatenate([w_in0[:, perm], w_in0[:, ATTN_W + perm], w_in0[:, 2 * ATTN_W:]],
                             axis=1).astype(BF16)
    row = lambda v: v.astype(F32).reshape(1, -1)
    return dict(
        norm_mix=row(norm_mix[0]), w_in=w_in_p, diff_lambda=diff_lambda[0].astype(F32),
        subln=row(subln[0]), w_attn_proj=w_attn_proj[0].astype(BF16),
        ssm=_ssm_operators(lam_re[0], lam_im[0], log_dt[0], b_re[0], b_im[0], c_re[0], c_im[0],
                           d_skip[0]),
        w_glu_a=w_glu_a[0].astype(BF16), w_glu_b=w_glu_b[0].astype(BF16),
        w_mix_out=w_mix_out[0].astype(BF16),
        norm_cross=row(norm_cross[0]), norm_mem=row(norm_mem[0]),
        w_q_cross=w_q_cross[0].astype(BF16), w_kv_cross=w_kv_cross[0].astype(BF16),
        w_o_cross=w_o_cross[0].astype(BF16),
        norm_mlp=row(norm_mlp[0]), w_mlp_up=w_mlp_up[0].astype(BF16),
        w_mlp_down=w_mlp_down[0].astype(BF16), norm_final=row(norm_final),
    )


def _encode(x, mem, p):
    b, seq, _ = x.shape
    assert seq % (8 * CHUNK) == 0
    n = b * seq
    nc = seq // CHUNK
    x2 = x.reshape(n, D_MODEL)
    cos_t, sin_t = _rope_tables(seq)

    q, kt, v, u, gates = _in_proj(x2, p["norm_mix"], p["w_in"], cos_t, sin_t, seq)
    att = _diff_attn(q.reshape(b, seq, ATTN_W), kt, v.reshape(b, seq, ATTN_W),
                     p["diff_lambda"], p["subln"])

    z = _ssm(u.reshape(b * nc, CHUNK, SSM_W), b, *p["ssm"]).reshape(n, SSM_W)

    mixed = _merge(att.reshape(n, ATTN_W), z, gates, p["w_attn_proj"], p["w_glu_a"], p["w_glu_b"])
    x2 = _mm_res(mixed, p["w_mix_out"], x2)

    qc = _norm_mm(x2, p["norm_cross"], p["w_q_cross"], X_HEAD_DIM ** -0.5)
    m_tok = mem.shape[1]
    kv = _norm_mm(mem.reshape(b * m_tok, D_MODEL), p["norm_mem"], p["w_kv_cross"], 1.0)
    oc = _cross_attn(qc.reshape(b, seq, D_MODEL), kv.reshape(b, m_tok, 2 * D_MODEL))
    x2 = _mm_res(oc.reshape(n, D_MODEL), p["w_o_cross"], x2)

    out = _mlp(x2, p["norm_mlp"], p["w_mlp_up"], p["w_mlp_down"], p["norm_final"])
    return out.reshape(b, seq, D_MODEL)


def kernel(x_prompt, x_sample, mem_prompt, mem_sample, norm_mix, w_in, diff_lambda, subln, w_attn_proj, ssm_lambda_re, ssm_lambda_im, ssm_log_dt, ssm_b_re, ssm_b_im, ssm_c_re, ssm_c_im, ssm_d, w_glu_a, w_glu_b, w_mix_out, norm_cross, norm_mem, w_q_cross, w_kv_cross, w_o_cross, norm_mlp, w_mlp_up, w_mlp_down, norm_final):
    p = _prepare((norm_mix, w_in, diff_lambda, subln, w_attn_proj,
                  ssm_lambda_re, ssm_lambda_im, ssm_log_dt, ssm_b_re, ssm_b_im, ssm_c_re, ssm_c_im,
                  ssm_d, w_glu_a, w_glu_b, w_mix_out,
                  norm_cross, norm_mem, w_q_cross, w_kv_cross, w_o_cross,
                  norm_mlp, w_mlp_up, w_mlp_down, norm_final))
    return (_encode(x_prompt, mem_prompt, p), _encode(x_sample, mem_sample, p))
```

```python
import functools
import math

import jax
import jax.numpy as jnp
from jax import lax
from jax.experimental import pallas as pl
from jax.experimental.pallas import tpu as pltpu

D_MODEL = 2048
N_HEADS = 8
HEAD_DIM = 64
ATTN_W = N_HEADS * 2 * HEAD_DIM
SSM_W = D_MODEL // 2
GROUP_SIZE = 16
N_GROUPS = SSM_W // GROUP_SIZE
STATE = 64
IN_COLS = 3 * ATTN_W + SSM_W + 2 * D_MODEL
D_FF = 4 * D_MODEL
X_HEADS = 4
X_HEAD_DIM = D_MODEL // X_HEADS
ROPE_THETA = 10000.0
EPS = 1e-6
LAM_INIT = 0.8 - 0.6 * math.exp(-0.3 * 0)
Q_SCALE = HEAD_DIM ** -0.5 * math.log2(math.e)

CHUNK = 16
GROUP_BLOCK = 8
LANES = 128
VMEM_LIMIT = 56 * 1024 * 1024

F32 = jnp.float32
BF16 = jnp.bfloat16
HI = lax.Precision.HIGHEST


def _params(*sem):
    return pltpu.CompilerParams(dimension_semantics=sem, vmem_limit_bytes=VMEM_LIMIT)


def _rms(x, g):
    return x * lax.rsqrt(jnp.mean(x * x, axis=-1, keepdims=True) + EPS) * g


def _in_proj_kernel(x_ref, g_ref, w_ref, cos_ref, sin_ref,
                    q_ref, kt_ref, v_ref, u_ref, gate_ref, xn_ref):
    j = pl.program_id(1)

    @pl.when(j == 0)
    def _():
        xn_ref[...] = _rms(x_ref[...], g_ref[...]).astype(BF16)

    acc = jnp.dot(xn_ref[...], w_ref[...], preferred_element_type=F32)

    def rope(scale):
        cos = cos_ref[...]
        sin = sin_ref[...]
        outs = []
        for c in range(acc.shape[1] // LANES):
            xc = acc[:, c * LANES:(c + 1) * LANES]
            outs.append((xc * cos + pltpu.roll(xc, LANES // 2, axis=1) * sin) * scale)
        return jnp.concatenate(outs, axis=1)

    @pl.when(j == 0)
    def _():
        q_ref[...] = rope(Q_SCALE).astype(BF16)

    @pl.when(j == 1)
    def _():
        kt_ref[...] = rope(1.0).T.astype(BF16)

    @pl.when(j == 2)
    def _():
        v_ref[...] = acc.astype(BF16)

    @pl.when(j == 3)
    def _():
        u_ref[...] = acc

    @pl.when(j >= 4)
    def _():
        gate_ref[...] = acc.astype(BF16)


def _in_proj(x2, g, w, cos_t, sin_t, seq):
    n = x2.shape[0]
    tm = min(512, seq)
    tn = ATTN_W
    nj = IN_COLS // tn
    pos_blocks = seq // tm
    row = lambda i, j: (i, 0)
    out_sd = lambda cols: jax.ShapeDtypeStruct((n, cols), BF16)
    return pl.pallas_call(
        _in_proj_kernel,
        grid=(n // tm, nj),
        in_specs=[
            pl.BlockSpec((tm, D_MODEL), row),
            pl.BlockSpec((1, D_MODEL), lambda i, j: (0, 0)),
            pl.BlockSpec((D_MODEL, tn), lambda i, j: (0, j)),
            pl.BlockSpec((tm, LANES), lambda i, j: (i % pos_blocks, 0)),
            pl.BlockSpec((tm, LANES), lambda i, j: (i % pos_blocks, 0)),
        ],
        out_specs=[
            pl.BlockSpec((tm, tn), row),
            pl.BlockSpec((tn, tm), lambda i, j: (0, i)),
            pl.BlockSpec((tm, tn), row),
            pl.BlockSpec((tm, tn), row),
            pl.BlockSpec((tm, tn), lambda i, j: (i, jnp.maximum(j - 4, 0))),
        ],
        out_shape=[out_sd(ATTN_W), jax.ShapeDtypeStruct((ATTN_W, n), BF16), out_sd(ATTN_W),
                   jax.ShapeDtypeStruct((n, SSM_W), F32), out_sd(2 * D_MODEL)],
        scratch_shapes=[pltpu.VMEM((tm, D_MODEL), BF16)],
        compiler_params=_params("parallel", "arbitrary"),
        name="in_proj",
    )(x2, g, w, cos_t, sin_t)


def _diff_attn_kernel(dl_ref, sub_ref, q_ref, kt_ref, v_ref, o_ref, s_ref, mx_ref, m_ref, acc_ref,
                      *, tq, tk, unroll):
    seq = v_ref.shape[1]
    nk = seq // tk
    steps = (seq // tq) * nk
    hw = 2 * HEAD_DIM
    lane = lax.broadcasted_iota(jnp.int32, (tq, hw), 1)
    first = ((lane // (HEAD_DIM // 2)) % 2) == 0
    ones = jnp.ones((tk, hw), BF16)
    dl = dl_ref[...]
    lam = (jnp.exp(jnp.sum(dl[0:1] * dl[1:2], axis=-1, keepdims=True))
           - jnp.exp(jnp.sum(dl[2:3] * dl[3:4], axis=-1, keepdims=True)) + LAM_INIT)

    def scores(f, slot):
        f = jnp.minimum(f, steps - 1)
        q = q_ref[0, pl.ds(pl.multiple_of((f // nk) * tq, tq), tq), :]
        kc = kt_ref[:, pl.ds(pl.multiple_of((f % nk) * tk, tk), tk)]
        zero = jnp.zeros_like(q)
        for comp, qm in enumerate((jnp.where(first, q, zero), jnp.where(first, zero, q))):
            s = jnp.dot(qm, kc, preferred_element_type=F32)
            s_ref[slot, comp] = s
            mx_ref[slot, comp] = jnp.max(s, axis=-1, keepdims=True)

    def accumulate(f, slot, emit):
        c = f % nk
        vx = jnp.concatenate([v_ref[0, pl.ds(pl.multiple_of(c * tk, tk), tk), :], ones], axis=1)
        acc = []
        for comp in range(2):
            m_old = jnp.where(c == 0, -jnp.inf, m_ref[comp])
            m_new = jnp.maximum(m_old, mx_ref[slot, comp])
            m_ref[comp] = m_new
            p = jnp.exp2(s_ref[slot, comp] - m_new).astype(BF16)
            acc.append(jnp.exp2(m_old - m_new) * acc_ref[comp]
                       + jnp.dot(p, vx, preferred_element_type=F32))
            acc_ref[comp] = acc[comp]
        if emit:
            o = acc[0][:, :hw] / acc[0][:, hw:] - lam * (acc[1][:, :hw] / acc[1][:, hw:])
            rows = pl.ds(pl.multiple_of((f // nk) * tq, tq), tq)
            o_ref[0, rows, :] = (_rms(o, sub_ref[...]) * (1.0 - LAM_INIT)).astype(BF16)

    m_ref[...] = jnp.full(m_ref.shape, -jnp.inf, F32)
    acc_ref[...] = jnp.zeros(acc_ref.shape, F32)
    scores(0, 0)

    def body(i, _):
        for k in range(unroll):
            f = i * unroll + k
            scores(f + 1, (k + 1) % 2)
            accumulate(f, k % 2, emit=(k + 1) % min(unroll, nk) == 0)
        return 0

    lax.fori_loop(0, steps // unroll, body, 0)


def _diff_attn(q3, kt, v3, dl, sub):
    b, seq, _ = q3.shape
    tq = min(256, seq)
    tk = min(4096, seq // 2)
    nk = seq // tk
    unroll = 4 if ((seq // tq) * nk) % 4 == 0 else 2
    assert nk % unroll == 0 or unroll % nk == 0
    hw = 2 * HEAD_DIM
    blk = pl.BlockSpec((1, seq, hw), lambda b_, h: (b_, 0, h))
    return pl.pallas_call(
        functools.partial(_diff_attn_kernel, tq=tq, tk=tk, unroll=unroll),
        grid=(b, N_HEADS),
        in_specs=[
            pl.BlockSpec((4, HEAD_DIM), lambda b_, h: (0, 0)),
            pl.BlockSpec((1, hw), lambda b_, h: (0, 0)),
            blk,
            pl.BlockSpec((hw, seq), lambda b_, h: (h, b_)),
            blk,
        ],
        out_specs=blk,
        out_shape=jax.ShapeDtypeStruct((b, seq, ATTN_W), BF16),
        scratch_shapes=[pltpu.VMEM((2, 2, tq, tk), F32), pltpu.VMEM((2, 2, tq, 1), F32),
                        pltpu.VMEM((2, tq, 1), F32), pltpu.VMEM((2, tq, 2 * hw), F32)],
        compiler_params=_params("parallel", "arbitrary"),
        name="diff_attn",
    )(dl, sub, q3, kt, v3)


def _slot_transpose(vs):
    lane = lax.broadcasted_iota(jnp.int32, vs[0].shape, 1)
    for d in (4, 2, 1):
        keep = ((lane // GROUP_SIZE) & d) == 0
        new = list(vs)
        for i in range(8):
            if i & d == 0:
                lo, hi = vs[i], vs[i + d]
                new[i] = jnp.where(keep, lo, pltpu.roll(hi, d * GROUP_SIZE, axis=1))
                new[i + d] = jnp.where(keep, pltpu.roll(lo, LANES - d * GROUP_SIZE, axis=1), hi)
        vs = new
    return vs


def _ssm_kernel(u_ref, wi_ref, wb_ref, wc_ref, ap_ref, pt_ref, z_ref, ut_ref, s_ref, zt_ref, *, rb):
    nc = u_ref.shape[0]
    n_tiles = nc // 8
    gw = CHUNK * GROUP_SIZE
    pw = 2 * gw

    def to_chunk_major(r, _):
        rs = pl.ds(pl.multiple_of(r * rb, rb), rb)
        for th in range(CHUNK // 8):
            by_group = _slot_transpose([u_ref[rs, th * 8 + tl, :] for tl in range(8)])
            for g in range(GROUP_BLOCK):
                ut_ref[rs, g * gw + th * LANES:g * gw + (th + 1) * LANES] = by_group[g]
        return 0

    lax.fori_loop(0, nc // rb, to_chunk_major, 0)

    for q in range(GROUP_BLOCK // 2):
        cols = slice(q * pw, (q + 1) * pw)
        s_ref[:, cols] = jnp.dot(ut_ref[:, cols].astype(BF16), wb_ref[0, q],
                                 preferred_element_type=F32)

    row = lax.broadcasted_iota(jnp.int32, (8, LANES), 0)

    def shift_down(x, k):
        return jnp.where(row >= k, pltpu.roll(x, k, axis=0), 0.0)

    def shift_up(x, k):
        return jnp.where(row < 8 - k, pltpu.roll(x, 8 - k, axis=0), 0.0)

    def scan(i, carry):
        new = []
        for q in range(GROUP_BLOCK // 2):
            for d in range(2):
                h_re, h_im = carry[q * 2 + d]
                tile = i if d == 0 else n_tiles - 1 - i
                rs = pl.ds(pl.multiple_of(tile * 8, 8), 8)
                c_re = slice(q * pw + d * 2 * LANES, q * pw + d * 2 * LANES + LANES)
                c_im = slice(q * pw + d * 2 * LANES + LANES, q * pw + (d + 1) * 2 * LANES)
                shift = shift_down if d == 0 else shift_up
                j = (q * 2 + d) * 2
                x_re, x_im = s_ref[rs, c_re], s_ref[rs, c_im]
                for ki, k in enumerate((1, 2, 4)):
                    a_re, a_im = ap_ref[0, j * 4 + ki], ap_ref[0, (j + 1) * 4 + ki]
                    y_re, y_im = shift(x_re, k), shift(x_im, k)
                    x_re, x_im = (x_re + a_re * y_re - a_im * y_im,
                                  x_im + a_re * y_im + a_im * y_re)
                p_re, p_im = pt_ref[0, j], pt_ref[0, j + 1]
                s_ref[rs, c_re] = p_re * h_re - p_im * h_im + shift(x_re, 1)
                s_ref[rs, c_im] = p_re * h_im + p_im * h_re + shift(x_im, 1)
                a_re, a_im = ap_ref[0, j * 4 + 3], ap_ref[0, (j + 1) * 4 + 3]
                edge = slice(7, 8) if d == 0 else slice(0, 1)
                new.append((a_re * h_re - a_im * h_im + jnp.broadcast_to(x_re[edge], (8, LANES)),
                            a_re * h_im + a_im * h_re + jnp.broadcast_to(x_im[edge], (8, LANES))))
        return tuple(new)

    zero = jnp.zeros((8, LANES), F32)
    lax.fori_loop(0, n_tiles, scan, tuple((zero, zero) for _ in range(GROUP_BLOCK)))

    for q in range(GROUP_BLOCK // 2):
        cols = slice(q * pw, (q + 1) * pw)
        carried = jnp.dot(s_ref[:, cols].astype(BF16), wc_ref[0, q], preferred_element_type=F32)
        for g2 in range(2):
            g = 2 * q + g2
            gc = slice(g * gw, (g + 1) * gw)
            y = carried[:, g2 * gw:(g2 + 1) * gw] + jnp.dot(
                ut_ref[:, gc].astype(BF16), wi_ref[0, g], preferred_element_type=F32)
            ut_ref[:, gc] = 0.5 * y * (1.0 + jnp.tanh(0.7978845608028654 * (y + 0.044715 * (y * y * y))))

    def to_token_major(r, _):
        rs = pl.ds(pl.multiple_of(r * rb, rb), rb)
        for th in range(CHUNK // 8):
            by_token = _slot_transpose(
                [ut_ref[rs, g * gw + th * LANES:g * gw + (th + 1) * LANES] for g in range(GROUP_BLOCK)])
            for tl in range(8):
                zt_ref[rs, th * 8 + tl, :] = by_token[tl]
        return 0

    lax.fori_loop(0, nc // rb, to_token_major, 0)
    z_ref[...] = zt_ref[...].astype(BF16)


def _ssm(u3, b, wi, wb, wc, ap, pt):
    nc = u3.shape[0] // b
    cw = GROUP_BLOCK * CHUNK * GROUP_SIZE
    blk = pl.BlockSpec((nc, CHUNK, LANES), lambda g, b_: (b_, 0, g))
    whole = lambda a: pl.BlockSpec((1,) + a.shape[1:], lambda g, b_: (g,) + (0,) * (a.ndim - 1))
    return pl.pallas_call(
        functools.partial(_ssm_kernel, rb=min(64, nc)),
        grid=(N_GROUPS // GROUP_BLOCK, b),
        in_specs=[blk, whole(wi), whole(wb), whole(wc), whole(ap), whole(pt)],
        out_specs=blk,
        out_shape=jax.ShapeDtypeStruct(u3.shape, BF16),
        scratch_shapes=[pltpu.VMEM((nc, cw), F32), pltpu.VMEM((nc, cw), F32),
                        pltpu.VMEM((nc, CHUNK, LANES), F32)],
        compiler_params=_params("parallel", "arbitrary"),
        name="s5_scan",
    )(u3, wi, wb, wc, ap, pt)


def _ssm_operators(lam_re, lam_im, log_dt, b_re, b_im, c_re, c_im, d_skip):
    t_ = CHUNK
    g_, p_, c_ = N_GROUPS, STATE, GROUP_SIZE
    nb, npair = g_ // GROUP_BLOCK, g_ // 2
    lr, li = lam_re.astype(F32), lam_im.astype(F32)
    dt = jnp.exp(log_dt.astype(F32))[..., None]
    mag = jnp.exp(lr * dt)
    ab_re, ab_im = mag * jnp.cos(li * dt), mag * jnp.sin(li * dt)
    den = lr * lr + li * li
    n_re, n_im = ab_re - 1.0, ab_im
    k_re = (n_re * lr + n_im * li) / den
    k_im = (n_im * lr - n_re * li) / den
    br, bi = b_re.astype(F32), b_im.astype(F32)
    bb_re = k_re[..., None] * br - k_im[..., None] * bi
    bb_im = k_re[..., None] * bi + k_im[..., None] * br
    cr, ci = c_re.astype(F32), c_im.astype(F32)

    def cpow(e):
        m = jnp.exp(e * (lr * dt))
        return m * jnp.cos(e * (li * dt)), m * jnp.sin(e * (li * dt))

    pw_re, pw_im = cpow(jnp.arange(t_ + 1, dtype=F32)[:, None, None, None])

    ca_re = cr[None] * pw_re[:t_, :, :, None, :] - ci[None] * pw_im[:t_, :, :, None, :]
    ca_im = cr[None] * pw_im[:t_, :, :, None, :] + ci[None] * pw_re[:t_, :, :, None, :]
    kern = (jnp.einsum('tdgop,dgpc->tdgoc', ca_re, bb_re, precision=HI)
            - jnp.einsum('tdgop,dgpc->tdgoc', ca_im, bb_im, precision=HI))
    tt = jnp.arange(t_)
    lag = tt[None, :] - tt[:, None]
    k_f = kern[jnp.clip(lag, 0, t_ - 1), 0] * (lag >= 0)[..., None, None, None].astype(F32)
    k_b = kern[jnp.clip(-lag, 0, t_ - 1), 1] * (lag <= 0)[..., None, None, None].astype(F32)
    w_intra = (k_f + k_b).transpose(2, 0, 4, 1, 3)
    skip = (jnp.eye(t_, dtype=F32)[None, :, None, :, None]
            * jnp.eye(c_, dtype=F32)[None, None, :, None, :]
            * d_skip.astype(F32).reshape(g_, 1, c_, 1, 1))
    w_intra = (w_intra + skip).reshape(nb, GROUP_BLOCK, t_ * c_, t_ * c_)

    eye2 = jnp.eye(2, dtype=F32)

    def summ(pw_r, pw_i, d):
        re = pw_r[..., None] * bb_re[d][None] - pw_i[..., None] * bb_im[d][None]
        im = pw_r[..., None] * bb_im[d][None] + pw_i[..., None] * bb_re[d][None]
        return re.transpose(1, 0, 3, 2), im.transpose(1, 0, 3, 2)

    f_re, f_im = summ(pw_re[:t_, 0][::-1], pw_im[:t_, 0][::-1], 0)
    r_re, r_im = summ(pw_re[:t_, 1], pw_im[:t_, 1], 1)
    wb = jnp.stack([f_re, f_im, r_re, r_im], axis=3)
    wb = wb.reshape(npair, 2, t_ * c_, 4, p_)
    wb = jnp.einsum('qgkcp,gh->qgkchp', wb, eye2)
    wb = wb.reshape(nb, GROUP_BLOCK // 2, 2 * t_ * c_, 4 * 2 * p_)

    def readout(pw_r, pw_i, d):
        re = cr[d][None] * pw_r[:, :, None, :] - ci[d][None] * pw_i[:, :, None, :]
        im = cr[d][None] * pw_i[:, :, None, :] + ci[d][None] * pw_r[:, :, None, :]
        return re.transpose(1, 3, 0, 2), -im.transpose(1, 3, 0, 2)

    of_re, of_im = readout(pw_re[1:, 0], pw_im[1:, 0], 0)
    ob_re, ob_im = readout(pw_re[1:, 1][::-1], pw_im[1:, 1][::-1], 1)
    wc = jnp.stack([of_re, of_im, ob_re, ob_im], axis=1)
    wc = wc.reshape(npair, 2, 4, p_, t_ * c_)
    wc = jnp.einsum('qgcpk,gh->qcgphk', wc, eye2)
    wc = wc.reshape(nb, GROUP_BLOCK // 2, 4 * 2 * p_, 2 * t_ * c_)

    def table(e):
        re, im = cpow(e)
        tab = jnp.stack([re, im], axis=2)
        k = tab.shape[0]
        tab = tab.reshape(k, 2, 2, nb, GROUP_BLOCK // 2, 2 * p_).transpose(3, 4, 1, 2, 0, 5)
        return tab.reshape(nb, (GROUP_BLOCK // 2) * 4, k, 2 * p_)

    step = jnp.asarray([1.0, 2.0, 4.0, 8.0], F32) * t_
    ap = table(jnp.broadcast_to(step[:, None, None, None], (4, 2, 1, 1)))
    ap = jnp.broadcast_to(ap[:, :, :, None, :], ap.shape[:3] + (8, 2 * p_)).reshape(nb, -1, 8, 2 * p_)
    j8 = jnp.arange(8, dtype=F32) * t_
    pt = table(jnp.stack([j8, j8[::-1]], axis=1)[:, :, None, None])
    return w_intra.astype(BF16), wb.astype(BF16), wc.astype(BF16), ap, pt


def _merge_kernel(att_ref, z_ref, ga_ref, gs_ref, wp_ref, wa_ref, wb_ref, o_ref):
    z = z_ref[...]
    y_attn = jnp.dot(att_ref[...], wp_ref[...], preferred_element_type=F32)
    y_ssm = (jnp.dot(z, wa_ref[...], preferred_element_type=F32)
             * jax.nn.sigmoid(jnp.dot(z, wb_ref[...], preferred_element_type=F32)))
    g_a = jax.nn.sigmoid(ga_ref[...].astype(F32))
    g_s = jax.nn.sigmoid(gs_ref[...].astype(F32))
    o_ref[...] = (g_a * y_attn + g_s * y_ssm).astype(BF16)


def _merge(att, z, gates, wp, wa, wb):
    n = att.shape[0]
    tm = min(1024, n)
    tn = 512
    nj = D_MODEL // tn
    return pl.pallas_call(
        _merge_kernel,
        grid=(n // tm, nj),
        in_specs=[
            pl.BlockSpec((tm, ATTN_W), lambda i, j: (i, 0)),
            pl.BlockSpec((tm, SSM_W), lambda i, j: (i, 0)),
            pl.BlockSpec((tm, tn), lambda i, j: (i, j)),
            pl.BlockSpec((tm, tn), lambda i, j: (i, j + nj)),
            pl.BlockSpec((ATTN_W, tn), lambda i, j: (0, j)),
            pl.BlockSpec((SSM_W, tn), lambda i, j: (0, j)),
            pl.BlockSpec((SSM_W, tn), lambda i, j: (0, j)),
        ],
        out_specs=pl.BlockSpec((tm, tn), lambda i, j: (i, j)),
        out_shape=jax.ShapeDtypeStruct((n, D_MODEL), BF16),
        compiler_params=_params("parallel", "arbitrary"),
        name="merge",
    )(att, z, gates, gates, wp, wa, wb)


def _mm_res_kernel(a_ref, w_ref, r_ref, o_ref):
    o_ref[...] = r_ref[...] + jnp.dot(a_ref[...], w_ref[...], preferred_element_type=F32)


def _mm_res(a, w, res):
    n, kdim = a.shape
    cols = w.shape[1]
    tm = min(1024, n)
    tn = 512
    return pl.pallas_call(
        _mm_res_kernel,
        grid=(n // tm, cols // tn),
        in_specs=[
            pl.BlockSpec((tm, kdim), lambda i, j: (i, 0)),
            pl.BlockSpec((kdim, tn), lambda i, j: (0, j)),
            pl.BlockSpec((tm, tn), lambda i, j: (i, j)),
        ],
        out_specs=pl.BlockSpec((tm, tn), lambda i, j: (i, j)),
        out_shape=jax.ShapeDtypeStruct((n, cols), F32),
        compiler_params=_params("parallel", "arbitrary"),
        name="matmul_residual",
    )(a, w, res)


def _norm_mm_kernel(x_ref, g_ref, w_ref, o_ref, xn_ref, *, scale):
    @pl.when(pl.program_id(1) == 0)
    def _():
        xn_ref[...] = _rms(x_ref[...], g_ref[...]).astype(BF16)

    acc = jnp.dot(xn_ref[...], w_ref[...], preferred_element_type=F32)
    o_ref[...] = (acc * scale).astype(BF16)


def _norm_mm(x, g, w, scale):
    n, kdim = x.shape
    cols = w.shape[1]
    tm = min(512, n)
    tn = 1024
    return pl.pallas_call(
        functools.partial(_norm_mm_kernel, scale=scale),
        grid=(n // tm, cols // tn),
        in_specs=[
            pl.BlockSpec((tm, kdim), lambda i, j: (i, 0)),
            pl.BlockSpec((1, kdim), lambda i, j: (0, 0)),
            pl.BlockSpec((kdim, tn), lambda i, j: (0, j)),
        ],
        out_specs=pl.BlockSpec((tm, tn), lambda i, j: (i, j)),
        out_shape=jax.ShapeDtypeStruct((n, cols), BF16),
        scratch_shapes=[pltpu.VMEM((tm, kdim), BF16)],
        compiler_params=_params("parallel", "arbitrary"),
        name="norm_matmul",
    )(x, g, w)


def _cross_attn_kernel(q_ref, kv_ref, o_ref):
    nt = (((1,), (1,)), ((), ()))
    for h in range(X_HEADS):
        cols = slice(h * X_HEAD_DIM, (h + 1) * X_HEAD_DIM)
        vcols = slice(D_MODEL + h * X_HEAD_DIM, D_MODEL + (h + 1) * X_HEAD_DIM)
        s = lax.dot_general(q_ref[0, :, cols], kv_ref[0, :, cols], nt, preferred_element_type=F32)
        p = jnp.exp(s - jnp.max(s, axis=-1, keepdims=True))
        l = jnp.sum(p, axis=-1, keepdims=True)
        o = jnp.dot(p.astype(BF16), kv_ref[0, :, vcols], preferred_element_type=F32)
        o_ref[0, :, cols] = (o / l).astype(BF16)


def _cross_attn(qc3, kv3):
    b, seq, _ = qc3.shape
    m = kv3.shape[1]
    tq = min(512, seq)
    return pl.pallas_call(
        _cross_attn_kernel,
        grid=(b, seq // tq),
        in_specs=[
            pl.BlockSpec((1, tq, D_MODEL), lambda b_, i: (b_, i, 0)),
            pl.BlockSpec((1, m, 2 * D_MODEL), lambda b_, i: (b_, 0, 0)),
        ],
        out_specs=pl.BlockSpec((1, tq, D_MODEL), lambda b_, i: (b_, i, 0)),
        out_shape=jax.ShapeDtypeStruct((b, seq, D_MODEL), BF16),
        compiler_params=_params("parallel", "arbitrary"),
        name="cross_attn",
    )(qc3, kv3)


def _mlp_kernel(x_ref, g_ref, wu_ref, wd_ref, gf_ref, o_ref, hn_ref, acc_ref):
    f = pl.program_id(1)

    @pl.when(f == 0)
    def _():
        hn_ref[...] = _rms(x_ref[...], g_ref[...]).astype(BF16)
        acc_ref[...] = jnp.zeros_like(acc_ref)

    h = jnp.maximum(jnp.dot(hn_ref[...], wu_ref[...], preferred_element_type=F32), 0.0)
    acc_ref[...] += jnp.dot((h * h).astype(BF16), wd_ref[...], preferred_element_type=F32)

    @pl.when(f == pl.num_programs(1) - 1)
    def _():
        o_ref[...] = _rms(x_ref[...] + acc_ref[...], gf_ref[...])


def _mlp(x, g, wu, wd, gf):
    n = x.shape[0]
    tm = min(512, n)
    tf = 512
    return pl.pallas_call(
        _mlp_kernel,
        grid=(n // tm, D_FF // tf),
        in_specs=[
            pl.BlockSpec((tm, D_MODEL), lambda i, f: (i, 0)),
            pl.BlockSpec((1, D_MODEL), lambda i, f: (0, 0)),
            pl.BlockSpec((D_MODEL, tf), lambda i, f: (0, f)),
            pl.BlockSpec((tf, D_MODEL), lambda i, f: (f, 0)),
            pl.BlockSpec((1, D_MODEL), lambda i, f: (0, 0)),
        ],
        out_specs=pl.BlockSpec((tm, D_MODEL), lambda i, f: (i, 0)),
        out_shape=jax.ShapeDtypeStruct((n, D_MODEL), F32),
        scratch_shapes=[pltpu.VMEM((tm, D_MODEL), BF16), pltpu.VMEM((tm, D_MODEL), F32)],
        compiler_params=_params("parallel", "arbitrary"),
        name="mlp_final_norm",
    )(x, g, wu, wd, gf)


def _rope_tables(seq):
    half = HEAD_DIM // 2
    inv = ROPE_THETA ** (-jnp.arange(0, HEAD_DIM, 2, dtype=F32) / HEAD_DIM)
    ang = jnp.arange(seq, dtype=F32)[:, None] * inv[None, :]
    cos, sin = jnp.cos(ang), jnp.sin(ang)
    cos_t = jnp.tile(cos, (1, LANES // half))
    sin_t = jnp.concatenate([-sin, -sin, sin, sin], axis=1)
    return cos_t, sin_t


def _qk_perm():
    half = HEAD_DIM // 2
    idx = []
    for h in range(N_HEADS):
        for part in range(2):
            for m in range(2):
                base = h * 2 * HEAD_DIM + m * HEAD_DIM + part * half
                idx.extend(range(base, base + half))
    return jnp.asarray(idx, dtype=jnp.int32)


def _prepare(w):
    (norm_mix, w_in, diff_lambda, subln, w_attn_proj,
     lam_re, lam_im, log_dt, b_re, b_im, c_re, c_im, d_skip,
     w_glu_a, w_glu_b, w_mix_out,
     norm_cross, norm_mem, w_q_cross, w_kv_cross, w_o_cross,
     norm_mlp, w_mlp_up, w_mlp_down, norm_final) = w
    perm = _qk_perm()
    w_in0 = w_in[0]
    w_in_p = jnp.concatenate([w_in0[:, perm], w_in0[:, ATTN_W + perm], w_in0[:, 2 * ATTN_W:]],
                             axis=1).astype(BF16)
    row = lambda v: v.astype(F32).reshape(1, -1)
    return dict(
        norm_mix=row(norm_mix[0]), w_in=w_in_p, diff_lambda=diff_lambda[0].astype(F32),
        subln=row(subln[0]), w_attn_proj=w_attn_proj[0].astype(BF16),
        ssm=_ssm_operators(lam_re[0], lam_im[0], log_dt[0], b_re[0], b_im[0], c_re[0], c_im[0],
                           d_skip[0]),
        w_glu_a=w_glu_a[0].astype(BF16), w_glu_b=w_glu_b[0].astype(BF16),
        w_mix_out=w_mix_out[0].astype(BF16),
        norm_cross=row(norm_cross[0]), norm_mem=row(norm_mem[0]),
        w_q_cross=w_q_cross[0].astype(BF16), w_kv_cross=w_kv_cross[0].astype(BF16),
        w_o_cross=w_o_cross[0].astype(BF16),
        norm_mlp=row(norm_mlp[0]), w_mlp_up=w_mlp_up[0].astype(BF16),
        w_mlp_down=w_mlp_down[0].astype(BF16), norm_final=row(norm_final),
    )


def _encode(x, mem, p):
    b, seq, _ = x.shape
    assert seq % (8 * CHUNK) == 0
    n = b * seq
    nc = seq // CHUNK
    x2 = x.reshape(n, D_MODEL)
    cos_t, sin_t = _rope_tables(seq)

    q, kt, v, u, gates = _in_proj(x2, p["norm_mix"], p["w_in"], cos_t, sin_t, seq)
    att = _diff_attn(q.reshape(b, seq, ATTN_W), kt, v.reshape(b, seq, ATTN_W),
                     p["diff_lambda"], p["subln"])

    z = _ssm(u.reshape(b * nc, CHUNK, SSM_W), b, *p["ssm"]).reshape(n, SSM_W)

    mixed = _merge(att.reshape(n, ATTN_W), z, gates, p["w_attn_proj"], p["w_glu_a"], p["w_glu_b"])
    x2 = _mm_res(mixed, p["w_mix_out"], x2)

    qc = _norm_mm(x2, p["norm_cross"], p["w_q_cross"], X_HEAD_DIM ** -0.5)
    m_tok = mem.shape[1]
    kv = _norm_mm(mem.reshape(b * m_tok, D_MODEL), p["norm_mem"], p["w_kv_cross"], 1.0)
    oc = _cross_attn(qc.reshape(b, seq, D_MODEL), kv.reshape(b, m_tok, 2 * D_MODEL))
    x2 = _mm_res(oc.reshape(n, D_MODEL), p["w_o_cross"], x2)

    out = _mlp(x2, p["norm_mlp"], p["w_mlp_up"], p["w_mlp_down"], p["norm_final"])
    return out.reshape(b, seq, D_MODEL)


def kernel(x_prompt, x_sample, mem_prompt, mem_sample, norm_mix, w_in, diff_lambda, subln, w_attn_proj, ssm_lambda_re, ssm_lambda_im, ssm_log_dt, ssm_b_re, ssm_b_im, ssm_c_re, ssm_c_im, ssm_d, w_glu_a, w_glu_b, w_mix_out, norm_cross, norm_mem, w_q_cross, w_kv_cross, w_o_cross, norm_mlp, w_mlp_up, w_mlp_down, norm_final):
    p = _prepare((norm_mix, w_in, diff_lambda, subln, w_attn_proj,
                  ssm_lambda_re, ssm_lambda_im, ssm_log_dt, ssm_b_re, ssm_b_im, ssm_c_re, ssm_c_im,
                  ssm_d, w_glu_a, w_glu_b, w_mix_out,
                  norm_cross, norm_mem, w_q_cross, w_kv_cross, w_o_cross,
                  norm_mlp, w_mlp_up, w_mlp_down, norm_final))
    return (_encode(x_prompt, mem_prompt, p), _encode(x_sample, mem_sample, p))
```

```python
import functools
import math

import jax
import jax.numpy as jnp
from jax import lax
from jax.experimental import pallas as pl
from jax.experimental.pallas import tpu as pltpu

D_MODEL = 2048
N_HEADS = 8
HEAD_DIM = 64
ATTN_W = N_HEADS * 2 * HEAD_DIM
SSM_W = D_MODEL // 2
GROUP_SIZE = 16
N_GROUPS = SSM_W // GROUP_SIZE
STATE = 64
IN_COLS = 3 * ATTN_W + SSM_W + 2 * D_MODEL
D_FF = 4 * D_MODEL
X_HEADS = 4
X_HEAD_DIM = D_MODEL // X_HEADS
ROPE_THETA = 10000.0
EPS = 1e-6
LAM_INIT = 0.8 - 0.6 * math.exp(-0.3 * 0)
Q_SCALE = HEAD_DIM ** -0.5 * math.log2(math.e)

CHUNK = 16
GROUP_BLOCK = 8
LANES = 128
VMEM_LIMIT = 56 * 1024 * 1024

F32 = jnp.float32
BF16 = jnp.bfloat16
HI = lax.Precision.HIGHEST


def _params(*sem):
    return pltpu.CompilerParams(dimension_semantics=sem, vmem_limit_bytes=VMEM_LIMIT)


def _rms(x, g):
    return x * lax.rsqrt(jnp.mean(x * x, axis=-1, keepdims=True) + EPS) * g


def _in_proj_kernel(x_ref, g_ref, w_ref, cos_ref, sin_ref,
                    q_ref, kt_ref, v_ref, u_ref, gate_ref, xn_ref):
    j = pl.program_id(1)

    @pl.when(j == 0)
    def _():
        xn_ref[...] = _rms(x_ref[...], g_ref[...]).astype(BF16)

    acc = jnp.dot(xn_ref[...], w_ref[...], preferred_element_type=F32)

    def rope(scale):
        cos = cos_ref[...]
        sin = sin_ref[...]
        outs = []
        for c in range(acc.shape[1] // LANES):
            xc = acc[:, c * LANES:(c + 1) * LANES]
            outs.append((xc * cos + pltpu.roll(xc, LANES // 2, axis=1) * sin) * scale)
        return jnp.concatenate(outs, axis=1)

    @pl.when(j == 0)
    def _():
        q_ref[...] = rope(Q_SCALE).astype(BF16)

    @pl.when(j == 1)
    def _():
        kt_ref[...] = rope(1.0).T.astype(BF16)

    @pl.when(j == 2)
    def _():
        v_ref[...] = acc.astype(BF16)

    @pl.when(j == 3)
    def _():
        u_ref[...] = acc

    @pl.when(j >= 4)
    def _():
        gate_ref[...] = acc.astype(BF16)


def _in_proj(x2, g, w, cos_t, sin_t, seq):
    n = x2.shape[0]
    tm = min(512, seq)
    tn = ATTN_W
    nj = IN_COLS // tn
    pos_blocks = seq // tm
    row = lambda i, j: (i, 0)
    out_sd = lambda cols: jax.ShapeDtypeStruct((n, cols), BF16)
    return pl.pallas_call(
        _in_proj_kernel,
        grid=(n // tm, nj),
        in_specs=[
            pl.BlockSpec((tm, D_MODEL), row),
            pl.BlockSpec((1, D_MODEL), lambda i, j: (0, 0)),
            pl.BlockSpec((D_MODEL, tn), lambda i, j: (0, j)),
            pl.BlockSpec((tm, LANES), lambda i, j: (i % pos_blocks, 0)),
            pl.BlockSpec((tm, LANES), lambda i, j: (i % pos_blocks, 0)),
        ],
        out_specs=[
            pl.BlockSpec((tm, tn), row),
            pl.BlockSpec((tn, tm), lambda i, j: (0, i)),
            pl.BlockSpec((tm, tn), row),
            pl.BlockSpec((tm, tn), row),
            pl.BlockSpec((tm, tn), lambda i, j: (i, jnp.maximum(j - 4, 0))),
        ],
        out_shape=[out_sd(ATTN_W), jax.ShapeDtypeStruct((ATTN_W, n), BF16), out_sd(ATTN_W),
                   jax.ShapeDtypeStruct((n, SSM_W), F32), out_sd(2 * D_MODEL)],
        scratch_shapes=[pltpu.VMEM((tm, D_MODEL), BF16)],
        compiler_params=_params("parallel", "arbitrary"),
        name="in_proj",
    )(x2, g, w, cos_t, sin_t)


def _diff_attn_kernel(dl_ref, sub_ref, q_ref, kt_ref, v_ref, o_ref, s_ref, mx_ref, m_ref, acc_ref,
                      *, tq, tk, unroll):
    seq = v_ref.shape[1]
    nk = seq // tk
    steps = (seq // tq) * nk
    hw = 2 * HEAD_DIM
    lane = lax.broadcasted_iota(jnp.int32, (tq, hw), 1)
    first = ((lane // (HEAD_DIM // 2)) % 2) == 0
    ones = jnp.ones((tk, hw), BF16)
    dl = dl_ref[...]
    lam = (jnp.exp(jnp.sum(dl[0:1] * dl[1:2], axis=-1, keepdims=True))
           - jnp.exp(jnp.sum(dl[2:3] * dl[3:4], axis=-1, keepdims=True)) + LAM_INIT)

    def scores(f, slot):
        f = jnp.minimum(f, steps - 1)
        q = q_ref[0, pl.ds(pl.multiple_of((f // nk) * tq, tq), tq), :]
        kc = kt_ref[:, pl.ds(pl.multiple_of((f % nk) * tk, tk), tk)]
        zero = jnp.zeros_like(q)
        for comp, qm in enumerate((jnp.where(first, q, zero), jnp.where(first, zero, q))):
            s = jnp.dot(qm, kc, preferred_element_type=F32)
            s_ref[slot, comp] = s
            mx_ref[slot, comp] = jnp.max(s, axis=-1, keepdims=True)

    def accumulate(f, slot, emit):
        c = f % nk
        vx = jnp.concatenate([v_ref[0, pl.ds(pl.multiple_of(c * tk, tk), tk), :], ones], axis=1)
        acc = []
        for comp in range(2):
            m_old = jnp.where(c == 0, -jnp.inf, m_ref[comp])
            m_new = jnp.maximum(m_old, mx_ref[slot, comp])
            m_ref[comp] = m_new
            p = jnp.exp2(s_ref[slot, comp] - m_new).astype(BF16)
            acc.append(jnp.exp2(m_old - m_new) * acc_ref[comp]
                       + jnp.dot(p, vx, preferred_element_type=F32))
            acc_ref[comp] = acc[comp]
        if emit:
            o = acc[0][:, :hw] / acc[0][:, hw:] - lam * (acc[1][:, :hw] / acc[1][:, hw:])
            rows = pl.ds(pl.multiple_of((f // nk) * tq, tq), tq)
            o_ref[0, rows, :] = (_rms(o, sub_ref[...]) * (1.0 - LAM_INIT)).astype(BF16)

    m_ref[...] = jnp.full(m_ref.shape, -jnp.inf, F32)
    acc_ref[...] = jnp.zeros(acc_ref.shape, F32)
    scores(0, 0)

    def body(i, _):
        for k in range(unroll):
            f = i * unroll + k
            scores(f + 1, (k + 1) % 2)
            accumulate(f, k % 2, emit=(k + 1) % min(unroll, nk) == 0)
        return 0

    lax.fori_loop(0, steps // unroll, body, 0)


def _diff_attn(q3, kt, v3, dl, sub):
    b, seq, _ = q3.shape
    tq = min(256, seq)
    tk = min(4096, seq // 2)
    nk = seq // tk
    unroll = 4 if ((seq // tq) * nk) % 4 == 0 else 2
    assert nk % unroll == 0 or unroll % nk == 0
    hw = 2 * HEAD_DIM
    blk = pl.BlockSpec((1, seq, hw), lambda b_, h: (b_, 0, h))
    return pl.pallas_call(
        functools.partial(_diff_attn_kernel, tq=tq, tk=tk, unroll=unroll),
        grid=(b, N_HEADS),
        in_specs=[
            pl.BlockSpec((4, HEAD_DIM), lambda b_, h: (0, 0)),
            pl.BlockSpec((1, hw), lambda b_, h: (0, 0)),
            blk,
            pl.BlockSpec((hw, seq), lambda b_, h: (h, b_)),
            blk,
        ],
        out_specs=blk,
        out_shape=jax.ShapeDtypeStruct((b, seq, ATTN_W), BF16),
        scratch_shapes=[pltpu.VMEM((2, 2, tq, tk), F32), pltpu.VMEM((2, 2, tq, 1), F32),
                        pltpu.VMEM((2, tq, 1), F32), pltpu.VMEM((2, tq, 2 * hw), F32)],
        compiler_params=_params("parallel", "arbitrary"),
        name="diff_attn",
    )(dl, sub, q3, kt, v3)


def _slot_swap_matrix():
    src = jnp.arange(8 * LANES).reshape(8, 8, GROUP_SIZE).transpose(1, 0, 2).reshape(-1)
    return (src[None, :] == jnp.arange(8 * LANES)[:, None]).astype(BF16)


def _ssm_kernel(u_ref, perm_ref, wi_ref, wb_ref, wc_ref, ap_ref, a8_ref, pt_ref, z_ref,
                ut_ref, s_ref, zt_ref, e_ref, hc_ref):
    nc = u_ref.shape[0]
    n_tiles = nc // 8
    gw = CHUNK * GROUP_SIZE
    pw = 2 * gw

    for th in range(CHUNK // 8):
        by_token = jnp.concatenate([u_ref[:, th * 8 + tl, :] for tl in range(8)], axis=1)
        by_group = jnp.dot(by_token.astype(BF16), perm_ref[...], preferred_element_type=F32)
        for g in range(GROUP_BLOCK):
            ut_ref[:, g * gw + th * LANES:g * gw + (th + 1) * LANES] = (
                by_group[:, g * LANES:(g + 1) * LANES].astype(BF16))

    for q in range(GROUP_BLOCK // 2):
        cols = slice(q * pw, (q + 1) * pw)
        s_ref[:, cols] = jnp.dot(ut_ref[:, cols], wb_ref[0, q], preferred_element_type=F32)

    row = lax.broadcasted_iota(jnp.int32, (8, LANES), 0)

    def shift_down(x, k):
        return jnp.where(row >= k, pltpu.roll(x, k, axis=0), 0.0)

    def shift_up(x, k):
        return jnp.where(row < 8 - k, pltpu.roll(x, 8 - k, axis=0), 0.0)

    n_chain = 2 * (GROUP_BLOCK // 2)

    def chain_cols(q, d):
        base = q * pw + d * 2 * LANES
        return slice(base, base + LANES), slice(base + LANES, base + 2 * LANES)

    def local_scan(i, _):
        rs = pl.ds(pl.multiple_of(i * 8, 8), 8)
        for q in range(GROUP_BLOCK // 2):
            for d in range(2):
                c_re, c_im = chain_cols(q, d)
                shift = shift_down if d == 0 else shift_up
                ch = q * 2 + d
                x_re, x_im = s_ref[rs, c_re], s_ref[rs, c_im]
                for ki, k in enumerate((1, 2, 4)):
                    a_re, a_im = ap_ref[0, (2 * ch) * 3 + ki], ap_ref[0, (2 * ch + 1) * 3 + ki]
                    y_re, y_im = shift(x_re, k), shift(x_im, k)
                    x_re, x_im = (x_re + a_re * y_re - a_im * y_im,
                                  x_im + a_re * y_im + a_im * y_re)
                s_ref[rs, c_re] = shift(x_re, 1)
                s_ref[rs, c_im] = shift(x_im, 1)
                pos = i if d == 0 else n_tiles - 1 - i
                edge = slice(7, 8) if d == 0 else slice(0, 1)
                e_ref[pos, ch:ch + 1, :] = x_re[edge]
                e_ref[pos, n_chain + ch:n_chain + ch + 1, :] = x_im[edge]
        return 0

    lax.fori_loop(0, n_tiles, local_scan, 0)

    a8_re, a8_im = a8_ref[0, :n_chain], a8_ref[0, n_chain:]

    def carry_in(i, h):
        h_re, h_im = h
        hc_ref[i, :n_chain, :] = h_re
        hc_ref[i, n_chain:, :] = h_im
        e_re, e_im = e_ref[i, :n_chain, :], e_ref[i, n_chain:, :]
        return a8_re * h_re - a8_im * h_im + e_re, a8_re * h_im + a8_im * h_re + e_im

    zero = jnp.zeros((n_chain, LANES), F32)
    lax.fori_loop(0, n_tiles, carry_in, (zero, zero))

    def add_carry(i, _):
        rs = pl.ds(pl.multiple_of(i * 8, 8), 8)
        for q in range(GROUP_BLOCK // 2):
            for d in range(2):
                c_re, c_im = chain_cols(q, d)
                ch = q * 2 + d
                pos = i if d == 0 else n_tiles - 1 - i
                h_re = jnp.broadcast_to(hc_ref[pos, ch:ch + 1, :], (8, LANES))
                h_im = jnp.broadcast_to(hc_ref[pos, n_chain + ch:n_chain + ch + 1, :], (8, LANES))
                p_re, p_im = pt_ref[0, 2 * ch], pt_ref[0, 2 * ch + 1]
                s_ref[rs, c_re] += p_re * h_re - p_im * h_im
                s_ref[rs, c_im] += p_re * h_im + p_im * h_re
        return 0

    lax.fori_loop(0, n_tiles, add_carry, 0)

    for q in range(GROUP_BLOCK // 2):
        cols = slice(q * pw, (q + 1) * pw)
        carried = jnp.dot(s_ref[:, cols].astype(BF16), wc_ref[0, q], preferred_element_type=F32)
        for g2 in range(2):
            g = 2 * q + g2
            gc = slice(g * gw, (g + 1) * gw)
            y = carried[:, g2 * gw:(g2 + 1) * gw] + jnp.dot(
                ut_ref[:, gc], wi_ref[0, g], preferred_element_type=F32)
            z = 0.5 * y * (1.0 + jnp.tanh(0.7978845608028654 * (y + 0.044715 * (y * y * y))))
            ut_ref[:, gc] = z.astype(BF16)

    for th in range(CHUNK // 8):
        by_group = jnp.concatenate(
            [ut_ref[:, g * gw + th * LANES:g * gw + (th + 1) * LANES] for g in range(GROUP_BLOCK)], axis=1)
        by_token = jnp.dot(by_group, perm_ref[...], preferred_element_type=F32)
        for tl in range(8):
            zt_ref[:, th * 8 + tl, :] = by_token[:, tl * LANES:(tl + 1) * LANES]
    z_ref[...] = zt_ref[...].astype(BF16)


def _ssm(u3, b, wi, wb, wc, ap, a8, pt):
    nc = u3.shape[0] // b
    cw = GROUP_BLOCK * CHUNK * GROUP_SIZE
    blk = pl.BlockSpec((nc, CHUNK, LANES), lambda g, b_: (b_, 0, g))
    whole = lambda a: pl.BlockSpec((1,) + a.shape[1:], lambda g, b_: (g,) + (0,) * (a.ndim - 1))
    carry = pltpu.VMEM((nc // 8, 2 * GROUP_BLOCK, LANES), F32)
    perm = _slot_swap_matrix()
    return pl.pallas_call(
        _ssm_kernel,
        grid=(N_GROUPS // GROUP_BLOCK, b),
        in_specs=[blk, pl.BlockSpec(perm.shape, lambda g, b_: (0, 0)),
                  whole(wi), whole(wb), whole(wc), whole(ap), whole(a8), whole(pt)],
        out_specs=blk,
        out_shape=jax.ShapeDtypeStruct(u3.shape, BF16),
        scratch_shapes=[pltpu.VMEM((nc, cw), BF16), pltpu.VMEM((nc, cw), F32),
                        pltpu.VMEM((nc, CHUNK, LANES), F32), carry, carry],
        compiler_params=_params("parallel", "arbitrary"),
        name="s5_scan",
    )(u3, perm, wi, wb, wc, ap, a8, pt)


def _ssm_operators(lam_re, lam_im, log_dt, b_re, b_im, c_re, c_im, d_skip):
    t_ = CHUNK
    g_, p_, c_ = N_GROUPS, STATE, GROUP_SIZE
    nb, npair = g_ // GROUP_BLOCK, g_ // 2
    lr, li = lam_re.astype(F32), lam_im.astype(F32)
    dt = jnp.exp(log_dt.astype(F32))[..., None]
    mag = jnp.exp(lr * dt)
    ab_re, ab_im = mag * jnp.cos(li * dt), mag * jnp.sin(li * dt)
    den = lr * lr + li * li
    n_re, n_im = ab_re - 1.0, ab_im
    k_re = (n_re * lr + n_im * li) / den
    k_im = (n_im * lr - n_re * li) / den
    br, bi = b_re.astype(F32), b_im.astype(F32)
    bb_re = k_re[..., None] * br - k_im[..., None] * bi
    bb_im = k_re[..., None] * bi + k_im[..., None] * br
    cr, ci = c_re.astype(F32), c_im.astype(F32)

    def cpow(e):
        m = jnp.exp(e * (lr * dt))
        return m * jnp.cos(e * (li * dt)), m * jnp.sin(e * (li * dt))

    pw_re, pw_im = cpow(jnp.arange(t_ + 1, dtype=F32)[:, None, None, None])

    ca_re = cr[None] * pw_re[:t_, :, :, None, :] - ci[None] * pw_im[:t_, :, :, None, :]
    ca_im = cr[None] * pw_im[:t_, :, :, None, :] + ci[None] * pw_re[:t_, :, :, None, :]
    kern = (jnp.einsum('tdgop,dgpc->tdgoc', ca_re, bb_re, precision=HI)
            - jnp.einsum('tdgop,dgpc->tdgoc', ca_im, bb_im, precision=HI))
    tt = jnp.arange(t_)
    lag = tt[None, :] - tt[:, None]
    k_f = kern[jnp.clip(lag, 0, t_ - 1), 0] * (lag >= 0)[..., None, None, None].astype(F32)
    k_b = kern[jnp.clip(-lag, 0, t_ - 1), 1] * (lag <= 0)[..., None, None, None].astype(F32)
    w_intra = (k_f + k_b).transpose(2, 0, 4, 1, 3)
    skip = (jnp.eye(t_, dtype=F32)[None, :, None, :, None]
            * jnp.eye(c_, dtype=F32)[None, None, :, None, :]
            * d_skip.astype(F32).reshape(g_, 1, c_, 1, 1))
    w_intra = (w_intra + skip).reshape(nb, GROUP_BLOCK, t_ * c_, t_ * c_)

    eye2 = jnp.eye(2, dtype=F32)

    def summ(pw_r, pw_i, d):
        re = pw_r[..., None] * bb_re[d][None] - pw_i[..., None] * bb_im[d][None]
        im = pw_r[..., None] * bb_im[d][None] + pw_i[..., None] * bb_re[d][None]
        return re.transpose(1, 0, 3, 2), im.transpose(1, 0, 3, 2)

    f_re, f_im = summ(pw_re[:t_, 0][::-1], pw_im[:t_, 0][::-1], 0)
    r_re, r_im = summ(pw_re[:t_, 1], pw_im[:t_, 1], 1)
    wb = jnp.stack([f_re, f_im, r_re, r_im], axis=3)
    wb = wb.reshape(npair, 2, t_ * c_, 4, p_)
    wb = jnp.einsum('qgkcp,gh->qgkchp', wb, eye2)
    wb = wb.reshape(nb, GROUP_BLOCK // 2, 2 * t_ * c_, 4 * 2 * p_)

    def readout(pw_r, pw_i, d):
        re = cr[d][None] * pw_r[:, :, None, :] - ci[d][None] * pw_i[:, :, None, :]
        im = cr[d][None] * pw_i[:, :, None, :] + ci[d][None] * pw_r[:, :, None, :]
        return re.transpose(1, 3, 0, 2), -im.transpose(1, 3, 0, 2)

    of_re, of_im = readout(pw_re[1:, 0], pw_im[1:, 0], 0)
    ob_re, ob_im = readout(pw_re[1:, 1][::-1], pw_im[1:, 1][::-1], 1)
    wc = jnp.stack([of_re, of_im, ob_re, ob_im], axis=1)
    wc = wc.reshape(npair, 2, 4, p_, t_ * c_)
    wc = jnp.einsum('qgcpk,gh->qcgphk', wc, eye2)
    wc = wc.reshape(nb, GROUP_BLOCK // 2, 4 * 2 * p_, 2 * t_ * c_)

    def table(e):
        re, im = cpow(e)
        tab = jnp.stack([re, im], axis=2)
        k = tab.shape[0]
        tab = tab.reshape(k, 2, 2, nb, GROUP_BLOCK // 2, 2 * p_).transpose(3, 4, 1, 2, 0, 5)
        return tab.reshape(nb, (GROUP_BLOCK // 2) * 4, k, 2 * p_)

    step = jnp.asarray([1.0, 2.0, 4.0], F32) * t_
    ap = table(jnp.broadcast_to(step[:, None, None, None], (3, 2, 1, 1)))
    ap = jnp.broadcast_to(ap[:, :, :, None, :], ap.shape[:3] + (8, 2 * p_)).reshape(nb, -1, 8, 2 * p_)
    a8 = table(jnp.full((1, 2, 1, 1), 8.0 * t_, F32))
    a8 = a8.reshape(nb, GROUP_BLOCK // 2, 2, 2, 2 * p_).transpose(0, 3, 1, 2, 4).reshape(nb, -1, 2 * p_)
    j8 = jnp.arange(8, dtype=F32) * t_
    pt = table(jnp.stack([j8, j8[::-1]], axis=1)[:, :, None, None])
    return w_intra.astype(BF16), wb.astype(BF16), wc.astype(BF16), ap, a8, pt


def _resident(a):
    return pl.BlockSpec(a.shape, lambda *_: (0,) * a.ndim, pipeline_mode=pl.Buffered(1))


def _merge_kernel(x_ref, att_ref, z_ref, gate_ref, wp_ref, wa_ref, wb_ref, wo_ref, o_ref, mix_ref,
                  *, tn):
    att = att_ref[...]
    z = z_ref[...]
    for j in range(D_MODEL // tn):
        cs = slice(j * tn, (j + 1) * tn)
        gs = slice(D_MODEL + j * tn, D_MODEL + (j + 1) * tn)
        y_attn = jnp.dot(att, wp_ref[:, cs], preferred_element_type=F32)
        y_ssm = (jnp.dot(z, wa_ref[:, cs], preferred_element_type=F32)
                 * jax.nn.sigmoid(jnp.dot(z, wb_ref[:, cs], preferred_element_type=F32)))
        g_a = jax.nn.sigmoid(gate_ref[:, cs].astype(F32))
        g_s = jax.nn.sigmoid(gate_ref[:, gs].astype(F32))
        mix_ref[:, cs] = (g_a * y_attn + g_s * y_ssm).astype(BF16)
    o_ref[...] = x_ref[...] + jnp.dot(mix_ref[...], wo_ref[...], preferred_element_type=F32)


def _merge(x2, att, z, gates, wp, wa, wb, wo):
    n = att.shape[0]
    tm = min(256, n)
    row = lambda cols: pl.BlockSpec((tm, cols), lambda i: (i, 0))
    return pl.pallas_call(
        functools.partial(_merge_kernel, tn=512),
        grid=(n // tm,),
        in_specs=[row(D_MODEL), row(ATTN_W), row(SSM_W), row(2 * D_MODEL),
                  _resident(wp), _resident(wa), _resident(wb), _resident(wo)],
        out_specs=row(D_MODEL),
        out_shape=jax.ShapeDtypeStruct((n, D_MODEL), F32),
        scratch_shapes=[pltpu.VMEM((tm, D_MODEL), BF16)],
        compiler_params=_params("parallel"),
        name="merge_mix_out",
    )(x2, att, z, gates, wp, wa, wb, wo)


def _norm_mm_kernel(x_ref, g_ref, w_ref, o_ref, xn_ref, *, scale):
    @pl.when(pl.program_id(1) == 0)
    def _():
        xn_ref[...] = _rms(x_ref[...], g_ref[...]).astype(BF16)

    acc = jnp.dot(xn_ref[...], w_ref[...], preferred_element_type=F32)
    o_ref[...] = (acc * scale).astype(BF16)


def _norm_mm(x, g, w, scale):
    n, kdim = x.shape
    cols = w.shape[1]
    tm = min(512, n)
    tn = 1024
    return pl.pallas_call(
        functools.partial(_norm_mm_kernel, scale=scale),
        grid=(n // tm, cols // tn),
        in_specs=[
            pl.BlockSpec((tm, kdim), lambda i, j: (i, 0)),
            pl.BlockSpec((1, kdim), lambda i, j: (0, 0)),
            pl.BlockSpec((kdim, tn), lambda i, j: (0, j)),
        ],
        out_specs=pl.BlockSpec((tm, tn), lambda i, j: (i, j)),
        out_shape=jax.ShapeDtypeStruct((n, cols), BF16),
        scratch_shapes=[pltpu.VMEM((tm, kdim), BF16)],
        compiler_params=_params("parallel", "arbitrary"),
        name="norm_matmul",
    )(x, g, w)


def _cross_kernel(x_ref, g_ref, wq_ref, kv_ref, wo_ref, o_ref, q_ref, oc_ref):
    x = x_ref[0]
    hn = _rms(x, g_ref[...]).astype(BF16)
    q_ref[...] = (jnp.dot(hn, wq_ref[...], preferred_element_type=F32)
                  * X_HEAD_DIM ** -0.5).astype(BF16)
    nt = (((1,), (1,)), ((), ()))
    for h in range(X_HEADS):
        cols = slice(h * X_HEAD_DIM, (h + 1) * X_HEAD_DIM)
        vcols = slice(D_MODEL + h * X_HEAD_DIM, D_MODEL + (h + 1) * X_HEAD_DIM)
        s = lax.dot_general(q_ref[:, cols], kv_ref[0, :, cols], nt, preferred_element_type=F32)
        p = jnp.exp(s - jnp.max(s, axis=-1, keepdims=True))
        l = jnp.sum(p, axis=-1, keepdims=True)
        o = jnp.dot(p.astype(BF16), kv_ref[0, :, vcols], preferred_element_type=F32)
        oc_ref[:, cols] = (o / l).astype(BF16)
    o_ref[0] = x + jnp.dot(oc_ref[...], wo_ref[...], preferred_element_type=F32)


def _cross(x3, g, wq, kv3, wo):
    b, seq, _ = x3.shape
    m = kv3.shape[1]
    tq = min(512, seq)
    blk = pl.BlockSpec((1, tq, D_MODEL), lambda b_, i: (b_, i, 0))
    return pl.pallas_call(
        _cross_kernel,
        grid=(b, seq // tq),
        in_specs=[blk, _resident(g), _resident(wq),
                  pl.BlockSpec((1, m, 2 * D_MODEL), lambda b_, i: (b_, 0, 0)), _resident(wo)],
        out_specs=blk,
        out_shape=jax.ShapeDtypeStruct(x3.shape, F32),
        scratch_shapes=[pltpu.VMEM((tq, D_MODEL), BF16), pltpu.VMEM((tq, D_MODEL), BF16)],
        compiler_params=_params("parallel", "arbitrary"),
        name="cross_attn_block",
    )(x3, g, wq, kv3, wo)


def _mlp_kernel(x_ref, g_ref, wu_ref, wd_ref, gf_ref, o_ref, hn_ref, acc_ref):
    f = pl.program_id(1)

    @pl.when(f == 0)
    def _():
        hn_ref[...] = _rms(x_ref[...], g_ref[...]).astype(BF16)
        acc_ref[...] = jnp.zeros_like(acc_ref)

    h = jnp.maximum(jnp.dot(hn_ref[...], wu_ref[...], preferred_element_type=F32), 0.0)
    acc_ref[...] += jnp.dot((h * h).astype(BF16), wd_ref[...], preferred_element_type=F32)

    @pl.when(f == pl.num_programs(1) - 1)
    def _():
        o_ref[...] = _rms(x_ref[...] + acc_ref[...], gf_ref[...])


def _mlp(x, g, wu, wd, gf):
    n = x.shape[0]
    tm = min(512, n)
    tf = 512
    return pl.pallas_call(
        _mlp_kernel,
        grid=(n // tm, D_FF // tf),
        in_specs=[
            pl.BlockSpec((tm, D_MODEL), lambda i, f: (i, 0)),
            pl.BlockSpec((1, D_MODEL), lambda i, f: (0, 0)),
            pl.BlockSpec((D_MODEL, tf), lambda i, f: (0, f)),
            pl.BlockSpec((tf, D_MODEL), lambda i, f: (f, 0)),
            pl.BlockSpec((1, D_MODEL), lambda i, f: (0, 0)),
        ],
        out_specs=pl.BlockSpec((tm, D_MODEL), lambda i, f: (i, 0)),
        out_shape=jax.ShapeDtypeStruct((n, D_MODEL), F32),
        scratch_shapes=[pltpu.VMEM((tm, D_MODEL), BF16), pltpu.VMEM((tm, D_MODEL), F32)],
        compiler_params=_params("parallel", "arbitrary"),
        name="mlp_final_norm",
    )(x, g, wu, wd, gf)


def _rope_tables(seq):
    half = HEAD_DIM // 2
    inv = ROPE_THETA ** (-jnp.arange(0, HEAD_DIM, 2, dtype=F32) / HEAD_DIM)
    ang = jnp.arange(seq, dtype=F32)[:, None] * inv[None, :]
    cos, sin = jnp.cos(ang), jnp.sin(ang)
    cos_t = jnp.tile(cos, (1, LANES // half))
    sin_t = jnp.concatenate([-sin, -sin, sin, sin], axis=1)
    return cos_t, sin_t


def _qk_perm():
    half = HEAD_DIM // 2
    idx = []
    for h in range(N_HEADS):
        for part in range(2):
            for m in range(2):
                base = h * 2 * HEAD_DIM + m * HEAD_DIM + part * half
                idx.extend(range(base, base + half))
    return jnp.asarray(idx, dtype=jnp.int32)


def _prepare(w):
    (norm_mix, w_in, diff_lambda, subln, w_attn_proj,
     lam_re, lam_im, log_dt, b_re, b_im, c_re, c_im, d_skip,
     w_glu_a, w_glu_b, w_mix_out,
     norm_cross, norm_mem, w_q_cross, w_kv_cross, w_o_cross,
     norm_mlp, w_mlp_up, w_mlp_down, norm_final) = w
    perm = _qk_perm()
    w_in0 = w_in[0]
    w_in_p = jnp.concatenate([w_in0[:, perm], w_in0[:, ATTN_W + perm], w_in0[:, 2 * ATTN_W:]],
                             axis=1).astype(BF16)
    row = lambda v: v.astype(F32).reshape(1, -1)
    return dict(
        norm_mix=row(norm_mix[0]), w_in=w_in_p, diff_lambda=diff_lambda[0].astype(F32),
        subln=row(subln[0]), w_attn_proj=w_attn_proj[0].astype(BF16),
        ssm=_ssm_operators(lam_re[0], lam_im[0], log_dt[0], b_re[0], b_im[0], c_re[0], c_im[0],
                           d_skip[0]),
        w_glu_a=w_glu_a[0].astype(BF16), w_glu_b=w_glu_b[0].astype(BF16),
        w_mix_out=w_mix_out[0].astype(BF16),
        norm_cross=row(norm_cross[0]), norm_mem=row(norm_mem[0]),
        w_q_cross=w_q_cross[0].astype(BF16), w_kv_cross=w_kv_cross[0].astype(BF16),
        w_o_cross=w_o_cross[0].astype(BF16),
        norm_mlp=row(norm_mlp[0]), w_mlp_up=w_mlp_up[0].astype(BF16),
        w_mlp_down=w_mlp_down[0].astype(BF16), norm_final=row(norm_final),
    )


def _encode(x, mem, p):
    b, seq, _ = x.shape
    assert seq % (8 * CHUNK) == 0
    n = b * seq
    nc = seq // CHUNK
    x2 = x.reshape(n, D_MODEL)
    cos_t, sin_t = _rope_tables(seq)

    q, kt, v, u, gates = _in_proj(x2, p["norm_mix"], p["w_in"], cos_t, sin_t, seq)
    att = _diff_attn(q.reshape(b, seq, ATTN_W), kt, v.reshape(b, seq, ATTN_W),
                     p["diff_lambda"], p["subln"])

    z = _ssm(u.reshape(b * nc, CHUNK, SSM_W), b, *p["ssm"]).reshape(n, SSM_W)

    x2 = _merge(x2, att.reshape(n, ATTN_W), z, gates, p["w_attn_proj"], p["w_glu_a"], p["w_glu_b"],
                p["w_mix_out"])

    m_tok = mem.shape[1]
    kv = _norm_mm(mem.reshape(b * m_tok, D_MODEL), p["norm_mem"], p["w_kv_cross"], 1.0)
    x2 = _cross(x2.reshape(b, seq, D_MODEL), p["norm_cross"], p["w_q_cross"],
                kv.reshape(b, m_tok, 2 * D_MODEL), p["w_o_cross"]).reshape(n, D_MODEL)

    out = _mlp(x2, p["norm_mlp"], p["w_mlp_up"], p["w_mlp_down"], p["norm_final"])
    return out.reshape(b, seq, D_MODEL)


def kernel(x_prompt, x_sample, mem_prompt, mem_sample, norm_mix, w_in, diff_lambda, subln, w_attn_proj, ssm_lambda_re, ssm_lambda_im, ssm_log_dt, ssm_b_re, ssm_b_im, ssm_c_re, ssm_c_im, ssm_d, w_glu_a, w_glu_b, w_mix_out, norm_cross, norm_mem, w_q_cross, w_kv_cross, w_o_cross, norm_mlp, w_mlp_up, w_mlp_down, norm_final):
    p = _prepare((norm_mix, w_in, diff_lambda, subln, w_attn_proj,
                  ssm_lambda_re, ssm_lambda_im, ssm_log_dt, ssm_b_re, ssm_b_im, ssm_c_re, ssm_c_im,
                  ssm_d, w_glu_a, w_glu_b, w_mix_out,
                  norm_cross, norm_mem, w_q_cross, w_kv_cross, w_o_cross,
                  norm_mlp, w_mlp_up, w_mlp_down, norm_final))
    return (_encode(x_prompt, mem_prompt, p), _encode(x_sample, mem_sample, p))
```

```python
import functools
import math

import jax
import jax.numpy as jnp
from jax import lax
from jax.experimental import pallas as pl
from jax.experimental.pallas import tpu as pltpu

D_MODEL = 2048
N_HEADS = 8
HEAD_DIM = 64
ATTN_W = N_HEADS * 2 * HEAD_DIM
SSM_W = D_MODEL // 2
GROUP_SIZE = 16
N_GROUPS = SSM_W // GROUP_SIZE
STATE = 64
IN_COLS = 3 * ATTN_W + SSM_W + 2 * D_MODEL
D_FF = 4 * D_MODEL
X_HEADS = 4
X_HEAD_DIM = D_MODEL // X_HEADS
ROPE_THETA = 10000.0
EPS = 1e-6
LAM_INIT = 0.8 - 0.6 * math.exp(-0.3 * 0)
Q_SCALE = HEAD_DIM ** -0.5 * math.log2(math.e)

CHUNK = 16
GROUP_BLOCK = 8
LANES = 128
VMEM_LIMIT = 56 * 1024 * 1024

F32 = jnp.float32
BF16 = jnp.bfloat16
HI = lax.Precision.HIGHEST


def _params(*sem):
    return pltpu.CompilerParams(dimension_semantics=sem, vmem_limit_bytes=VMEM_LIMIT)


def _rms(x, g):
    return x * lax.rsqrt(jnp.mean(x * x, axis=-1, keepdims=True) + EPS) * g


def _in_proj_kernel(x_ref, g_ref, w_ref, cos_ref, sin_ref,
                    q_ref, kt_ref, v_ref, u_ref, gate_ref, xn_ref):
    j = pl.program_id(1)

    @pl.when(j == 0)
    def _():
        xn_ref[...] = _rms(x_ref[...], g_ref[...]).astype(BF16)

    acc = jnp.dot(xn_ref[...], w_ref[...], preferred_element_type=F32)

    def rope(scale):
        cos = cos_ref[...]
        sin = sin_ref[...]
        half = HEAD_DIM // 2
        lane = lax.broadcasted_iota(jnp.int32, cos.shape, 1)
        low = (lane % HEAD_DIM) < half
        outs = []
        for c in range(acc.shape[1] // LANES):
            xc = acc[:, c * LANES:(c + 1) * LANES]
            other = jnp.where(low, pltpu.roll(xc, LANES - half, axis=1), pltpu.roll(xc, half, axis=1))
            outs.append((xc * cos + other * sin) * scale)
        return jnp.concatenate(outs, axis=1)

    @pl.when(j == 0)
    def _():
        q_ref[...] = rope(Q_SCALE).astype(BF16)

    @pl.when(j == 1)
    def _():
        kt_ref[...] = rope(1.0).T.astype(BF16)

    @pl.when(j == 2)
    def _():
        v_ref[...] = acc.astype(BF16)

    @pl.when(j == 3)
    def _():
        u_ref[...] = acc

    @pl.when(j >= 4)
    def _():
        gate_ref[...] = acc.astype(BF16)


def _in_proj(x2, g, w, cos_t, sin_t, seq):
    n = x2.shape[0]
    tm = min(512, seq)
    tn = ATTN_W
    nj = IN_COLS // tn
    pos_blocks = seq // tm
    row = lambda i, j: (i, 0)
    out_sd = lambda cols: jax.ShapeDtypeStruct((n, cols), BF16)
    return pl.pallas_call(
        _in_proj_kernel,
        grid=(n // tm, nj),
        in_specs=[
            pl.BlockSpec((tm, D_MODEL), row),
            pl.BlockSpec((1, D_MODEL), lambda i, j: (0, 0)),
            pl.BlockSpec((D_MODEL, tn), lambda i, j: (0, j)),
            pl.BlockSpec((tm, LANES), lambda i, j: (i % pos_blocks, 0)),
            pl.BlockSpec((tm, LANES), lambda i, j: (i % pos_blocks, 0)),
        ],
        out_specs=[
            pl.BlockSpec((tm, tn), row),
            pl.BlockSpec((tn, tm), lambda i, j: (0, i)),
            pl.BlockSpec((tm, tn), row),
            pl.BlockSpec((tm, tn), row),
            pl.BlockSpec((tm, tn), lambda i, j: (i, jnp.maximum(j - 4, 0))),
        ],
        out_shape=[out_sd(ATTN_W), jax.ShapeDtypeStruct((ATTN_W, n), BF16), out_sd(ATTN_W),
                   jax.ShapeDtypeStruct((n, SSM_W), F32), out_sd(2 * D_MODEL)],
        scratch_shapes=[pltpu.VMEM((tm, D_MODEL), BF16)],
        compiler_params=_params("parallel", "arbitrary"),
        name="in_proj",
    )(x2, g, w, cos_t, sin_t)


def _diff_attn_kernel(dl_ref, sub_ref, q_ref, kt_ref, v_ref, o_ref, s_ref, mx_ref, m_ref, acc_ref,
                      *, tq, tk, unroll):
    seq = v_ref.shape[1]
    nk = seq // tk
    steps = (seq // tq) * nk
    hw = 2 * HEAD_DIM
    lane = lax.broadcasted_iota(jnp.int32, (tq, hw), 1)
    first = lane < HEAD_DIM
    ones = jnp.ones((tk, hw), BF16)
    dl = dl_ref[...]
    lam = (jnp.exp(jnp.sum(dl[0:1] * dl[1:2], axis=-1, keepdims=True))
           - jnp.exp(jnp.sum(dl[2:3] * dl[3:4], axis=-1, keepdims=True)) + LAM_INIT)

    def scores(f, slot):
        f = jnp.minimum(f, steps - 1)
        q = q_ref[0, pl.ds(pl.multiple_of((f // nk) * tq, tq), tq), :]
        kc = kt_ref[:, pl.ds(pl.multiple_of((f % nk) * tk, tk), tk)]
        zero = jnp.zeros_like(q)
        for comp, qm in enumerate((jnp.where(first, q, zero), jnp.where(first, zero, q))):
            s = jnp.dot(qm, kc, preferred_element_type=F32)
            s_ref[slot, comp] = s
            mx_ref[slot, comp] = jnp.max(s, axis=-1, keepdims=True)

    def accumulate(f, slot, emit):
        c = f % nk
        vx = jnp.concatenate([v_ref[0, pl.ds(pl.multiple_of(c * tk, tk), tk), :], ones], axis=1)
        acc = []
        for comp in range(2):
            m_old = jnp.where(c == 0, -jnp.inf, m_ref[comp])
            m_new = jnp.maximum(m_old, mx_ref[slot, comp])
            m_ref[comp] = m_new
            p = jnp.exp2(s_ref[slot, comp] - m_new).astype(BF16)
            acc.append(jnp.exp2(m_old - m_new) * acc_ref[comp]
                       + jnp.dot(p, vx, preferred_element_type=F32))
            acc_ref[comp] = acc[comp]
        if emit:
            o = acc[0][:, :hw] / acc[0][:, hw:] - lam * (acc[1][:, :hw] / acc[1][:, hw:])
            rows = pl.ds(pl.multiple_of((f // nk) * tq, tq), tq)
            o_ref[0, rows, :] = (_rms(o, sub_ref[...]) * (1.0 - LAM_INIT)).astype(BF16)

    m_ref[...] = jnp.full(m_ref.shape, -jnp.inf, F32)
    acc_ref[...] = jnp.zeros(acc_ref.shape, F32)
    scores(0, 0)

    def body(i, _):
        for k in range(unroll):
            f = i * unroll + k
            scores(f + 1, (k + 1) % 2)
            accumulate(f, k % 2, emit=(k + 1) % min(unroll, nk) == 0)
        return 0

    lax.fori_loop(0, steps // unroll, body, 0)


def _diff_attn(q3, kt, v3, dl, sub):
    b, seq, _ = q3.shape
    tq = min(256, seq)
    tk = min(4096, seq // 2)
    nk = seq // tk
    unroll = 4 if ((seq // tq) * nk) % 4 == 0 else 2
    assert nk % unroll == 0 or unroll % nk == 0
    hw = 2 * HEAD_DIM
    blk = pl.BlockSpec((1, seq, hw), lambda b_, h: (b_, 0, h))
    return pl.pallas_call(
        functools.partial(_diff_attn_kernel, tq=tq, tk=tk, unroll=unroll),
        grid=(b, N_HEADS),
        in_specs=[
            pl.BlockSpec((4, HEAD_DIM), lambda b_, h: (0, 0)),
            pl.BlockSpec((1, hw), lambda b_, h: (0, 0)),
            blk,
            pl.BlockSpec((hw, seq), lambda b_, h: (h, b_)),
            blk,
        ],
        out_specs=blk,
        out_shape=jax.ShapeDtypeStruct((b, seq, ATTN_W), BF16),
        scratch_shapes=[pltpu.VMEM((2, 2, tq, tk), F32), pltpu.VMEM((2, 2, tq, 1), F32),
                        pltpu.VMEM((2, tq, 1), F32), pltpu.VMEM((2, tq, 2 * hw), F32)],
        compiler_params=_params("parallel", "arbitrary"),
        name="diff_attn",
    )(dl, sub, q3, kt, v3)


def _slot_swap_matrix():
    src = jnp.arange(8 * LANES).reshape(8, 8, GROUP_SIZE).transpose(1, 0, 2).reshape(-1)
    return (src[None, :] == jnp.arange(8 * LANES)[:, None]).astype(BF16)


def _ssm_kernel(u_ref, perm_ref, wi_ref, wb_ref, wc_ref, ap_ref, a8_ref, pt_ref, z_ref,
                ut_ref, s_ref, zt_ref, e_ref, hc_ref):
    nc = u_ref.shape[0]
    n_tiles = nc // 8
    gw = CHUNK * GROUP_SIZE
    pw = 2 * gw

    for th in range(CHUNK // 8):
        by_token = jnp.concatenate([u_ref[:, th * 8 + tl, :] for tl in range(8)], axis=1)
        by_group = jnp.dot(by_token.astype(BF16), perm_ref[...], preferred_element_type=F32)
        for g in range(GROUP_BLOCK):
            ut_ref[:, g * gw + th * LANES:g * gw + (th + 1) * LANES] = (
                by_group[:, g * LANES:(g + 1) * LANES].astype(BF16))

    for q in range(GROUP_BLOCK // 2):
        cols = slice(q * pw, (q + 1) * pw)
        s_ref[:, cols] = jnp.dot(ut_ref[:, cols], wb_ref[0, q], preferred_element_type=F32)

    row = lax.broadcasted_iota(jnp.int32, (8, LANES), 0)

    def shift_down(x, k):
        return jnp.where(row >= k, pltpu.roll(x, k, axis=0), 0.0)

    def shift_up(x, k):
        return jnp.where(row < 8 - k, pltpu.roll(x, 8 - k, axis=0), 0.0)

    n_chain = 2 * (GROUP_BLOCK // 2)

    def chain_cols(q, d):
        base = q * pw + d * 2 * LANES
        return slice(base, base + LANES), slice(base + LANES, base + 2 * LANES)

    def local_scan(i, _):
        rs = pl.ds(pl.multiple_of(i * 8, 8), 8)
        for q in range(GROUP_BLOCK // 2):
            for d in range(2):
                c_re, c_im = chain_cols(q, d)
                shift = shift_down if d == 0 else shift_up
                ch = q * 2 + d
                x_re, x_im = s_ref[rs, c_re], s_ref[rs, c_im]
                for ki, k in enumerate((1, 2, 4)):
                    a_re, a_im = ap_ref[0, (2 * ch) * 3 + ki], ap_ref[0, (2 * ch + 1) * 3 + ki]
                    y_re, y_im = shift(x_re, k), shift(x_im, k)
                    x_re, x_im = (x_re + a_re * y_re - a_im * y_im,
                                  x_im + a_re * y_im + a_im * y_re)
                s_ref[rs, c_re] = shift(x_re, 1)
                s_ref[rs, c_im] = shift(x_im, 1)
                pos = i if d == 0 else n_tiles - 1 - i
                edge = slice(7, 8) if d == 0 else slice(0, 1)
                e_ref[pos, ch:ch + 1, :] = x_re[edge]
                e_ref[pos, n_chain + ch:n_chain + ch + 1, :] = x_im[edge]
        return 0

    lax.fori_loop(0, n_tiles, local_scan, 0)

    a8_re, a8_im = a8_ref[0, :n_chain], a8_ref[0, n_chain:]

    def carry_in(i, h):
        h_re, h_im = h
        hc_ref[i, :n_chain, :] = h_re
        hc_ref[i, n_chain:, :] = h_im
        e_re, e_im = e_ref[i, :n_chain, :], e_ref[i, n_chain:, :]
        return a8_re * h_re - a8_im * h_im + e_re, a8_re * h_im + a8_im * h_re + e_im

    zero = jnp.zeros((n_chain, LANES), F32)
    lax.fori_loop(0, n_tiles, carry_in, (zero, zero))

    def add_carry(i, _):
        rs = pl.ds(pl.multiple_of(i * 8, 8), 8)
        for q in range(GROUP_BLOCK // 2):
            for d in range(2):
                c_re, c_im = chain_cols(q, d)
                ch = q * 2 + d
                pos = i if d == 0 else n_tiles - 1 - i
                h_re = jnp.broadcast_to(hc_ref[pos, ch:ch + 1, :], (8, LANES))
                h_im = jnp.broadcast_to(hc_ref[pos, n_chain + ch:n_chain + ch + 1, :], (8, LANES))
                p_re, p_im = pt_ref[0, 2 * ch], pt_ref[0, 2 * ch + 1]
                s_ref[rs, c_re] += p_re * h_re - p_im * h_im
                s_ref[rs, c_im] += p_re * h_im + p_im * h_re
        return 0

    lax.fori_loop(0, n_tiles, add_carry, 0)

    for q in range(GROUP_BLOCK // 2):
        cols = slice(q * pw, (q + 1) * pw)
        carried = jnp.dot(s_ref[:, cols].astype(BF16), wc_ref[0, q], preferred_element_type=F32)
        for g2 in range(2):
            g = 2 * q + g2
            gc = slice(g * gw, (g + 1) * gw)
            y = carried[:, g2 * gw:(g2 + 1) * gw] + jnp.dot(
                ut_ref[:, gc], wi_ref[0, g], preferred_element_type=F32)
            z = 0.5 * y * (1.0 + jnp.tanh(0.7978845608028654 * (y + 0.044715 * (y * y * y))))
            ut_ref[:, gc] = z.astype(BF16)

    for th in range(CHUNK // 8):
        by_group = jnp.concatenate(
            [ut_ref[:, g * gw + th * LANES:g * gw + (th + 1) * LANES] for g in range(GROUP_BLOCK)], axis=1)
        by_token = jnp.dot(by_group, perm_ref[...], preferred_element_type=F32)
        for tl in range(8):
            zt_ref[:, th * 8 + tl, :] = by_token[:, tl * LANES:(tl + 1) * LANES]
    z_ref[...] = zt_ref[...].astype(BF16)


def _ssm(u3, b, wi, wb, wc, ap, a8, pt):
    nc = u3.shape[0] // b
    cw = GROUP_BLOCK * CHUNK * GROUP_SIZE
    blk = pl.BlockSpec((nc, CHUNK, LANES), lambda g, b_: (b_, 0, g))
    whole = lambda a: pl.BlockSpec((1,) + a.shape[1:], lambda g, b_: (g,) + (0,) * (a.ndim - 1))
    carry = pltpu.VMEM((nc // 8, 2 * GROUP_BLOCK, LANES), F32)
    perm = _slot_swap_matrix()
    return pl.pallas_call(
        _ssm_kernel,
        grid=(N_GROUPS // GROUP_BLOCK, b),
        in_specs=[blk, pl.BlockSpec(perm.shape, lambda g, b_: (0, 0)),
                  whole(wi), whole(wb), whole(wc), whole(ap), whole(a8), whole(pt)],
        out_specs=blk,
        out_shape=jax.ShapeDtypeStruct(u3.shape, BF16),
        scratch_shapes=[pltpu.VMEM((nc, cw), BF16), pltpu.VMEM((nc, cw), F32),
                        pltpu.VMEM((nc, CHUNK, LANES), F32), carry, carry],
        compiler_params=_params("parallel", "arbitrary"),
        name="s5_scan",
    )(u3, perm, wi, wb, wc, ap, a8, pt)


def _ssm_operators(lam_re, lam_im, log_dt, b_re, b_im, c_re, c_im, d_skip):
    t_ = CHUNK
    g_, p_, c_ = N_GROUPS, STATE, GROUP_SIZE
    nb, npair = g_ // GROUP_BLOCK, g_ // 2
    lr, li = lam_re.astype(F32), lam_im.astype(F32)
    dt = jnp.exp(log_dt.astype(F32))[..., None]
    mag = jnp.exp(lr * dt)
    ab_re, ab_im = mag * jnp.cos(li * dt), mag * jnp.sin(li * dt)
    den = lr * lr + li * li
    n_re, n_im = ab_re - 1.0, ab_im
    k_re = (n_re * lr + n_im * li) / den
    k_im = (n_im * lr - n_re * li) / den
    br, bi = b_re.astype(F32), b_im.astype(F32)
    bb_re = k_re[..., None] * br - k_im[..., None] * bi
    bb_im = k_re[..., None] * bi + k_im[..., None] * br
    cr, ci = c_re.astype(F32), c_im.astype(F32)

    x_re, x_im = lr * dt, li * dt
    gw = t_ * c_

    def cpow(e, xr, xi):
        m = jnp.exp(e * xr)
        return m * jnp.cos(e * xi), m * jnp.sin(e * xi)

    swap = lambda a: a.transpose(0, 1, 3, 2)
    tok = jnp.repeat(jnp.arange(t_, dtype=F32), c_)
    first = (jnp.arange(2) == 0)[None, None, :, None, None]

    pw_r, pw_i = cpow(tok, x_re[..., None], x_im[..., None])
    crt, cit = jnp.tile(swap(cr), (1, 1, 1, t_)), jnp.tile(swap(ci), (1, 1, 1, t_))
    lhs = jnp.concatenate([crt * pw_r - cit * pw_i, -(crt * pw_i + cit * pw_r)], axis=2)
    rhs = jnp.concatenate([swap(bb_re), swap(bb_im)], axis=3)
    kern = jnp.einsum('dgck,dgkn->dgcn', rhs, lhs, precision=HI)
    fwd, bwd = kern[0], kern[1]
    bwd_rev = bwd.reshape(g_, c_, t_, c_)[:, :, ::-1].reshape(g_, c_, gw)
    lagged = jnp.concatenate([bwd_rev[..., :gw - c_], fwd[..., :c_] + bwd[..., :c_], fwd[..., c_:]], axis=-1)
    w_intra = jnp.stack([lagged[..., (t_ - 1 - ti) * c_:(t_ - 1 - ti) * c_ + gw] for ti in range(t_)], axis=1)
    skip = jnp.eye(gw, dtype=F32) * jnp.tile(d_skip.astype(F32).reshape(g_, c_), (1, t_))[:, :, None]
    w_intra = (w_intra.reshape(g_, gw, gw) + skip).reshape(nb, GROUP_BLOCK, gw, gw)

    def pair_rows(bb):
        x = jnp.tile(swap(bb), (1, 1, t_, 1)).reshape(2, npair, 2, gw, p_)
        return jnp.concatenate([jnp.where(first, x, 0.0), jnp.where(first, 0.0, x)], axis=-1).reshape(
            2, npair, 2 * gw, 2 * p_)

    tok2 = jnp.tile(tok, 2)
    e_rows = jnp.stack([t_ - 1 - tok2, tok2])[:, None, :, None]
    pr, pi = cpow(e_rows, x_re.reshape(2, npair, 1, 2 * p_), x_im.reshape(2, npair, 1, 2 * p_))
    bt_re, bt_im = pair_rows(bb_re), pair_rows(bb_im)
    wb_re, wb_im = pr * bt_re - pi * bt_im, pr * bt_im + pi * bt_re
    wb = jnp.concatenate([wb_re[0], wb_im[0], wb_re[1], wb_im[1]], axis=-1)
    wb = wb.reshape(nb, GROUP_BLOCK // 2, 2 * gw, 8 * p_)

    def pair_cols(cc):
        x = jnp.tile(swap(cc), (1, 1, 1, t_)).reshape(2, npair, 2, p_, gw)
        return jnp.concatenate([jnp.where(first, x, 0.0), jnp.where(first, 0.0, x)], axis=-1).reshape(
            2, npair, 2 * p_, 2 * gw)

    e_cols = jnp.stack([tok2 + 1, t_ - tok2])[:, None, None, :]
    qr, qi = cpow(e_cols, x_re.reshape(2, npair, 2 * p_, 1), x_im.reshape(2, npair, 2 * p_, 1))
    ct_re, ct_im = pair_cols(cr), pair_cols(ci)
    wc_re, wc_im = ct_re * qr - ct_im * qi, -(ct_re * qi + ct_im * qr)
    wc = jnp.concatenate([wc_re[0], wc_im[0], wc_re[1], wc_im[1]], axis=1)
    wc = wc.reshape(nb, GROUP_BLOCK // 2, 8 * p_, 2 * gw)

    def table(e):
        re, im = cpow(e, x_re, x_im)
        tab = jnp.stack([re, im], axis=2)
        k = tab.shape[0]
        tab = tab.reshape(k, 2, 2, nb, GROUP_BLOCK // 2, 2 * p_).transpose(3, 4, 1, 2, 0, 5)
        return tab.reshape(nb, (GROUP_BLOCK // 2) * 4, k, 2 * p_)

    step = jnp.asarray([1.0, 2.0, 4.0], F32) * t_
    ap = table(jnp.broadcast_to(step[:, None, None, None], (3, 2, 1, 1)))
    ap = jnp.broadcast_to(ap[:, :, :, None, :], ap.shape[:3] + (8, 2 * p_)).reshape(nb, -1, 8, 2 * p_)
    a8 = table(jnp.full((1, 2, 1, 1), 8.0 * t_, F32))
    a8 = a8.reshape(nb, GROUP_BLOCK // 2, 2, 2, 2 * p_).transpose(0, 3, 1, 2, 4).reshape(nb, -1, 2 * p_)
    j8 = jnp.arange(8, dtype=F32) * t_
    pt = table(jnp.stack([j8, j8[::-1]], axis=1)[:, :, None, None])
    return w_intra.astype(BF16), wb.astype(BF16), wc.astype(BF16), ap, a8, pt


def _resident(a):
    return pl.BlockSpec(a.shape, lambda *_: (0,) * a.ndim, pipeline_mode=pl.Buffered(1))


def _merge_kernel(x_ref, att_ref, z_ref, gate_ref, wp_ref, wa_ref, wb_ref, wo_ref, o_ref, mix_ref,
                  *, tn):
    att = att_ref[...]
    z = z_ref[...]
    for j in range(D_MODEL // tn):
        cs = slice(j * tn, (j + 1) * tn)
        gs = slice(D_MODEL + j * tn, D_MODEL + (j + 1) * tn)
        y_attn = jnp.dot(att, wp_ref[:, cs], preferred_element_type=F32)
        y_ssm = (jnp.dot(z, wa_ref[:, cs], preferred_element_type=F32)
                 * jax.nn.sigmoid(jnp.dot(z, wb_ref[:, cs], preferred_element_type=F32)))
        g_a = jax.nn.sigmoid(gate_ref[:, cs].astype(F32))
        g_s = jax.nn.sigmoid(gate_ref[:, gs].astype(F32))
        mix_ref[:, cs] = (g_a * y_attn + g_s * y_ssm).astype(BF16)
    o_ref[...] = x_ref[...] + jnp.dot(mix_ref[...], wo_ref[...], preferred_element_type=F32)


def _merge(x2, att, z, gates, wp, wa, wb, wo):
    n = att.shape[0]
    tm = min(256, n)
    row = lambda cols: pl.BlockSpec((tm, cols), lambda i: (i, 0))
    return pl.pallas_call(
        functools.partial(_merge_kernel, tn=512),
        grid=(n // tm,),
        in_specs=[row(D_MODEL), row(ATTN_W), row(SSM_W), row(2 * D_MODEL),
                  _resident(wp), _resident(wa), _resident(wb), _resident(wo)],
        out_specs=row(D_MODEL),
        out_shape=jax.ShapeDtypeStruct((n, D_MODEL), F32),
        scratch_shapes=[pltpu.VMEM((tm, D_MODEL), BF16)],
        compiler_params=_params("parallel"),
        name="merge_mix_out",
    )(x2, att, z, gates, wp, wa, wb, wo)


def _norm_mm_kernel(x_ref, g_ref, w_ref, o_ref, xn_ref, *, scale):
    @pl.when(pl.program_id(1) == 0)
    def _():
        xn_ref[...] = _rms(x_ref[...], g_ref[...]).astype(BF16)

    acc = jnp.dot(xn_ref[...], w_ref[...], preferred_element_type=F32)
    o_ref[...] = (acc * scale).astype(BF16)


def _norm_mm(x, g, w, scale):
    n, kdim = x.shape
    cols = w.shape[1]
    tm = min(512, n)
    tn = 1024
    return pl.pallas_call(
        functools.partial(_norm_mm_kernel, scale=scale),
        grid=(n // tm, cols // tn),
        in_specs=[
            pl.BlockSpec((tm, kdim), lambda i, j: (i, 0)),
            pl.BlockSpec((1, kdim), lambda i, j: (0, 0)),
            pl.BlockSpec((kdim, tn), lambda i, j: (0, j)),
        ],
        out_specs=pl.BlockSpec((tm, tn), lambda i, j: (i, j)),
        out_shape=jax.ShapeDtypeStruct((n, cols), BF16),
        scratch_shapes=[pltpu.VMEM((tm, kdim), BF16)],
        compiler_params=_params("parallel", "arbitrary"),
        name="norm_matmul",
    )(x, g, w)


def _cross_kernel(x_ref, g_ref, wq_ref, kv_ref, wo_ref, o_ref, q_ref, oc_ref):
    x = x_ref[0]
    hn = _rms(x, g_ref[...]).astype(BF16)
    q_ref[...] = (jnp.dot(hn, wq_ref[...], preferred_element_type=F32)
                  * X_HEAD_DIM ** -0.5).astype(BF16)
    nt = (((1,), (1,)), ((), ()))
    for h in range(X_HEADS):
        cols = slice(h * X_HEAD_DIM, (h + 1) * X_HEAD_DIM)
        vcols = slice(D_MODEL + h * X_HEAD_DIM, D_MODEL + (h + 1) * X_HEAD_DIM)
        s = lax.dot_general(q_ref[:, cols], kv_ref[0, :, cols], nt, preferred_element_type=F32)
        p = jnp.exp(s - jnp.max(s, axis=-1, keepdims=True))
        l = jnp.sum(p, axis=-1, keepdims=True)
        o = jnp.dot(p.astype(BF16), kv_ref[0, :, vcols], preferred_element_type=F32)
        oc_ref[:, cols] = (o / l).astype(BF16)
    o_ref[0] = x + jnp.dot(oc_ref[...], wo_ref[...], preferred_element_type=F32)


def _cross(x3, g, wq, kv3, wo):
    b, seq, _ = x3.shape
    m = kv3.shape[1]
    tq = min(512, seq)
    blk = pl.BlockSpec((1, tq, D_MODEL), lambda b_, i: (b_, i, 0))
    return pl.pallas_call(
        _cross_kernel,
        grid=(b, seq // tq),
        in_specs=[blk, _resident(g), _resident(wq),
                  pl.BlockSpec((1, m, 2 * D_MODEL), lambda b_, i: (b_, 0, 0)), _resident(wo)],
        out_specs=blk,
        out_shape=jax.ShapeDtypeStruct(x3.shape, F32),
        scratch_shapes=[pltpu.VMEM((tq, D_MODEL), BF16), pltpu.VMEM((tq, D_MODEL), BF16)],
        compiler_params=_params("parallel", "arbitrary"),
        name="cross_attn_block",
    )(x3, g, wq, kv3, wo)


def _mlp_kernel(x_ref, g_ref, wu_ref, wd_ref, gf_ref, o_ref, hn_ref, acc_ref):
    f = pl.program_id(1)

    @pl.when(f == 0)
    def _():
        hn_ref[...] = _rms(x_ref[...], g_ref[...]).astype(BF16)
        acc_ref[...] = jnp.zeros_like(acc_ref)

    h = jnp.maximum(jnp.dot(hn_ref[...], wu_ref[...], preferred_element_type=F32), 0.0)
    acc_ref[...] += jnp.dot((h * h).astype(BF16), wd_ref[...], preferred_element_type=F32)

    @pl.when(f == pl.num_programs(1) - 1)
    def _():
        o_ref[...] = _rms(x_ref[...] + acc_ref[...], gf_ref[...])


def _mlp(x, g, wu, wd, gf):
    n = x.shape[0]
    tm = min(512, n)
    tf = 512
    return pl.pallas_call(
        _mlp_kernel,
        grid=(n // tm, D_FF // tf),
        in_specs=[
            pl.BlockSpec((tm, D_MODEL), lambda i, f: (i, 0)),
            pl.BlockSpec((1, D_MODEL), lambda i, f: (0, 0)),
            pl.BlockSpec((D_MODEL, tf), lambda i, f: (0, f)),
            pl.BlockSpec((tf, D_MODEL), lambda i, f: (f, 0)),
            pl.BlockSpec((1, D_MODEL), lambda i, f: (0, 0)),
        ],
        out_specs=pl.BlockSpec((tm, D_MODEL), lambda i, f: (i, 0)),
        out_shape=jax.ShapeDtypeStruct((n, D_MODEL), F32),
        scratch_shapes=[pltpu.VMEM((tm, D_MODEL), BF16), pltpu.VMEM((tm, D_MODEL), F32)],
        compiler_params=_params("parallel", "arbitrary"),
        name="mlp_final_norm",
    )(x, g, wu, wd, gf)


def _rope_tables(seq):
    half = HEAD_DIM // 2
    inv = ROPE_THETA ** (-jnp.arange(0, HEAD_DIM, 2, dtype=F32) / HEAD_DIM)
    ang = jnp.arange(seq, dtype=F32)[:, None] * inv[None, :]
    cos, sin = jnp.cos(ang), jnp.sin(ang)
    cos_t = jnp.tile(cos, (1, LANES // half))
    sin_t = jnp.concatenate([-sin, sin, -sin, sin], axis=1)
    return cos_t, sin_t


def _prepare(w):
    (norm_mix, w_in, diff_lambda, subln, w_attn_proj,
     lam_re, lam_im, log_dt, b_re, b_im, c_re, c_im, d_skip,
     w_glu_a, w_glu_b, w_mix_out,
     norm_cross, norm_mem, w_q_cross, w_kv_cross, w_o_cross,
     norm_mlp, w_mlp_up, w_mlp_down, norm_final) = w
    w_in_p = w_in[0].astype(BF16)
    row = lambda v: v.astype(F32).reshape(1, -1)
    return dict(
        norm_mix=row(norm_mix[0]), w_in=w_in_p, diff_lambda=diff_lambda[0].astype(F32),
        subln=row(subln[0]), w_attn_proj=w_attn_proj[0].astype(BF16),
        ssm=_ssm_operators(lam_re[0], lam_im[0], log_dt[0], b_re[0], b_im[0], c_re[0], c_im[0],
                           d_skip[0]),
        w_glu_a=w_glu_a[0].astype(BF16), w_glu_b=w_glu_b[0].astype(BF16),
        w_mix_out=w_mix_out[0].astype(BF16),
        norm_cross=row(norm_cross[0]), norm_mem=row(norm_mem[0]),
        w_q_cross=w_q_cross[0].astype(BF16), w_kv_cross=w_kv_cross[0].astype(BF16),
        w_o_cross=w_o_cross[0].astype(BF16),
        norm_mlp=row(norm_mlp[0]), w_mlp_up=w_mlp_up[0].astype(BF16),
        w_mlp_down=w_mlp_down[0].astype(BF16), norm_final=row(norm_final),
    )


def _encode(x, mem, p):
    b, seq, _ = x.shape
    assert seq % (8 * CHUNK) == 0
    n = b * seq
    nc = seq // CHUNK
    x2 = x.reshape(n, D_MODEL)
    cos_t, sin_t = _rope_tables(seq)

    q, kt, v, u, gates = _in_proj(x2, p["norm_mix"], p["w_in"], cos_t, sin_t, seq)
    att = _diff_attn(q.reshape(b, seq, ATTN_W), kt, v.reshape(b, seq, ATTN_W),
                     p["diff_lambda"], p["subln"])

    z = _ssm(u.reshape(b * nc, CHUNK, SSM_W), b, *p["ssm"]).reshape(n, SSM_W)

    x2 = _merge(x2, att.reshape(n, ATTN_W), z, gates, p["w_attn_proj"], p["w_glu_a"], p["w_glu_b"],
                p["w_mix_out"])

    m_tok = mem.shape[1]
    kv = _norm_mm(mem.reshape(b * m_tok, D_MODEL), p["norm_mem"], p["w_kv_cross"], 1.0)
    x2 = _cross(x2.reshape(b, seq, D_MODEL), p["norm_cross"], p["w_q_cross"],
                kv.reshape(b, m_tok, 2 * D_MODEL), p["w_o_cross"]).reshape(n, D_MODEL)

    out = _mlp(x2, p["norm_mlp"], p["w_mlp_up"], p["w_mlp_down"], p["norm_final"])
    return out.reshape(b, seq, D_MODEL)


def kernel(x_prompt, x_sample, mem_prompt, mem_sample, norm_mix, w_in, diff_lambda, subln, w_attn_proj, ssm_lambda_re, ssm_lambda_im, ssm_log_dt, ssm_b_re, ssm_b_im, ssm_c_re, ssm_c_im, ssm_d, w_glu_a, w_glu_b, w_mix_out, norm_cross, norm_mem, w_q_cross, w_kv_cross, w_o_cross, norm_mlp, w_mlp_up, w_mlp_down, norm_final):
    p = _prepare((norm_mix, w_in, diff_lambda, subln, w_attn_proj,
                  ssm_lambda_re, ssm_lambda_im, ssm_log_dt, ssm_b_re, ssm_b_im, ssm_c_re, ssm_c_im,
                  ssm_d, w_glu_a, w_glu_b, w_mix_out,
                  norm_cross, norm_mem, w_q_cross, w_kv_cross, w_o_cross,
                  norm_mlp, w_mlp_up, w_mlp_down, norm_final))
    return (_encode(x_prompt, mem_prompt, p), _encode(x_sample, mem_sample, p))
```

```python
import functools
import math

import jax
import jax.numpy as jnp
from jax import lax
from jax.experimental import pallas as pl
from jax.experimental.pallas import tpu as pltpu

D_MODEL = 2048
N_HEADS = 8
HEAD_DIM = 64
ATTN_W = N_HEADS * 2 * HEAD_DIM
SSM_W = D_MODEL // 2
GROUP_SIZE = 16
N_GROUPS = SSM_W // GROUP_SIZE
STATE = 64
IN_COLS = 3 * ATTN_W + SSM_W + 2 * D_MODEL
D_FF = 4 * D_MODEL
X_HEADS = 4
X_HEAD_DIM = D_MODEL // X_HEADS
ROPE_THETA = 10000.0
EPS = 1e-6
LAM_INIT = 0.8 - 0.6 * math.exp(-0.3 * 0)
Q_SCALE = HEAD_DIM ** -0.5 * math.log2(math.e)

CHUNK = 16
GROUP_BLOCK = 8
LANES = 128
VMEM_LIMIT = 56 * 1024 * 1024

F32 = jnp.float32
BF16 = jnp.bfloat16
HI = lax.Precision.HIGHEST


def _params(*sem):
    return pltpu.CompilerParams(dimension_semantics=sem, vmem_limit_bytes=VMEM_LIMIT)


def _rms(x, g):
    return x * lax.rsqrt(jnp.mean(x * x, axis=-1, keepdims=True) + EPS) * g


def _in_proj_kernel(x_ref, g_ref, w_ref, cos_ref, sin_ref,
                    q_ref, kt_ref, v_ref, u_ref, gate_ref, xn_ref):
    j = pl.program_id(1)

    @pl.when(j == 0)
    def _():
        xn_ref[...] = _rms(x_ref[...], g_ref[...]).astype(BF16)

    acc = jnp.dot(xn_ref[...], w_ref[...], preferred_element_type=F32)

    def rope(scale):
        cos = cos_ref[...]
        sin = sin_ref[...]
        outs = []
        for c in range(acc.shape[1] // LANES):
            xc = acc[:, c * LANES:(c + 1) * LANES]
            outs.append((xc * cos + pltpu.roll(xc, LANES // 2, axis=1) * sin) * scale)
        return jnp.concatenate(outs, axis=1)

    @pl.when(j == 0)
    def _():
        q_ref[...] = rope(Q_SCALE).astype(BF16)

    @pl.when(j == 1)
    def _():
        kt_ref[...] = rope(1.0).T.astype(BF16)

    @pl.when(j == 2)
    def _():
        v_ref[...] = acc.astype(BF16)

    @pl.when(j == 3)
    def _():
        u_ref[...] = acc

    @pl.when(j >= 4)
    def _():
        gate_ref[...] = acc.astype(BF16)


def _in_proj(x2, g, w, cos_t, sin_t, seq):
    n = x2.shape[0]
    tm = min(512, seq)
    tn = ATTN_W
    nj = IN_COLS // tn
    pos_blocks = seq // tm
    row = lambda i, j: (i, 0)
    out_sd = lambda cols: jax.ShapeDtypeStruct((n, cols), BF16)
    return pl.pallas_call(
        _in_proj_kernel,
        grid=(n // tm, nj),
        in_specs=[
            pl.BlockSpec((tm, D_MODEL), row),
            pl.BlockSpec((1, D_MODEL), lambda i, j: (0, 0)),
            pl.BlockSpec((D_MODEL, tn), lambda i, j: (0, j)),
            pl.BlockSpec((tm, LANES), lambda i, j: (i % pos_blocks, 0)),
            pl.BlockSpec((tm, LANES), lambda i, j: (i % pos_blocks, 0)),
        ],
        out_specs=[
            pl.BlockSpec((tm, tn), row),
            pl.BlockSpec((tn, tm), lambda i, j: (0, i)),
            pl.BlockSpec((tm, tn), row),
            pl.BlockSpec((tm, tn), row),
            pl.BlockSpec((tm, tn), lambda i, j: (i, jnp.maximum(j - 4, 0))),
        ],
        out_shape=[out_sd(ATTN_W), jax.ShapeDtypeStruct((ATTN_W, n), BF16), out_sd(ATTN_W),
                   jax.ShapeDtypeStruct((n, SSM_W), F32), out_sd(2 * D_MODEL)],
        scratch_shapes=[pltpu.VMEM((tm, D_MODEL), BF16)],
        compiler_params=_params("parallel", "arbitrary"),
        name="in_proj",
    )(x2, g, w, cos_t, sin_t)


def _diff_attn_kernel(dl_ref, sub_ref, q_ref, kt_ref, v_ref, o_ref, s_ref, mx_ref, m_ref, acc_ref,
                      *, tq, tk, unroll):
    seq = v_ref.shape[1]
    nk = seq // tk
    steps = (seq // tq) * nk
    hw = 2 * HEAD_DIM
    lane = lax.broadcasted_iota(jnp.int32, (tq, hw), 1)
    first = ((lane // (HEAD_DIM // 2)) % 2) == 0
    ones = jnp.ones((tk, hw), BF16)
    dl = dl_ref[...]
    lam = (jnp.exp(jnp.sum(dl[0:1] * dl[1:2], axis=-1, keepdims=True))
           - jnp.exp(jnp.sum(dl[2:3] * dl[3:4], axis=-1, keepdims=True)) + LAM_INIT)

    def scores(f, slot):
        f = jnp.minimum(f, steps - 1)
        q = q_ref[0, pl.ds(pl.multiple_of((f // nk) * tq, tq), tq), :]
        kc = kt_ref[:, pl.ds(pl.multiple_of((f % nk) * tk, tk), tk)]
        zero = jnp.zeros_like(q)
        for comp, qm in enumerate((jnp.where(first, q, zero), jnp.where(first, zero, q))):
            s = jnp.dot(qm, kc, preferred_element_type=F32)
            s_ref[slot, comp] = s
            mx_ref[slot, comp] = jnp.max(s, axis=-1, keepdims=True)

    def accumulate(f, slot, emit):
        c = f % nk
        vx = jnp.concatenate([v_ref[0, pl.ds(pl.multiple_of(c * tk, tk), tk), :], ones], axis=1)
        acc = []
        for comp in range(2):
            m_old = jnp.where(c == 0, -jnp.inf, m_ref[comp])
            m_new = jnp.maximum(m_old, mx_ref[slot, comp])
            m_ref[comp] = m_new
            p = jnp.exp2(s_ref[slot, comp] - m_new).astype(BF16)
            acc.append(jnp.exp2(m_old - m_new) * acc_ref[comp]
                       + jnp.dot(p, vx, preferred_element_type=F32))
            acc_ref[comp] = acc[comp]
        if emit:
            o = acc[0][:, :hw] / acc[0][:, hw:] - lam * (acc[1][:, :hw] / acc[1][:, hw:])
            rows = pl.ds(pl.multiple_of((f // nk) * tq, tq), tq)
            o_ref[0, rows, :] = (_rms(o, sub_ref[...]) * (1.0 - LAM_INIT)).astype(BF16)

    m_ref[...] = jnp.full(m_ref.shape, -jnp.inf, F32)
    acc_ref[...] = jnp.zeros(acc_ref.shape, F32)
    scores(0, 0)

    def body(i, _):
        for k in range(unroll):
            f = i * unroll + k
            scores(f + 1, (k + 1) % 2)
            accumulate(f, k % 2, emit=(k + 1) % min(unroll, nk) == 0)
        return 0

    lax.fori_loop(0, steps // unroll, body, 0)


def _diff_attn(q3, kt, v3, dl, sub):
    b, seq, _ = q3.shape
    tq = min(256, seq)
    tk = min(4096, seq // 2)
    nk = seq // tk
    unroll = 4 if ((seq // tq) * nk) % 4 == 0 else 2
    assert nk % unroll == 0 or unroll % nk == 0
    hw = 2 * HEAD_DIM
    blk = pl.BlockSpec((1, seq, hw), lambda b_, h: (b_, 0, h))
    return pl.pallas_call(
        functools.partial(_diff_attn_kernel, tq=tq, tk=tk, unroll=unroll),
        grid=(b, N_HEADS),
        in_specs=[
            pl.BlockSpec((4, HEAD_DIM), lambda b_, h: (0, 0)),
            pl.BlockSpec((1, hw), lambda b_, h: (0, 0)),
            blk,
            pl.BlockSpec((hw, seq), lambda b_, h: (h, b_)),
            blk,
        ],
        out_specs=blk,
        out_shape=jax.ShapeDtypeStruct((b, seq, ATTN_W), BF16),
        scratch_shapes=[pltpu.VMEM((2, 2, tq, tk), F32), pltpu.VMEM((2, 2, tq, 1), F32),
                        pltpu.VMEM((2, tq, 1), F32), pltpu.VMEM((2, tq, 2 * hw), F32)],
        compiler_params=_params("parallel", "arbitrary"),
        name="diff_attn",
    )(dl, sub, q3, kt, v3)


def _slot_swap_matrix():
    src = jnp.arange(8 * LANES).reshape(8, 8, GROUP_SIZE).transpose(1, 0, 2).reshape(-1)
    return (src[None, :] == jnp.arange(8 * LANES)[:, None]).astype(BF16)


def _ssm_kernel(u_ref, perm_ref, wi_ref, wb_ref, wc_ref, ap_ref, a8_ref, pt_ref, z_ref,
                ut_ref, s_ref, zt_ref, e_ref, hc_ref):
    nc = u_ref.shape[0]
    n_tiles = nc // 8
    gw = CHUNK * GROUP_SIZE
    pw = 2 * gw

    for th in range(CHUNK // 8):
        by_token = jnp.concatenate([u_ref[:, th * 8 + tl, :] for tl in range(8)], axis=1)
        by_group = jnp.dot(by_token.astype(BF16), perm_ref[...], preferred_element_type=F32)
        for g in range(GROUP_BLOCK):
            ut_ref[:, g * gw + th * LANES:g * gw + (th + 1) * LANES] = (
                by_group[:, g * LANES:(g + 1) * LANES].astype(BF16))

    for q in range(GROUP_BLOCK // 2):
        cols = slice(q * pw, (q + 1) * pw)
        s_ref[:, cols] = jnp.dot(ut_ref[:, cols], wb_ref[0, q], preferred_element_type=F32)

    row = lax.broadcasted_iota(jnp.int32, (8, LANES), 0)

    def shift_down(x, k):
        return jnp.where(row >= k, pltpu.roll(x, k, axis=0), 0.0)

    def shift_up(x, k):
        return jnp.where(row < 8 - k, pltpu.roll(x, 8 - k, axis=0), 0.0)

    n_chain = 2 * (GROUP_BLOCK // 2)

    def chain_cols(q, d):
        base = q * pw + d * 2 * LANES
        return slice(base, base + LANES), slice(base + LANES, base + 2 * LANES)

    def local_scan(i, _):
        rs = pl.ds(pl.multiple_of(i * 8, 8), 8)
        for q in range(GROUP_BLOCK // 2):
            for d in range(2):
                c_re, c_im = chain_cols(q, d)
                shift = shift_down if d == 0 else shift_up
                ch = q * 2 + d
                x_re, x_im = s_ref[rs, c_re], s_ref[rs, c_im]
                for ki, k in enumerate((1, 2, 4)):
                    a_re, a_im = ap_ref[0, (2 * ch) * 3 + ki], ap_ref[0, (2 * ch + 1) * 3 + ki]
                    y_re, y_im = shift(x_re, k), shift(x_im, k)
                    x_re, x_im = (x_re + a_re * y_re - a_im * y_im,
                                  x_im + a_re * y_im + a_im * y_re)
                s_ref[rs, c_re] = shift(x_re, 1)
                s_ref[rs, c_im] = shift(x_im, 1)
                pos = i if d == 0 else n_tiles - 1 - i
                edge = slice(7, 8) if d == 0 else slice(0, 1)
                e_ref[pos, ch:ch + 1, :] = x_re[edge]
                e_ref[pos, n_chain + ch:n_chain + ch + 1, :] = x_im[edge]
        return 0

    lax.fori_loop(0, n_tiles, local_scan, 0)

    a8_re, a8_im = a8_ref[0, :n_chain], a8_ref[0, n_chain:]

    def carry_in(i, h):
        h_re, h_im = h
        hc_ref[i, :n_chain, :] = h_re
        hc_ref[i, n_chain:, :] = h_im
        e_re, e_im = e_ref[i, :n_chain, :], e_ref[i, n_chain:, :]
        return a8_re * h_re - a8_im * h_im + e_re, a8_re * h_im + a8_im * h_re + e_im

    zero = jnp.zeros((n_chain, LANES), F32)
    lax.fori_loop(0, n_tiles, carry_in, (zero, zero))

    def add_carry(i, _):
        rs = pl.ds(pl.multiple_of(i * 8, 8), 8)
        for q in range(GROUP_BLOCK // 2):
            for d in range(2):
                c_re, c_im = chain_cols(q, d)
                ch = q * 2 + d
                pos = i if d == 0 else n_tiles - 1 - i
                h_re = jnp.broadcast_to(hc_ref[pos, ch:ch + 1, :], (8, LANES))
                h_im = jnp.broadcast_to(hc_ref[pos, n_chain + ch:n_chain + ch + 1, :], (8, LANES))
                p_re, p_im = pt_ref[0, 2 * ch], pt_ref[0, 2 * ch + 1]
                s_ref[rs, c_re] += p_re * h_re - p_im * h_im
                s_ref[rs, c_im] += p_re * h_im + p_im * h_re
        return 0

    lax.fori_loop(0, n_tiles, add_carry, 0)

    for q in range(GROUP_BLOCK // 2):
        cols = slice(q * pw, (q + 1) * pw)
        carried = jnp.dot(s_ref[:, cols].astype(BF16), wc_ref[0, q], preferred_element_type=F32)
        for g2 in range(2):
            g = 2 * q + g2
            gc = slice(g * gw, (g + 1) * gw)
            y = carried[:, g2 * gw:(g2 + 1) * gw] + jnp.dot(
                ut_ref[:, gc], wi_ref[0, g], preferred_element_type=F32)
            z = 0.5 * y * (1.0 + jnp.tanh(0.7978845608028654 * (y + 0.044715 * (y * y * y))))
            ut_ref[:, gc] = z.astype(BF16)

    for th in range(CHUNK // 8):
        by_group = jnp.concatenate(
            [ut_ref[:, g * gw + th * LANES:g * gw + (th + 1) * LANES] for g in range(GROUP_BLOCK)], axis=1)
        by_token = jnp.dot(by_group, perm_ref[...], preferred_element_type=F32)
        for tl in range(8):
            zt_ref[:, th * 8 + tl, :] = by_token[:, tl * LANES:(tl + 1) * LANES]
    z_ref[...] = zt_ref[...].astype(BF16)


def _ssm(u3, b, wi, wb, wc, ap, a8, pt):
    nc = u3.shape[0] // b
    cw = GROUP_BLOCK * CHUNK * GROUP_SIZE
    blk = pl.BlockSpec((nc, CHUNK, LANES), lambda g, b_: (b_, 0, g))
    whole = lambda a: pl.BlockSpec((1,) + a.shape[1:], lambda g, b_: (g,) + (0,) * (a.ndim - 1))
    carry = pltpu.VMEM((nc // 8, 2 * GROUP_BLOCK, LANES), F32)
    perm = _slot_swap_matrix()
    return pl.pallas_call(
        _ssm_kernel,
        grid=(N_GROUPS // GROUP_BLOCK, b),
        in_specs=[blk, pl.BlockSpec(perm.shape, lambda g, b_: (0, 0)),
                  whole(wi), whole(wb), whole(wc), whole(ap), whole(a8), whole(pt)],
        out_specs=blk,
        out_shape=jax.ShapeDtypeStruct(u3.shape, BF16),
        scratch_shapes=[pltpu.VMEM((nc, cw), BF16), pltpu.VMEM((nc, cw), F32),
                        pltpu.VMEM((nc, CHUNK, LANES), F32), carry, carry],
        compiler_params=_params("parallel", "arbitrary"),
        name="s5_scan",
    )(u3, perm, wi, wb, wc, ap, a8, pt)


def _ssm_operators(lam_re, lam_im, log_dt, b_re, b_im, c_re, c_im, d_skip):
    t_ = CHUNK
    g_, p_, c_ = N_GROUPS, STATE, GROUP_SIZE
    nb, npair = g_ // GROUP_BLOCK, g_ // 2
    lr, li = lam_re.astype(F32), lam_im.astype(F32)
    dt = jnp.exp(log_dt.astype(F32))[..., None]
    mag = jnp.exp(lr * dt)
    ab_re, ab_im = mag * jnp.cos(li * dt), mag * jnp.sin(li * dt)
    den = lr * lr + li * li
    n_re, n_im = ab_re - 1.0, ab_im
    k_re = (n_re * lr + n_im * li) / den
    k_im = (n_im * lr - n_re * li) / den
    br, bi = b_re.astype(F32), b_im.astype(F32)
    bb_re = k_re[..., None] * br - k_im[..., None] * bi
    bb_im = k_re[..., None] * bi + k_im[..., None] * br
    cr, ci = c_re.astype(F32), c_im.astype(F32)

    x_re, x_im = lr * dt, li * dt
    gw = t_ * c_

    def cpow(e, xr, xi):
        m = jnp.exp(e * xr)
        return m * jnp.cos(e * xi), m * jnp.sin(e * xi)

    swap = lambda a: a.transpose(0, 1, 3, 2)
    tok = jnp.repeat(jnp.arange(t_, dtype=F32), c_)
    first = (jnp.arange(2) == 0)[None, None, :, None, None]

    pw_r, pw_i = cpow(tok, x_re[..., None], x_im[..., None])
    crt, cit = jnp.tile(swap(cr), (1, 1, 1, t_)), jnp.tile(swap(ci), (1, 1, 1, t_))
    lhs = jnp.concatenate([crt * pw_r - cit * pw_i, -(crt * pw_i + cit * pw_r)], axis=2)
    rhs = jnp.concatenate([swap(bb_re), swap(bb_im)], axis=3)
    kern = jnp.einsum('dgck,dgkn->dgcn', rhs, lhs, precision=HI)
    fwd, bwd = kern[0], kern[1]
    bwd_rev = bwd.reshape(g_, c_, t_, c_)[:, :, ::-1].reshape(g_, c_, gw)
    lagged = jnp.concatenate([bwd_rev[..., :gw - c_], fwd[..., :c_] + bwd[..., :c_], fwd[..., c_:]], axis=-1)
    w_intra = jnp.stack([lagged[..., (t_ - 1 - ti) * c_:(t_ - 1 - ti) * c_ + gw] for ti in range(t_)], axis=1)
    skip = jnp.eye(gw, dtype=F32) * jnp.tile(d_skip.astype(F32).reshape(g_, c_), (1, t_))[:, :, None]
    w_intra = (w_intra.reshape(g_, gw, gw) + skip).reshape(nb, GROUP_BLOCK, gw, gw)

    def pair_rows(bb):
        x = jnp.tile(swap(bb), (1, 1, t_, 1)).reshape(2, npair, 2, gw, p_)
        return jnp.concatenate([jnp.where(first, x, 0.0), jnp.where(first, 0.0, x)], axis=-1).reshape(
            2, npair, 2 * gw, 2 * p_)

    tok2 = jnp.tile(tok, 2)
    e_rows = jnp.stack([t_ - 1 - tok2, tok2])[:, None, :, None]
    pr, pi = cpow(e_rows, x_re.reshape(2, npair, 1, 2 * p_), x_im.reshape(2, npair, 1, 2 * p_))
    bt_re, bt_im = pair_rows(bb_re), pair_rows(bb_im)
    wb_re, wb_im = pr * bt_re - pi * bt_im, pr * bt_im + pi * bt_re
    wb = jnp.concatenate([wb_re[0], wb_im[0], wb_re[1], wb_im[1]], axis=-1)
    wb = wb.reshape(nb, GROUP_BLOCK // 2, 2 * gw, 8 * p_)

    def pair_cols(cc):
        x = jnp.tile(swap(cc), (1, 1, 1, t_)).reshape(2, npair, 2, p_, gw)
        return jnp.concatenate([jnp.where(first, x, 0.0), jnp.where(first, 0.0, x)], axis=-1).reshape(
            2, npair, 2 * p_, 2 * gw)

    e_cols = jnp.stack([tok2 + 1, t_ - tok2])[:, None, None, :]
    qr, qi = cpow(e_cols, x_re.reshape(2, npair, 2 * p_, 1), x_im.reshape(2, npair, 2 * p_, 1))
    ct_re, ct_im = pair_cols(cr), pair_cols(ci)
    wc_re, wc_im = ct_re * qr - ct_im * qi, -(ct_re * qi + ct_im * qr)
    wc = jnp.concatenate([wc_re[0], wc_im[0], wc_re[1], wc_im[1]], axis=1)
    wc = wc.reshape(nb, GROUP_BLOCK // 2, 8 * p_, 2 * gw)

    def table(e):
        re, im = cpow(e, x_re, x_im)
        tab = jnp.stack([re, im], axis=2)
        k = tab.shape[0]
        tab = tab.reshape(k, 2, 2, nb, GROUP_BLOCK // 2, 2 * p_).transpose(3, 4, 1, 2, 0, 5)
        return tab.reshape(nb, (GROUP_BLOCK // 2) * 4, k, 2 * p_)

    step = jnp.asarray([1.0, 2.0, 4.0], F32) * t_
    ap = table(jnp.broadcast_to(step[:, None, None, None], (3, 2, 1, 1)))
    ap = jnp.broadcast_to(ap[:, :, :, None, :], ap.shape[:3] + (8, 2 * p_)).reshape(nb, -1, 8, 2 * p_)
    a8 = table(jnp.full((1, 2, 1, 1), 8.0 * t_, F32))
    a8 = a8.reshape(nb, GROUP_BLOCK // 2, 2, 2, 2 * p_).transpose(0, 3, 1, 2, 4).reshape(nb, -1, 2 * p_)
    j8 = jnp.arange(8, dtype=F32) * t_
    pt = table(jnp.stack([j8, j8[::-1]], axis=1)[:, :, None, None])
    return w_intra.astype(BF16), wb.astype(BF16), wc.astype(BF16), ap, a8, pt


def _resident(a):
    return pl.BlockSpec(a.shape, lambda *_: (0,) * a.ndim, pipeline_mode=pl.Buffered(1))


def _merge_kernel(x_ref, att_ref, z_ref, gate_ref, wp_ref, wa_ref, wb_ref, wo_ref, o_ref, mix_ref,
                  *, tn):
    att = att_ref[...]
    z = z_ref[...]
    for j in range(D_MODEL // tn):
        cs = slice(j * tn, (j + 1) * tn)
        gs = slice(D_MODEL + j * tn, D_MODEL + (j + 1) * tn)
        y_attn = jnp.dot(att, wp_ref[:, cs], preferred_element_type=F32)
        y_ssm = (jnp.dot(z, wa_ref[:, cs], preferred_element_type=F32)
                 * jax.nn.sigmoid(jnp.dot(z, wb_ref[:, cs], preferred_element_type=F32)))
        g_a = jax.nn.sigmoid(gate_ref[:, cs].astype(F32))
        g_s = jax.nn.sigmoid(gate_ref[:, gs].astype(F32))
        mix_ref[:, cs] = (g_a * y_attn + g_s * y_ssm).astype(BF16)
    o_ref[...] = x_ref[...] + jnp.dot(mix_ref[...], wo_ref[...], preferred_element_type=F32)


def _merge(x2, att, z, gates, wp, wa, wb, wo):
    n = att.shape[0]
    tm = min(256, n)
    row = lambda cols: pl.BlockSpec((tm, cols), lambda i: (i, 0))
    return pl.pallas_call(
        functools.partial(_merge_kernel, tn=512),
        grid=(n // tm,),
        in_specs=[row(D_MODEL), row(ATTN_W), row(SSM_W), row(2 * D_MODEL),
                  _resident(wp), _resident(wa), _resident(wb), _resident(wo)],
        out_specs=row(D_MODEL),
        out_shape=jax.ShapeDtypeStruct((n, D_MODEL), F32),
        scratch_shapes=[pltpu.VMEM((tm, D_MODEL), BF16)],
        compiler_params=_params("parallel"),
        name="merge_mix_out",
    )(x2, att, z, gates, wp, wa, wb, wo)


def _norm_mm_kernel(x_ref, g_ref, w_ref, o_ref, xn_ref, *, scale):
    @pl.when(pl.program_id(1) == 0)
    def _():
        xn_ref[...] = _rms(x_ref[...], g_ref[...]).astype(BF16)

    acc = jnp.dot(xn_ref[...], w_ref[...], preferred_element_type=F32)
    o_ref[...] = (acc * scale).astype(BF16)


def _norm_mm(x, g, w, scale):
    n, kdim = x.shape
    cols = w.shape[1]
    tm = min(512, n)
    tn = 1024
    return pl.pallas_call(
        functools.partial(_norm_mm_kernel, scale=scale),
        grid=(n // tm, cols // tn),
        in_specs=[
            pl.BlockSpec((tm, kdim), lambda i, j: (i, 0)),
            pl.BlockSpec((1, kdim), lambda i, j: (0, 0)),
            pl.BlockSpec((kdim, tn), lambda i, j: (0, j)),
        ],
        out_specs=pl.BlockSpec((tm, tn), lambda i, j: (i, j)),
        out_shape=jax.ShapeDtypeStruct((n, cols), BF16),
        scratch_shapes=[pltpu.VMEM((tm, kdim), BF16)],
        compiler_params=_params("parallel", "arbitrary"),
        name="norm_matmul",
    )(x, g, w)


def _cross_kernel(x_ref, g_ref, wq_ref, kv_ref, wo_ref, o_ref, q_ref, oc_ref):
    x = x_ref[0]
    hn = _rms(x, g_ref[...]).astype(BF16)
    q_ref[...] = (jnp.dot(hn, wq_ref[...], preferred_element_type=F32)
                  * X_HEAD_DIM ** -0.5).astype(BF16)
    nt = (((1,), (1,)), ((), ()))
    for h in range(X_HEADS):
        cols = slice(h * X_HEAD_DIM, (h + 1) * X_HEAD_DIM)
        vcols = slice(D_MODEL + h * X_HEAD_DIM, D_MODEL + (h + 1) * X_HEAD_DIM)
        s = lax.dot_general(q_ref[:, cols], kv_ref[0, :, cols], nt, preferred_element_type=F32)
        p = jnp.exp(s - jnp.max(s, axis=-1, keepdims=True))
        l = jnp.sum(p, axis=-1, keepdims=True)
        o = jnp.dot(p.astype(BF16), kv_ref[0, :, vcols], preferred_element_type=F32)
        oc_ref[:, cols] = (o / l).astype(BF16)
    o_ref[0] = x + jnp.dot(oc_ref[...], wo_ref[...], preferred_element_type=F32)


def _cross(x3, g, wq, kv3, wo):
    b, seq, _ = x3.shape
    m = kv3.shape[1]
    tq = min(512, seq)
    blk = pl.BlockSpec((1, tq, D_MODEL), lambda b_, i: (b_, i, 0))
    return pl.pallas_call(
        _cross_kernel,
        grid=(b, seq // tq),
        in_specs=[blk, _resident(g), _resident(wq),
                  pl.BlockSpec((1, m, 2 * D_MODEL), lambda b_, i: (b_, 0, 0)), _resident(wo)],
        out_specs=blk,
        out_shape=jax.ShapeDtypeStruct(x3.shape, F32),
        scratch_shapes=[pltpu.VMEM((tq, D_MODEL), BF16), pltpu.VMEM((tq, D_MODEL), BF16)],
        compiler_params=_params("parallel", "arbitrary"),
        name="cross_attn_block",
    )(x3, g, wq, kv3, wo)


def _mlp_kernel(x_ref, g_ref, wu_ref, wd_ref, gf_ref, o_ref, hn_ref, acc_ref):
    f = pl.program_id(1)

    @pl.when(f == 0)
    def _():
        hn_ref[...] = _rms(x_ref[...], g_ref[...]).astype(BF16)
        acc_ref[...] = jnp.zeros_like(acc_ref)

    h = jnp.maximum(jnp.dot(hn_ref[...], wu_ref[...], preferred_element_type=F32), 0.0)
    acc_ref[...] += jnp.dot((h * h).astype(BF16), wd_ref[...], preferred_element_type=F32)

    @pl.when(f == pl.num_programs(1) - 1)
    def _():
        o_ref[...] = _rms(x_ref[...] + acc_ref[...], gf_ref[...])


def _mlp(x, g, wu, wd, gf):
    n = x.shape[0]
    tm = min(512, n)
    tf = 1024
    return pl.pallas_call(
        _mlp_kernel,
        grid=(n // tm, D_FF // tf),
        in_specs=[
            pl.BlockSpec((tm, D_MODEL), lambda i, f: (i, 0)),
            pl.BlockSpec((1, D_MODEL), lambda i, f: (0, 0)),
            pl.BlockSpec((D_MODEL, tf), lambda i, f: (0, f)),
            pl.BlockSpec((tf, D_MODEL), lambda i, f: (f, 0)),
            pl.BlockSpec((1, D_MODEL), lambda i, f: (0, 0)),
        ],
        out_specs=pl.BlockSpec((tm, D_MODEL), lambda i, f: (i, 0)),
        out_shape=jax.ShapeDtypeStruct((n, D_MODEL), F32),
        scratch_shapes=[pltpu.VMEM((tm, D_MODEL), BF16), pltpu.VMEM((tm, D_MODEL), F32)],
        compiler_params=_params("parallel", "arbitrary"),
        name="mlp_final_norm",
    )(x, g, wu, wd, gf)


def _rope_tables(seq):
    half = HEAD_DIM // 2
    inv = ROPE_THETA ** (-jnp.arange(0, HEAD_DIM, 2, dtype=F32) / HEAD_DIM)
    ang = jnp.arange(seq, dtype=F32)[:, None] * inv[None, :]
    cos, sin = jnp.cos(ang), jnp.sin(ang)
    cos_t = jnp.tile(cos, (1, LANES // half))
    sin_t = jnp.concatenate([-sin, -sin, sin, sin], axis=1)
    return cos_t, sin_t


def _qk_shuffle():
    half = HEAD_DIM // 2
    src = jnp.arange(2 * ATTN_W).reshape(2 * N_HEADS, 2, 2, half).transpose(0, 2, 1, 3).reshape(-1)
    return (src[None, :] == jnp.arange(2 * ATTN_W)[:, None]).astype(BF16)


def _prepare(w):
    (norm_mix, w_in, diff_lambda, subln, w_attn_proj,
     lam_re, lam_im, log_dt, b_re, b_im, c_re, c_im, d_skip,
     w_glu_a, w_glu_b, w_mix_out,
     norm_cross, norm_mem, w_q_cross, w_kv_cross, w_o_cross,
     norm_mlp, w_mlp_up, w_mlp_down, norm_final) = w
    w_in0 = w_in[0].astype(BF16)
    w_qk = jnp.dot(w_in0[:, :2 * ATTN_W], _qk_shuffle(), preferred_element_type=BF16)
    w_in_p = jnp.concatenate([w_qk, w_in0[:, 2 * ATTN_W:]], axis=1)
    row = lambda v: v.astype(F32).reshape(1, -1)
    return dict(
        norm_mix=row(norm_mix[0]), w_in=w_in_p, diff_lambda=diff_lambda[0].astype(F32),
        subln=row(subln[0]), w_attn_proj=w_attn_proj[0].astype(BF16),
        ssm=_ssm_operators(lam_re[0], lam_im[0], log_dt[0], b_re[0], b_im[0], c_re[0], c_im[0],
                           d_skip[0]),
        w_glu_a=w_glu_a[0].astype(BF16), w_glu_b=w_glu_b[0].astype(BF16),
        w_mix_out=w_mix_out[0].astype(BF16),
        norm_cross=row(norm_cross[0]), norm_mem=row(norm_mem[0]),
        w_q_cross=w_q_cross[0].astype(BF16), w_kv_cross=w_kv_cross[0].astype(BF16),
        w_o_cross=w_o_cross[0].astype(BF16),
        norm_mlp=row(norm_mlp[0]), w_mlp_up=w_mlp_up[0].astype(BF16),
        w_mlp_down=w_mlp_down[0].astype(BF16), norm_final=row(norm_final),
    )


def _encode(x, mem, p):
    b, seq, _ = x.shape
    assert seq % (8 * CHUNK) == 0
    n = b * seq
    nc = seq // CHUNK
    x2 = x.reshape(n, D_MODEL)
    cos_t, sin_t = _rope_tables(seq)

    q, kt, v, u, gates = _in_proj(x2, p["norm_mix"], p["w_in"], cos_t, sin_t, seq)
    att = _diff_attn(q.reshape(b, seq, ATTN_W), kt, v.reshape(b, seq, ATTN_W),
                     p["diff_lambda"], p["subln"])

    z = _ssm(u.reshape(b * nc, CHUNK, SSM_W), b, *p["ssm"]).reshape(n, SSM_W)

    x2 = _merge(x2, att.reshape(n, ATTN_W), z, gates, p["w_attn_proj"], p["w_glu_a"], p["w_glu_b"],
                p["w_mix_out"])

    m_tok = mem.shape[1]
    kv = _norm_mm(mem.reshape(b * m_tok, D_MODEL), p["norm_mem"], p["w_kv_cross"], 1.0)
    x2 = _cross(x2.reshape(b, seq, D_MODEL), p["norm_cross"], p["w_q_cross"],
                kv.reshape(b, m_tok, 2 * D_MODEL), p["w_o_cross"]).reshape(n, D_MODEL)

    out = _mlp(x2, p["norm_mlp"], p["w_mlp_up"], p["w_mlp_down"], p["norm_final"])
    return out.reshape(b, seq, D_MODEL)


def kernel(x_prompt, x_sample, mem_prompt, mem_sample, norm_mix, w_in, diff_lambda, subln, w_attn_proj, ssm_lambda_re, ssm_lambda_im, ssm_log_dt, ssm_b_re, ssm_b_im, ssm_c_re, ssm_c_im, ssm_d, w_glu_a, w_glu_b, w_mix_out, norm_cross, norm_mem, w_q_cross, w_kv_cross, w_o_cross, norm_mlp, w_mlp_up, w_mlp_down, norm_final):
    p = _prepare((norm_mix, w_in, diff_lambda, subln, w_attn_proj,
                  ssm_lambda_re, ssm_lambda_im, ssm_log_dt, ssm_b_re, ssm_b_im, ssm_c_re, ssm_c_im,
                  ssm_d, w_glu_a, w_glu_b, w_mix_out,
                  norm_cross, norm_mem, w_q_cross, w_kv_cross, w_o_cross,
                  norm_mlp, w_mlp_up, w_mlp_down, norm_final))
    return (_encode(x_prompt, mem_prompt, p), _encode(x_sample, mem_sample, p))
```

```python
import functools
import math

import jax
import jax.numpy as jnp
from jax import lax
from jax.experimental import pallas as pl
from jax.experimental.pallas import tpu as pltpu

D_MODEL = 2048
N_HEADS = 8
HEAD_DIM = 64
ATTN_W = N_HEADS * 2 * HEAD_DIM
SSM_W = D_MODEL // 2
GROUP_SIZE = 16
N_GROUPS = SSM_W // GROUP_SIZE
STATE = 64
IN_COLS = 3 * ATTN_W + SSM_W + 2 * D_MODEL
D_FF = 4 * D_MODEL
X_HEADS = 4
X_HEAD_DIM = D_MODEL // X_HEADS
ROPE_THETA = 10000.0
EPS = 1e-6
LAM_INIT = 0.8 - 0.6 * math.exp(-0.3 * 0)
Q_SCALE = HEAD_DIM ** -0.5 * math.log2(math.e)

CHUNK = 16
GROUP_BLOCK = 8
LANES = 128
VMEM_LIMIT = 56 * 1024 * 1024

F32 = jnp.float32
BF16 = jnp.bfloat16
HI = lax.Precision.HIGHEST


def _params(*sem):
    return pltpu.CompilerParams(dimension_semantics=sem, vmem_limit_bytes=VMEM_LIMIT)


def _rms(x, g):
    return x * lax.rsqrt(jnp.mean(x * x, axis=-1, keepdims=True) + EPS) * g


def _in_proj_kernel(x_ref, g_ref, w_ref, cos_ref, sin_ref, o_ref, u_ref, xn_ref, acc_a, acc_b):
    j = pl.program_id(1)

    @pl.when(j == 0)
    def _():
        xn_ref[...] = _rms(x_ref[...], g_ref[...]).astype(BF16)

    def project(cur):
        cur[...] = jnp.dot(xn_ref[...], w_ref[...], preferred_element_type=F32)

    def finish(prev):
        cos = cos_ref[0]
        sin = sin_ref[0]
        for c in range(prev.shape[1] // LANES):
            xc = prev[:, c * LANES:(c + 1) * LANES]
            o_ref[:, c * LANES:(c + 1) * LANES] = (
                xc * cos + pltpu.roll(xc, LANES // 2, axis=1) * sin).astype(BF16)

    @pl.when(j == 0)
    def _():
        project(acc_a)

    @pl.when((j > 0) & (j % 2 == 0))
    def _():
        finish(acc_b)
        project(acc_a)

    @pl.when(j % 2 == 1)
    def _():
        finish(acc_a)
        project(acc_b)

    @pl.when(j == pl.num_programs(1) - 1)
    def _():
        u_ref[...] = acc_b[...]


def _in_proj(x2, g, w, cos_t, sin_t, seq):
    n = x2.shape[0]
    tm = min(512, seq)
    tn = ATTN_W
    nj = IN_COLS // tn
    assert nj % 2 == 0
    pos_blocks = seq // tm
    table = lambda i, j: (jnp.where(j == 5, 0, jnp.where(j == 6, 1, 2)), i % pos_blocks, 0)
    return pl.pallas_call(
        _in_proj_kernel,
        grid=(n // tm, nj),
        in_specs=[
            pl.BlockSpec((tm, D_MODEL), lambda i, j: (i, 0)),
            pl.BlockSpec((1, D_MODEL), lambda i, j: (0, 0)),
            pl.BlockSpec((D_MODEL, tn), lambda i, j: (0, (j + nj // 2) % nj)),
            pl.BlockSpec((1, tm, LANES), table),
            pl.BlockSpec((1, tm, LANES), table),
        ],
        out_specs=[
            pl.BlockSpec((tm, tn), lambda i, j: (i, jnp.maximum(j - 1, 0))),
            pl.BlockSpec((tm, tn), lambda i, j: (i, 0)),
        ],
        out_shape=[jax.ShapeDtypeStruct((n, (nj - 1) * tn), BF16),
                   jax.ShapeDtypeStruct((n, SSM_W), F32)],
        scratch_shapes=[pltpu.VMEM((tm, D_MODEL), BF16), pltpu.VMEM((tm, tn), F32),
                        pltpu.VMEM((tm, tn), F32)],
        compiler_params=_params("parallel", "arbitrary"),
        name="in_proj",
    )(x2, g, w, cos_t, sin_t)


def _diff_attn_kernel(dl_ref, sub_ref, q_ref, k_ref, v_ref, o_ref, kt_ref, s_ref, mx_ref, m_ref, acc_ref,
                      *, tq, tk, unroll):
    seq = v_ref.shape[1]
    nk = seq // tk
    steps = (seq // tq) * nk
    hw = 2 * HEAD_DIM
    lane = lax.broadcasted_iota(jnp.int32, (tq, hw), 1)
    first = ((lane // (HEAD_DIM // 2)) % 2) == 0
    ones = jnp.ones((tk, hw), BF16)
    dl = dl_ref[...]
    lam = (jnp.exp(jnp.sum(dl[0:1] * dl[1:2], axis=-1, keepdims=True))
           - jnp.exp(jnp.sum(dl[2:3] * dl[3:4], axis=-1, keepdims=True)) + LAM_INIT)

    def scores(f, slot):
        f = jnp.minimum(f, steps - 1)
        q = q_ref[0, pl.ds(pl.multiple_of((f // nk) * tq, tq), tq), :]
        kc = kt_ref[:, pl.ds(pl.multiple_of((f % nk) * tk, tk), tk)]
        zero = jnp.zeros_like(q)
        for comp, qm in enumerate((jnp.where(first, q, zero), jnp.where(first, zero, q))):
            s = jnp.dot(qm, kc, preferred_element_type=F32)
            s_ref[slot, comp] = s
            mx_ref[slot, comp] = jnp.max(s, axis=-1, keepdims=True)

    def accumulate(f, slot, emit):
        c = f % nk
        vx = jnp.concatenate([v_ref[0, pl.ds(pl.multiple_of(c * tk, tk), tk), :], ones], axis=1)
        acc = []
        for comp in range(2):
            m_old = jnp.where(c == 0, -jnp.inf, m_ref[comp])
            m_new = jnp.maximum(m_old, mx_ref[slot, comp])
            m_ref[comp] = m_new
            p = jnp.exp2(s_ref[slot, comp] - m_new).astype(BF16)
            acc.append(jnp.exp2(m_old - m_new) * acc_ref[comp]
                       + jnp.dot(p, vx, preferred_element_type=F32))
            acc_ref[comp] = acc[comp]
        if emit:
            o = acc[0][:, :hw] / acc[0][:, hw:] - lam * (acc[1][:, :hw] / acc[1][:, hw:])
            rows = pl.ds(pl.multiple_of((f // nk) * tq, tq), tq)
            o_ref[0, rows, :] = (_rms(o, sub_ref[...]) * (1.0 - LAM_INIT)).astype(BF16)

    for r in range(0, seq, tq):
        kt_ref[:, r:r + tq] = k_ref[0, r:r + tq, :].T

    m_ref[...] = jnp.full(m_ref.shape, -jnp.inf, F32)
    acc_ref[...] = jnp.zeros(acc_ref.shape, F32)
    scores(0, 0)

    def body(i, _):
        for k in range(unroll):
            f = i * unroll + k
            scores(f + 1, (k + 1) % 2)
            accumulate(f, k % 2, emit=(k + 1) % min(unroll, nk) == 0)
        return 0

    lax.fori_loop(0, steps // unroll, body, 0)


def _diff_attn(proj3, dl, sub):
    b, seq, _ = proj3.shape
    tq = min(256, seq)
    tk = min(4096, seq // 2)
    nk = seq // tk
    unroll = 4 if ((seq // tq) * nk) % 4 == 0 else 2
    assert nk % unroll == 0 or unroll % nk == 0
    hw = 2 * HEAD_DIM
    blk = pl.BlockSpec((1, seq, hw), lambda b_, h: (b_, 0, h))
    col = lambda first: pl.BlockSpec((1, seq, hw), lambda b_, h: (b_, 0, first // hw + h))
    return pl.pallas_call(
        functools.partial(_diff_attn_kernel, tq=tq, tk=tk, unroll=unroll),
        grid=(b, N_HEADS),
        in_specs=[
            pl.BlockSpec((4, HEAD_DIM), lambda b_, h: (0, 0)),
            pl.BlockSpec((1, hw), lambda b_, h: (0, 0)),
            col(2 * D_MODEL), col(2 * D_MODEL + ATTN_W), col(2 * D_MODEL + 2 * ATTN_W),
        ],
        out_specs=blk,
        out_shape=jax.ShapeDtypeStruct((b, seq, ATTN_W), BF16),
        scratch_shapes=[pltpu.VMEM((hw, seq), BF16),
                        pltpu.VMEM((2, 2, tq, tk), F32), pltpu.VMEM((2, 2, tq, 1), F32),
                        pltpu.VMEM((2, tq, 1), F32), pltpu.VMEM((2, tq, 2 * hw), F32)],
        compiler_params=_params("parallel", "arbitrary"),
        name="diff_attn",
    )(dl, sub, proj3, proj3, proj3)


def _slot_swap_matrix():
    src = jnp.arange(8 * LANES).reshape(8, 8, GROUP_SIZE).transpose(1, 0, 2).reshape(-1)
    return (src[None, :] == jnp.arange(8 * LANES)[:, None]).astype(BF16)


def _ssm_kernel(u_ref, perm_ref, wi_ref, wb_ref, wc_ref, ap_ref, a8_ref, pt_ref, z_ref,
                ut_ref, s_ref, zt_ref, e_ref, hc_ref):
    nc = u_ref.shape[0]
    n_tiles = nc // 8
    gw = CHUNK * GROUP_SIZE
    pw = 2 * gw

    for th in range(CHUNK // 8):
        by_token = jnp.concatenate([u_ref[:, th * 8 + tl, :] for tl in range(8)], axis=1)
        by_group = jnp.dot(by_token.astype(BF16), perm_ref[...], preferred_element_type=F32)
        for g in range(GROUP_BLOCK):
            ut_ref[:, g * gw + th * LANES:g * gw + (th + 1) * LANES] = (
                by_group[:, g * LANES:(g + 1) * LANES].astype(BF16))

    for q in range(GROUP_BLOCK // 2):
        cols = slice(q * pw, (q + 1) * pw)
        s_ref[:, cols] = jnp.dot(ut_ref[:, cols], wb_ref[0, q], preferred_element_type=F32)

    row = lax.broadcasted_iota(jnp.int32, (8, LANES), 0)

    def shift_down(x, k):
        return jnp.where(row >= k, pltpu.roll(x, k, axis=0), 0.0)

    def shift_up(x, k):
        return jnp.where(row < 8 - k, pltpu.roll(x, 8 - k, axis=0), 0.0)

    n_chain = 2 * (GROUP_BLOCK // 2)

    def chain_cols(q, d):
        base = q * pw + d * 2 * LANES
        return slice(base, base + LANES), slice(base + LANES, base + 2 * LANES)

    def local_scan(i, _):
        rs = pl.ds(pl.multiple_of(i * 8, 8), 8)
        for q in range(GROUP_BLOCK // 2):
            for d in range(2):
                c_re, c_im = chain_cols(q, d)
                shift = shift_down if d == 0 else shift_up
                ch = q * 2 + d
                x_re, x_im = s_ref[rs, c_re], s_ref[rs, c_im]
                for ki, k in enumerate((1, 2, 4)):
                    a_re, a_im = ap_ref[0, (2 * ch) * 3 + ki], ap_ref[0, (2 * ch + 1) * 3 + ki]
                    y_re, y_im = shift(x_re, k), shift(x_im, k)
                    x_re, x_im = (x_re + a_re * y_re - a_im * y_im,
                                  x_im + a_re * y_im + a_im * y_re)
                s_ref[rs, c_re] = shift(x_re, 1)
                s_ref[rs, c_im] = shift(x_im, 1)
                pos = i if d == 0 else n_tiles - 1 - i
                edge = slice(7, 8) if d == 0 else slice(0, 1)
                e_ref[pos, ch:ch + 1, :] = x_re[edge]
                e_ref[pos, n_chain + ch:n_chain + ch + 1, :] = x_im[edge]
        return 0

    lax.fori_loop(0, n_tiles, local_scan, 0)

    a8_re, a8_im = a8_ref[0, :n_chain], a8_ref[0, n_chain:]

    def carry_in(i, h):
        h_re, h_im = h
        hc_ref[i, :n_chain, :] = h_re
        hc_ref[i, n_chain:, :] = h_im
        e_re, e_im = e_ref[i, :n_chain, :], e_ref[i, n_chain:, :]
        return a8_re * h_re - a8_im * h_im + e_re, a8_re * h_im + a8_im * h_re + e_im

    zero = jnp.zeros((n_chain, LANES), F32)
    lax.fori_loop(0, n_tiles, carry_in, (zero, zero))

    def add_carry(i, _):
        rs = pl.ds(pl.multiple_of(i * 8, 8), 8)
        for q in range(GROUP_BLOCK // 2):
            for d in range(2):
                c_re, c_im = chain_cols(q, d)
                ch = q * 2 + d
                pos = i if d == 0 else n_tiles - 1 - i
                h_re = jnp.broadcast_to(hc_ref[pos, ch:ch + 1, :], (8, LANES))
                h_im = jnp.broadcast_to(hc_ref[pos, n_chain + ch:n_chain + ch + 1, :], (8, LANES))
                p_re, p_im = pt_ref[0, 2 * ch], pt_ref[0, 2 * ch + 1]
                s_ref[rs, c_re] += p_re * h_re - p_im * h_im
                s_ref[rs, c_im] += p_re * h_im + p_im * h_re
        return 0

    lax.fori_loop(0, n_tiles, add_carry, 0)

    for q in range(GROUP_BLOCK // 2):
        cols = slice(q * pw, (q + 1) * pw)
        carried = jnp.dot(s_ref[:, cols].astype(BF16), wc_ref[0, q], preferred_element_type=F32)
        for g2 in range(2):
            g = 2 * q + g2
            gc = slice(g * gw, (g + 1) * gw)
            y = carried[:, g2 * gw:(g2 + 1) * gw] + jnp.dot(
                ut_ref[:, gc], wi_ref[0, g], preferred_element_type=F32)
            z = 0.5 * y * (1.0 + jnp.tanh(0.7978845608028654 * (y + 0.044715 * (y * y * y))))
            ut_ref[:, gc] = z.astype(BF16)

    for th in range(CHUNK // 8):
        by_group = jnp.concatenate(
            [ut_ref[:, g * gw + th * LANES:g * gw + (th + 1) * LANES] for g in range(GROUP_BLOCK)], axis=1)
        by_token = jnp.dot(by_group, perm_ref[...], preferred_element_type=F32)
        for tl in range(8):
            zt_ref[:, th * 8 + tl, :] = by_token[:, tl * LANES:(tl + 1) * LANES]
    z_ref[...] = zt_ref[...].astype(BF16)


def _ssm(u3, b, wi, wb, wc, ap, a8, pt):
    nc = u3.shape[0] // b
    cw = GROUP_BLOCK * CHUNK * GROUP_SIZE
    blk = pl.BlockSpec((nc, CHUNK, LANES), lambda g, b_: (b_, 0, g))
    whole = lambda a: pl.BlockSpec((1,) + a.shape[1:], lambda g, b_: (g,) + (0,) * (a.ndim - 1))
    carry = pltpu.VMEM((nc // 8, 2 * GROUP_BLOCK, LANES), F32)
    perm = _slot_swap_matrix()
    return pl.pallas_call(
        _ssm_kernel,
        grid=(N_GROUPS // GROUP_BLOCK, b),
        in_specs=[blk, pl.BlockSpec(perm.shape, lambda g, b_: (0, 0)),
                  whole(wi), whole(wb), whole(wc), whole(ap), whole(a8), whole(pt)],
        out_specs=blk,
        out_shape=jax.ShapeDtypeStruct(u3.shape, BF16),
        scratch_shapes=[pltpu.VMEM((nc, cw), BF16), pltpu.VMEM((nc, cw), F32),
                        pltpu.VMEM((nc, CHUNK, LANES), F32), carry, carry],
        compiler_params=_params("parallel", "arbitrary"),
        name="s5_scan",
    )(u3, perm, wi, wb, wc, ap, a8, pt)


def _ssm_operators(lam_re, lam_im, log_dt, b_re, b_im, c_re, c_im, d_skip):
    t_ = CHUNK
    g_, p_, c_ = N_GROUPS, STATE, GROUP_SIZE
    nb, npair = g_ // GROUP_BLOCK, g_ // 2
    lr, li = lam_re.astype(F32), lam_im.astype(F32)
    dt = jnp.exp(log_dt.astype(F32))[..., None]
    mag = jnp.exp(lr * dt)
    ab_re, ab_im = mag * jnp.cos(li * dt), mag * jnp.sin(li * dt)
    den = lr * lr + li * li
    n_re, n_im = ab_re - 1.0, ab_im
    k_re = (n_re * lr + n_im * li) / den
    k_im = (n_im * lr - n_re * li) / den
    br, bi = b_re.astype(F32), b_im.astype(F32)
    bb_re = k_re[..., None] * br - k_im[..., None] * bi
    bb_im = k_re[..., None] * bi + k_im[..., None] * br
    cr, ci = c_re.astype(F32), c_im.astype(F32)

    x_re, x_im = lr * dt, li * dt
    gw = t_ * c_

    def cpow(e, xr, xi):
        m = jnp.exp(e * xr)
        return m * jnp.cos(e * xi), m * jnp.sin(e * xi)

    swap = lambda a: a.transpose(0, 1, 3, 2)
    tok = jnp.repeat(jnp.arange(t_, dtype=F32), c_)
    first = (jnp.arange(2) == 0)[None, None, :, None, None]

    pw_r, pw_i = cpow(tok, x_re[..., None], x_im[..., None])
    crt, cit = jnp.tile(swap(cr), (1, 1, 1, t_)), jnp.tile(swap(ci), (1, 1, 1, t_))
    lhs = jnp.concatenate([crt * pw_r - cit * pw_i, -(crt * pw_i + cit * pw_r)], axis=2)
    rhs = jnp.concatenate([swap(bb_re), swap(bb_im)], axis=3)
    kern = jnp.einsum('dgck,dgkn->dgcn', rhs, lhs, precision=HI)
    fwd, bwd = kern[0], kern[1]
    bwd_rev = bwd.reshape(g_, c_, t_, c_)[:, :, ::-1].reshape(g_, c_, gw)
    lagged = jnp.concatenate([bwd_rev[..., :gw - c_], fwd[..., :c_] + bwd[..., :c_], fwd[..., c_:]], axis=-1)
    w_intra = jnp.stack([lagged[..., (t_ - 1 - ti) * c_:(t_ - 1 - ti) * c_ + gw] for ti in range(t_)], axis=1)
    skip = jnp.eye(gw, dtype=F32) * jnp.tile(d_skip.astype(F32).reshape(g_, c_), (1, t_))[:, :, None]
    w_intra = (w_intra.reshape(g_, gw, gw) + skip).reshape(nb, GROUP_BLOCK, gw, gw)

    def pair_rows(bb):
        x = jnp.tile(swap(bb), (1, 1, t_, 1)).reshape(2, npair, 2, gw, p_)
        return jnp.concatenate([jnp.where(first, x, 0.0), jnp.where(first, 0.0, x)], axis=-1).reshape(
            2, npair, 2 * gw, 2 * p_)

    tok2 = jnp.tile(tok, 2)
    e_rows = jnp.stack([t_ - 1 - tok2, tok2])[:, None, :, None]
    pr, pi = cpow(e_rows, x_re.reshape(2, npair, 1, 2 * p_), x_im.reshape(2, npair, 1, 2 * p_))
    bt_re, bt_im = pair_rows(bb_re), pair_rows(bb_im)
    wb_re, wb_im = pr * bt_re - pi * bt_im, pr * bt_im + pi * bt_re
    wb = jnp.concatenate([wb_re[0], wb_im[0], wb_re[1], wb_im[1]], axis=-1)
    wb = wb.reshape(nb, GROUP_BLOCK // 2, 2 * gw, 8 * p_)

    def pair_cols(cc):
        x = jnp.tile(swap(cc), (1, 1, 1, t_)).reshape(2, npair, 2, p_, gw)
        return jnp.concatenate([jnp.where(first, x, 0.0), jnp.where(first, 0.0, x)], axis=-1).reshape(
            2, npair, 2 * p_, 2 * gw)

    e_cols = jnp.stack([tok2 + 1, t_ - tok2])[:, None, None, :]
    qr, qi = cpow(e_cols, x_re.reshape(2, npair, 2 * p_, 1), x_im.reshape(2, npair, 2 * p_, 1))
    ct_re, ct_im = pair_cols(cr), pair_cols(ci)
    wc_re, wc_im = ct_re * qr - ct_im * qi, -(ct_re * qi + ct_im * qr)
    wc = jnp.concatenate([wc_re[0], wc_im[0], wc_re[1], wc_im[1]], axis=1)
    wc = wc.reshape(nb, GROUP_BLOCK // 2, 8 * p_, 2 * gw)

    def table(e):
        re, im = cpow(e, x_re, x_im)
        tab = jnp.stack([re, im], axis=2)
        k = tab.shape[0]
        tab = tab.reshape(k, 2, 2, nb, GROUP_BLOCK // 2, 2 * p_).transpose(3, 4, 1, 2, 0, 5)
        return tab.reshape(nb, (GROUP_BLOCK // 2) * 4, k, 2 * p_)

    step = jnp.asarray([1.0, 2.0, 4.0], F32) * t_
    ap = table(jnp.broadcast_to(step[:, None, None, None], (3, 2, 1, 1)))
    ap = jnp.broadcast_to(ap[:, :, :, None, :], ap.shape[:3] + (8, 2 * p_)).reshape(nb, -1, 8, 2 * p_)
    a8 = table(jnp.full((1, 2, 1, 1), 8.0 * t_, F32))
    a8 = a8.reshape(nb, GROUP_BLOCK // 2, 2, 2, 2 * p_).transpose(0, 3, 1, 2, 4).reshape(nb, -1, 2 * p_)
    j8 = jnp.arange(8, dtype=F32) * t_
    pt = table(jnp.stack([j8, j8[::-1]], axis=1)[:, :, None, None])
    return w_intra.astype(BF16), wb.astype(BF16), wc.astype(BF16), ap, a8, pt


def _resident(a):
    return pl.BlockSpec(a.shape, lambda *_: (0,) * a.ndim, pipeline_mode=pl.Buffered(1))


def _merge_kernel(x_ref, att_ref, z_ref, gate_ref, wp_ref, wa_ref, wb_ref, wo_ref, o_ref, mix_ref,
                  *, tn):
    att = att_ref[...]
    z = z_ref[...]
    for j in range(D_MODEL // tn):
        cs = slice(j * tn, (j + 1) * tn)
        gs = slice(D_MODEL + j * tn, D_MODEL + (j + 1) * tn)
        y_attn = jnp.dot(att, wp_ref[:, cs], preferred_element_type=F32)
        y_ssm = (jnp.dot(z, wa_ref[:, cs], preferred_element_type=F32)
                 * jax.nn.sigmoid(jnp.dot(z, wb_ref[:, cs], preferred_element_type=F32)))
        g_a = jax.nn.sigmoid(gate_ref[:, cs].astype(F32))
        g_s = jax.nn.sigmoid(gate_ref[:, gs].astype(F32))
        mix_ref[:, cs] = (g_a * y_attn + g_s * y_ssm).astype(BF16)
    o_ref[...] = x_ref[...] + jnp.dot(mix_ref[...], wo_ref[...], preferred_element_type=F32)


def _merge(x2, att, z, gates, wp, wa, wb, wo):
    n = att.shape[0]
    tm = min(256, n)
    row = lambda cols: pl.BlockSpec((tm, cols), lambda i: (i, 0))
    return pl.pallas_call(
        functools.partial(_merge_kernel, tn=512),
        grid=(n // tm,),
        in_specs=[row(D_MODEL), row(ATTN_W), row(SSM_W), row(2 * D_MODEL),
                  _resident(wp), _resident(wa), _resident(wb), _resident(wo)],
        out_specs=row(D_MODEL),
        out_shape=jax.ShapeDtypeStruct((n, D_MODEL), F32),
        scratch_shapes=[pltpu.VMEM((tm, D_MODEL), BF16)],
        compiler_params=_params("parallel"),
        name="merge_mix_out",
    )(x2, att, z, gates, wp, wa, wb, wo)


def _norm_mm_kernel(x_ref, g_ref, w_ref, o_ref, xn_ref, *, scale):
    @pl.when(pl.program_id(1) == 0)
    def _():
        xn_ref[...] = _rms(x_ref[...], g_ref[...]).astype(BF16)

    acc = jnp.dot(xn_ref[...], w_ref[...], preferred_element_type=F32)
    o_ref[...] = (acc * scale).astype(BF16)


def _norm_mm(x, g, w, scale):
    n, kdim = x.shape
    cols = w.shape[1]
    tm = min(512, n)
    tn = 1024
    return pl.pallas_call(
        functools.partial(_norm_mm_kernel, scale=scale),
        grid=(n // tm, cols // tn),
        in_specs=[
            pl.BlockSpec((tm, kdim), lambda i, j: (i, 0)),
            pl.BlockSpec((1, kdim), lambda i, j: (0, 0)),
            pl.BlockSpec((kdim, tn), lambda i, j: (0, j)),
        ],
        out_specs=pl.BlockSpec((tm, tn), lambda i, j: (i, j)),
        out_shape=jax.ShapeDtypeStruct((n, cols), BF16),
        scratch_shapes=[pltpu.VMEM((tm, kdim), BF16)],
        compiler_params=_params("parallel", "arbitrary"),
        name="norm_matmul",
    )(x, g, w)


def _cross_kernel(x_ref, g_ref, wq_ref, kv_ref, wo_ref, o_ref, q_ref, oc_ref):
    x = x_ref[0]
    hn = _rms(x, g_ref[...]).astype(BF16)
    q_ref[...] = (jnp.dot(hn, wq_ref[...], preferred_element_type=F32)
                  * X_HEAD_DIM ** -0.5).astype(BF16)
    nt = (((1,), (1,)), ((), ()))
    for h in range(X_HEADS):
        cols = slice(h * X_HEAD_DIM, (h + 1) * X_HEAD_DIM)
        vcols = slice(D_MODEL + h * X_HEAD_DIM, D_MODEL + (h + 1) * X_HEAD_DIM)
        s = lax.dot_general(q_ref[:, cols], kv_ref[0, :, cols], nt, preferred_element_type=F32)
        p = jnp.exp(s - jnp.max(s, axis=-1, keepdims=True))
        l = jnp.sum(p, axis=-1, keepdims=True)
        o = jnp.dot(p.astype(BF16), kv_ref[0, :, vcols], preferred_element_type=F32)
        oc_ref[:, cols] = (o / l).astype(BF16)
    o_ref[0] = x + jnp.dot(oc_ref[...], wo_ref[...], preferred_element_type=F32)


def _cross(x3, g, wq, kv3, wo):
    b, seq, _ = x3.shape
    m = kv3.shape[1]
    tq = min(512, seq)
    blk = pl.BlockSpec((1, tq, D_MODEL), lambda b_, i: (b_, i, 0))
    return pl.pallas_call(
        _cross_kernel,
        grid=(b, seq // tq),
        in_specs=[blk, _resident(g), _resident(wq),
                  pl.BlockSpec((1, m, 2 * D_MODEL), lambda b_, i: (b_, 0, 0)), _resident(wo)],
        out_specs=blk,
        out_shape=jax.ShapeDtypeStruct(x3.shape, F32),
        scratch_shapes=[pltpu.VMEM((tq, D_MODEL), BF16), pltpu.VMEM((tq, D_MODEL), BF16)],
        compiler_params=_params("parallel", "arbitrary"),
        name="cross_attn_block",
    )(x3, g, wq, kv3, wo)


def _mlp_kernel(x_ref, g_ref, wu_ref, wd_ref, gf_ref, o_ref, hn_ref, acc_ref):
    f = pl.program_id(1)

    @pl.when(f == 0)
    def _():
        hn_ref[...] = _rms(x_ref[...], g_ref[...]).astype(BF16)
        acc_ref[...] = jnp.zeros_like(acc_ref)

    h = jnp.maximum(jnp.dot(hn_ref[...], wu_ref[...], preferred_element_type=F32), 0.0)
    acc_ref[...] += jnp.dot((h * h).astype(BF16), wd_ref[...], preferred_element_type=F32)

    @pl.when(f == pl.num_programs(1) - 1)
    def _():
        o_ref[...] = _rms(x_ref[...] + acc_ref[...], gf_ref[...])


def _mlp(x, g, wu, wd, gf):
    n = x.shape[0]
    tm = min(512, n)
    tf = 1024
    return pl.pallas_call(
        _mlp_kernel,
        grid=(n // tm, D_FF // tf),
        in_specs=[
            pl.BlockSpec((tm, D_MODEL), lambda i, f: (i, 0)),
            pl.BlockSpec((1, D_MODEL), lambda i, f: (0, 0)),
            pl.BlockSpec((D_MODEL, tf), lambda i, f: (0, f)),
            pl.BlockSpec((tf, D_MODEL), lambda i, f: (f, 0)),
            pl.BlockSpec((1, D_MODEL), lambda i, f: (0, 0)),
        ],
        out_specs=pl.BlockSpec((tm, D_MODEL), lambda i, f: (i, 0)),
        out_shape=jax.ShapeDtypeStruct((n, D_MODEL), F32),
        scratch_shapes=[pltpu.VMEM((tm, D_MODEL), BF16), pltpu.VMEM((tm, D_MODEL), F32)],
        compiler_params=_params("parallel", "arbitrary"),
        name="mlp_final_norm",
    )(x, g, wu, wd, gf)


def _rope_tables(seq):
    half = HEAD_DIM // 2
    inv = ROPE_THETA ** (-jnp.arange(0, HEAD_DIM, 2, dtype=F32) / HEAD_DIM)
    ang = jnp.arange(seq, dtype=F32)[:, None] * inv[None, :]
    cos, sin = jnp.cos(ang), jnp.sin(ang)
    cos_t = jnp.tile(cos, (1, LANES // half))
    sin_t = jnp.concatenate([-sin, -sin, sin, sin], axis=1)
    return (jnp.stack([cos_t * Q_SCALE, cos_t, jnp.ones_like(cos_t)]),
            jnp.stack([sin_t * Q_SCALE, sin_t, jnp.zeros_like(sin_t)]))


def _qk_shuffle():
    half = HEAD_DIM // 2
    src = jnp.arange(2 * ATTN_W).reshape(2 * N_HEADS, 2, 2, half).transpose(0, 2, 1, 3).reshape(-1)
    return (src[None, :] == jnp.arange(2 * ATTN_W)[:, None]).astype(BF16)


def _prepare(w):
    (norm_mix, w_in, diff_lambda, subln, w_attn_proj,
     lam_re, lam_im, log_dt, b_re, b_im, c_re, c_im, d_skip,
     w_glu_a, w_glu_b, w_mix_out,
     norm_cross, norm_mem, w_q_cross, w_kv_cross, w_o_cross,
     norm_mlp, w_mlp_up, w_mlp_down, norm_final) = w
    w_in0 = w_in[0].astype(BF16)
    w_qk = jnp.dot(w_in0[:, :2 * ATTN_W], _qk_shuffle(), preferred_element_type=BF16)
    w_in_p = jnp.concatenate([w_qk, w_in0[:, 2 * ATTN_W:]], axis=1)
    row = lambda v: v.astype(F32).reshape(1, -1)
    return dict(
        norm_mix=row(norm_mix[0]), w_in=w_in_p, diff_lambda=diff_lambda[0].astype(F32),
        subln=row(subln[0]), w_attn_proj=w_attn_proj[0].astype(BF16),
        ssm=_ssm_operators(lam_re[0], lam_im[0], log_dt[0], b_re[0], b_im[0], c_re[0], c_im[0],
                           d_skip[0]),
        w_glu_a=w_glu_a[0].astype(BF16), w_glu_b=w_glu_b[0].astype(BF16),
        w_mix_out=w_mix_out[0].astype(BF16),
        norm_cross=row(norm_cross[0]), norm_mem=row(norm_mem[0]),
        w_q_cross=w_q_cross[0].astype(BF16), w_kv_cross=w_kv_cross[0].astype(BF16),
        w_o_cross=w_o_cross[0].astype(BF16),
        norm_mlp=row(norm_mlp[0]), w_mlp_up=w_mlp_up[0].astype(BF16),
        w_mlp_down=w_mlp_down[0].astype(BF16), norm_final=row(norm_final),
    )


def _encode(x, mem, p):
    b, seq, _ = x.shape
    assert seq % (8 * CHUNK) == 0
    n = b * seq
    nc = seq // CHUNK
    x2 = x.reshape(n, D_MODEL)
    cos_t, sin_t = _rope_tables(seq)

    proj, u = _in_proj(x2, p["norm_mix"], p["w_in"], cos_t, sin_t, seq)
    att = _diff_attn(proj.reshape(b, seq, -1), p["diff_lambda"], p["subln"])

    z = _ssm(u.reshape(b * nc, CHUNK, SSM_W), b, *p["ssm"]).reshape(n, SSM_W)

    x2 = _merge(x2, att.reshape(n, ATTN_W), z, proj, p["w_attn_proj"], p["w_glu_a"], p["w_glu_b"],
                p["w_mix_out"])

    m_tok = mem.shape[1]
    kv = _norm_mm(mem.reshape(b * m_tok, D_MODEL), p["norm_mem"], p["w_kv_cross"], 1.0)
    x2 = _cross(x2.reshape(b, seq, D_MODEL), p["norm_cross"], p["w_q_cross"],
                kv.reshape(b, m_tok, 2 * D_MODEL), p["w_o_cross"]).reshape(n, D_MODEL)

    out = _mlp(x2, p["norm_mlp"], p["w_mlp_up"], p["w_mlp_down"], p["norm_final"])
    return out.reshape(b, seq, D_MODEL)


def kernel(x_prompt, x_sample, mem_prompt, mem_sample, norm_mix, w_in, diff_lambda, subln, w_attn_proj, ssm_lambda_re, ssm_lambda_im, ssm_log_dt, ssm_b_re, ssm_b_im, ssm_c_re, ssm_c_im, ssm_d, w_glu_a, w_glu_b, w_mix_out, norm_cross, norm_mem, w_q_cross, w_kv_cross, w_o_cross, norm_mlp, w_mlp_up, w_mlp_down, norm_final):
    p = _prepare((norm_mix, w_in, diff_lambda, subln, w_attn_proj,
                  ssm_lambda_re, ssm_lambda_im, ssm_log_dt, ssm_b_re, ssm_b_im, ssm_c_re, ssm_c_im,
                  ssm_d, w_glu_a, w_glu_b, w_mix_out,
                  norm_cross, norm_mem, w_q_cross, w_kv_cross, w_o_cross,
                  norm_mlp, w_mlp_up, w_mlp_down, norm_final))
    return (_encode(x_prompt, mem_prompt, p), _encode(x_sample, mem_sample, p))
```

```python
import functools
import math

import jax
import jax.numpy as jnp
from jax import lax
from jax.experimental import pallas as pl
from jax.experimental.pallas import tpu as pltpu

D_MODEL = 2048
N_HEADS = 8
HEAD_DIM = 64
ATTN_W = N_HEADS * 2 * HEAD_DIM
SSM_W = D_MODEL // 2
GROUP_SIZE = 16
N_GROUPS = SSM_W // GROUP_SIZE
STATE = 64
IN_COLS = 3 * ATTN_W + SSM_W + 2 * D_MODEL
D_FF = 4 * D_MODEL
X_HEADS = 4
X_HEAD_DIM = D_MODEL // X_HEADS
ROPE_THETA = 10000.0
EPS = 1e-6
LAM_INIT = 0.8 - 0.6 * math.exp(-0.3 * 0)
Q_SCALE = HEAD_DIM ** -0.5 * math.log2(math.e)

CHUNK = 16
GROUP_BLOCK = 8
LANES = 128
VMEM_LIMIT = 56 * 1024 * 1024

F32 = jnp.float32
BF16 = jnp.bfloat16
HI = lax.Precision.HIGHEST


def _params(*sem):
    return pltpu.CompilerParams(dimension_semantics=sem, vmem_limit_bytes=VMEM_LIMIT)


def _rms(x, g):
    return x * lax.rsqrt(jnp.mean(x * x, axis=-1, keepdims=True) + EPS) * g


def _in_proj_kernel(x_ref, g_ref, w_ref, cos_ref, sin_ref, o_ref, u_ref, xn_ref, acc_a, acc_b):
    j = pl.program_id(1)

    @pl.when(j == 0)
    def _():
        xn_ref[...] = _rms(x_ref[...], g_ref[...]).astype(BF16)

    def project(cur):
        cur[...] = jnp.dot(xn_ref[...], w_ref[...], preferred_element_type=F32)

    def finish(prev):
        cos = cos_ref[0]
        sin = sin_ref[0]
        for c in range(prev.shape[1] // LANES):
            xc = prev[:, c * LANES:(c + 1) * LANES]
            o_ref[:, c * LANES:(c + 1) * LANES] = (
                xc * cos + pltpu.roll(xc, LANES // 2, axis=1) * sin).astype(BF16)

    @pl.when(j == 0)
    def _():
        project(acc_a)

    @pl.when((j > 0) & (j % 2 == 0))
    def _():
        finish(acc_b)
        project(acc_a)

    @pl.when(j % 2 == 1)
    def _():
        finish(acc_a)
        project(acc_b)

    @pl.when(j == pl.num_programs(1) - 1)
    def _():
        u_ref[...] = acc_b[...]


def _in_proj(x2, g, w, cos_t, sin_t, seq):
    n = x2.shape[0]
    tm = min(512, seq)
    tn = ATTN_W
    nj = IN_COLS // tn
    assert nj % 2 == 0
    pos_blocks = seq // tm
    table = lambda i, j: (jnp.where(j == 5, 0, jnp.where(j == 6, 1, 2)), i % pos_blocks, 0)
    return pl.pallas_call(
        _in_proj_kernel,
        grid=(n // tm, nj),
        in_specs=[
            pl.BlockSpec((tm, D_MODEL), lambda i, j: (i, 0)),
            pl.BlockSpec((1, D_MODEL), lambda i, j: (0, 0)),
            pl.BlockSpec((D_MODEL, tn), lambda i, j: (0, (j + nj // 2) % nj)),
            pl.BlockSpec((1, tm, LANES), table),
            pl.BlockSpec((1, tm, LANES), table),
        ],
        out_specs=[
            pl.BlockSpec((tm, tn), lambda i, j: (i, jnp.maximum(j - 1, 0))),
            pl.BlockSpec((tm, tn), lambda i, j: (i, 0)),
        ],
        out_shape=[jax.ShapeDtypeStruct((n, (nj - 1) * tn), BF16),
                   jax.ShapeDtypeStruct((n, SSM_W), F32)],
        scratch_shapes=[pltpu.VMEM((tm, D_MODEL), BF16), pltpu.VMEM((tm, tn), F32),
                        pltpu.VMEM((tm, tn), F32)],
        compiler_params=_params("parallel", "arbitrary"),
        name="in_proj",
    )(x2, g, w, cos_t, sin_t)


def _diff_attn_kernel(dl_ref, sub_ref, q_ref, k_ref, v_ref, o_ref, kt_ref, s_ref, mx_ref, m_ref, acc_ref,
                      *, tq, tk, unroll):
    seq = v_ref.shape[1]
    nk = seq // tk
    steps = (seq // tq) * nk
    hw = 2 * HEAD_DIM
    lane = lax.broadcasted_iota(jnp.int32, (tq, hw), 1)
    first = ((lane // (HEAD_DIM // 2)) % 2) == 0
    ones = jnp.ones((tk, hw), BF16)
    dl = dl_ref[...]
    lam = (jnp.exp(jnp.sum(dl[0:1] * dl[1:2], axis=-1, keepdims=True))
           - jnp.exp(jnp.sum(dl[2:3] * dl[3:4], axis=-1, keepdims=True)) + LAM_INIT)

    def scores(f, slot):
        f = jnp.minimum(f, steps - 1)
        q = q_ref[0, pl.ds(pl.multiple_of((f // nk) * tq, tq), tq), :]
        kc = kt_ref[:, pl.ds(pl.multiple_of((f % nk) * tk, tk), tk)]
        zero = jnp.zeros_like(q)
        for comp, qm in enumerate((jnp.where(first, q, zero), jnp.where(first, zero, q))):
            s = jnp.dot(qm, kc, preferred_element_type=F32)
            s_ref[slot, comp] = s
            mx_ref[slot, comp] = jnp.max(s, axis=-1, keepdims=True)

    def accumulate(f, slot, emit):
        c = f % nk
        vx = jnp.concatenate([v_ref[0, pl.ds(pl.multiple_of(c * tk, tk), tk), :], ones], axis=1)
        acc = []
        for comp in range(2):
            m_old = jnp.where(c == 0, -jnp.inf, m_ref[comp])
            m_new = jnp.maximum(m_old, mx_ref[slot, comp])
            m_ref[comp] = m_new
            p = jnp.exp2(s_ref[slot, comp] - m_new).astype(BF16)
            acc.append(jnp.exp2(m_old - m_new) * acc_ref[comp]
                       + jnp.dot(p, vx, preferred_element_type=F32))
            acc_ref[comp] = acc[comp]
        if emit:
            o = acc[0][:, :hw] / acc[0][:, hw:] - lam * (acc[1][:, :hw] / acc[1][:, hw:])
            rows = pl.ds(pl.multiple_of((f // nk) * tq, tq), tq)
            o_ref[0, rows, :] = (_rms(o, sub_ref[...]) * (1.0 - LAM_INIT)).astype(BF16)

    for r in range(0, seq, tq):
        kt_ref[:, r:r + tq] = k_ref[0, r:r + tq, :].T

    m_ref[...] = jnp.full(m_ref.shape, -jnp.inf, F32)
    acc_ref[...] = jnp.zeros(acc_ref.shape, F32)
    scores(0, 0)

    def body(i, _):
        for k in range(unroll):
            f = i * unroll + k
            scores(f + 1, (k + 1) % 2)
            accumulate(f, k % 2, emit=(k + 1) % min(unroll, nk) == 0)
        return 0

    lax.fori_loop(0, steps // unroll, body, 0)


def _diff_attn(proj3, dl, sub):
    b, seq, _ = proj3.shape
    tq = min(256, seq)
    tk = min(4096, seq // 2)
    nk = seq // tk
    unroll = 4 if ((seq // tq) * nk) % 4 == 0 else 2
    assert nk % unroll == 0 or unroll % nk == 0
    hw = 2 * HEAD_DIM
    blk = pl.BlockSpec((1, seq, hw), lambda b_, h: (b_, 0, h))
    col = lambda first: pl.BlockSpec((1, seq, hw), lambda b_, h: (b_, 0, first // hw + h))
    return pl.pallas_call(
        functools.partial(_diff_attn_kernel, tq=tq, tk=tk, unroll=unroll),
        grid=(b, N_HEADS),
        in_specs=[
            pl.BlockSpec((4, HEAD_DIM), lambda b_, h: (0, 0)),
            pl.BlockSpec((1, hw), lambda b_, h: (0, 0)),
            col(2 * D_MODEL), col(2 * D_MODEL + ATTN_W), col(2 * D_MODEL + 2 * ATTN_W),
        ],
        out_specs=blk,
        out_shape=jax.ShapeDtypeStruct((b, seq, ATTN_W), BF16),
        scratch_shapes=[pltpu.VMEM((hw, seq), BF16),
                        pltpu.VMEM((2, 2, tq, tk), F32), pltpu.VMEM((2, 2, tq, 1), F32),
                        pltpu.VMEM((2, tq, 1), F32), pltpu.VMEM((2, tq, 2 * hw), F32)],
        compiler_params=_params("parallel", "arbitrary"),
        name="diff_attn",
    )(dl, sub, proj3, proj3, proj3)


def _slot_swap_matrix():
    src = jnp.arange(8 * LANES).reshape(8, 8, GROUP_SIZE).transpose(1, 0, 2).reshape(-1)
    return (src[None, :] == jnp.arange(8 * LANES)[:, None]).astype(BF16)


def _ssm_kernel(u_ref, perm_ref, wi_ref, wb_ref, wc_ref, ap_ref, a8_ref, pt_ref, z_ref,
                ut_ref, s_ref, zt_ref, e_ref, hc_ref):
    nc = u_ref.shape[0]
    n_tiles = nc // 8
    gw = CHUNK * GROUP_SIZE
    pw = 2 * gw

    for th in range(CHUNK // 8):
        by_token = jnp.concatenate([u_ref[:, th * 8 + tl, :] for tl in range(8)], axis=1)
        by_group = jnp.dot(by_token.astype(BF16), perm_ref[...], preferred_element_type=F32)
        for g in range(GROUP_BLOCK):
            ut_ref[:, g * gw + th * LANES:g * gw + (th + 1) * LANES] = (
                by_group[:, g * LANES:(g + 1) * LANES].astype(BF16))

    for q in range(GROUP_BLOCK // 2):
        cols = slice(q * pw, (q + 1) * pw)
        s_ref[:, cols] = jnp.dot(ut_ref[:, cols], wb_ref[0, q], preferred_element_type=F32)

    row = lax.broadcasted_iota(jnp.int32, (8, LANES), 0)

    def shift_down(x, k):
        return jnp.where(row >= k, pltpu.roll(x, k, axis=0), 0.0)

    def shift_up(x, k):
        return jnp.where(row < 8 - k, pltpu.roll(x, 8 - k, axis=0), 0.0)

    n_chain = 2 * (GROUP_BLOCK // 2)

    def chain_cols(q, d):
        base = q * pw + d * 2 * LANES
        return slice(base, base + LANES), slice(base + LANES, base + 2 * LANES)

    def local_scan(i, _):
        rs = pl.ds(pl.multiple_of(i * 8, 8), 8)
        for q in range(GROUP_BLOCK // 2):
            for d in range(2):
                c_re, c_im = chain_cols(q, d)
                shift = shift_down if d == 0 else shift_up
                ch = q * 2 + d
                x_re, x_im = s_ref[rs, c_re], s_ref[rs, c_im]
                for ki, k in enumerate((1, 2, 4)):
                    a_re, a_im = ap_ref[0, (2 * ch) * 3 + ki], ap_ref[0, (2 * ch + 1) * 3 + ki]
                    y_re, y_im = shift(x_re, k), shift(x_im, k)
                    x_re, x_im = (x_re + a_re * y_re - a_im * y_im,
                                  x_im + a_re * y_im + a_im * y_re)
                s_ref[rs, c_re] = shift(x_re, 1)
                s_ref[rs, c_im] = shift(x_im, 1)
                pos = i if d == 0 else n_tiles - 1 - i
                edge = slice(7, 8) if d == 0 else slice(0, 1)
                e_ref[pos, ch:ch + 1, :] = x_re[edge]
                e_ref[pos, n_chain + ch:n_chain + ch + 1, :] = x_im[edge]
        return 0

    lax.fori_loop(0, n_tiles, local_scan, 0)

    a8_re, a8_im = a8_ref[0, :n_chain], a8_ref[0, n_chain:]

    def carry_in(i, h):
        h_re, h_im = h
        hc_ref[i, :n_chain, :] = h_re
        hc_ref[i, n_chain:, :] = h_im
        e_re, e_im = e_ref[i, :n_chain, :], e_ref[i, n_chain:, :]
        return a8_re * h_re - a8_im * h_im + e_re, a8_re * h_im + a8_im * h_re + e_im

    zero = jnp.zeros((n_chain, LANES), F32)
    lax.fori_loop(0, n_tiles, carry_in, (zero, zero))

    def add_carry(i, _):
        rs = pl.ds(pl.multiple_of(i * 8, 8), 8)
        for q in range(GROUP_BLOCK // 2):
            for d in range(2):
                c_re, c_im = chain_cols(q, d)
                ch = q * 2 + d
                pos = i if d == 0 else n_tiles - 1 - i
                h_re = jnp.broadcast_to(hc_ref[pos, ch:ch + 1, :], (8, LANES))
                h_im = jnp.broadcast_to(hc_ref[pos, n_chain + ch:n_chain + ch + 1, :], (8, LANES))
                p_re, p_im = pt_ref[0, 2 * ch], pt_ref[0, 2 * ch + 1]
                s_ref[rs, c_re] += p_re * h_re - p_im * h_im
                s_ref[rs, c_im] += p_re * h_im + p_im * h_re
        return 0

    lax.fori_loop(0, n_tiles, add_carry, 0)

    for q in range(GROUP_BLOCK // 2):
        cols = slice(q * pw, (q + 1) * pw)
        carried = jnp.dot(s_ref[:, cols].astype(BF16), wc_ref[0, q], preferred_element_type=F32)
        for g2 in range(2):
            g = 2 * q + g2
            gc = slice(g * gw, (g + 1) * gw)
            y = carried[:, g2 * gw:(g2 + 1) * gw] + jnp.dot(
                ut_ref[:, gc], wi_ref[0, g], preferred_element_type=F32)
            z = 0.5 * y * (1.0 + jnp.tanh(0.7978845608028654 * (y + 0.044715 * (y * y * y))))
            ut_ref[:, gc] = z.astype(BF16)

    for th in range(CHUNK // 8):
        by_group = jnp.concatenate(
            [ut_ref[:, g * gw + th * LANES:g * gw + (th + 1) * LANES] for g in range(GROUP_BLOCK)], axis=1)
        by_token = jnp.dot(by_group, perm_ref[...], preferred_element_type=F32)
        for tl in range(8):
            zt_ref[:, th * 8 + tl, :] = by_token[:, tl * LANES:(tl + 1) * LANES]
    z_ref[...] = zt_ref[...].astype(BF16)


def _ssm(u3, b, wi, wb, wc, ap, a8, pt):
    nc = u3.shape[0] // b
    cw = GROUP_BLOCK * CHUNK * GROUP_SIZE
    blk = pl.BlockSpec((nc, CHUNK, LANES), lambda g, b_: (b_, 0, g))
    whole = lambda a: pl.BlockSpec((1,) + a.shape[1:], lambda g, b_: (g,) + (0,) * (a.ndim - 1))
    carry = pltpu.VMEM((nc // 8, 2 * GROUP_BLOCK, LANES), F32)
    perm = _slot_swap_matrix()
    return pl.pallas_call(
        _ssm_kernel,
        grid=(N_GROUPS // GROUP_BLOCK, b),
        in_specs=[blk, pl.BlockSpec(perm.shape, lambda g, b_: (0, 0)),
                  whole(wi), whole(wb), whole(wc), whole(ap), whole(a8), whole(pt)],
        out_specs=blk,
        out_shape=jax.ShapeDtypeStruct(u3.shape, BF16),
        scratch_shapes=[pltpu.VMEM((nc, cw), BF16), pltpu.VMEM((nc, cw), F32),
                        pltpu.VMEM((nc, CHUNK, LANES), F32), carry, carry],
        compiler_params=_params("parallel", "arbitrary"),
        name="s5_scan",
    )(u3, perm, wi, wb, wc, ap, a8, pt)


def _ssm_operators(lam_re, lam_im, log_dt, b_re, b_im, c_re, c_im, d_skip):
    t_ = CHUNK
    g_, p_, c_ = N_GROUPS, STATE, GROUP_SIZE
    nb, npair = g_ // GROUP_BLOCK, g_ // 2
    lr, li = lam_re.astype(F32), lam_im.astype(F32)
    dt = jnp.exp(log_dt.astype(F32))[..., None]
    mag = jnp.exp(lr * dt)
    ab_re, ab_im = mag * jnp.cos(li * dt), mag * jnp.sin(li * dt)
    den = lr * lr + li * li
    n_re, n_im = ab_re - 1.0, ab_im
    k_re = (n_re * lr + n_im * li) / den
    k_im = (n_im * lr - n_re * li) / den
    br, bi = b_re.astype(F32), b_im.astype(F32)
    bb_re = k_re[..., None] * br - k_im[..., None] * bi
    bb_im = k_re[..., None] * bi + k_im[..., None] * br
    cr, ci = c_re.astype(F32), c_im.astype(F32)

    x_re, x_im = lr * dt, li * dt
    gw = t_ * c_

    def cpow(e, xr, xi):
        m = jnp.exp(e * xr)
        return m * jnp.cos(e * xi), m * jnp.sin(e * xi)

    swap = lambda a: a.transpose(0, 1, 3, 2)
    tok = jnp.repeat(jnp.arange(t_, dtype=F32), c_)
    first = (jnp.arange(2) == 0)[None, None, :, None, None]

    pw_r, pw_i = cpow(tok, x_re[..., None], x_im[..., None])
    rep = jnp.tile(jnp.eye(c_, dtype=F32), (1, t_))
    crt = jnp.einsum('dgop,on->dgpn', cr, rep, precision=HI)
    cit = jnp.einsum('dgop,on->dgpn', ci, rep, precision=HI)
    lhs = jnp.concatenate([crt * pw_r - cit * pw_i, -(crt * pw_i + cit * pw_r)], axis=2)
    rhs = jnp.concatenate([swap(bb_re), swap(bb_im)], axis=3)
    kern = jnp.einsum('dgck,dgkn->dgcn', rhs, lhs, precision=HI)
    fwd, bwd = kern[0], kern[1]
    bwd_rev = bwd.reshape(g_, c_, t_, c_)[:, :, ::-1].reshape(g_, c_, gw)
    lagged = jnp.concatenate([bwd_rev[..., :gw - c_], fwd[..., :c_] + bwd[..., :c_], fwd[..., c_:]], axis=-1)
    n_lag = lagged.shape[-1]
    start = (t_ - 1 - jnp.arange(t_)) * c_
    window = (jnp.arange(n_lag)[None, :, None] == start[:, None, None] + jnp.arange(gw)[None, None, :])
    w_intra = jnp.einsum('gck,tkn->gtcn', lagged.astype(BF16), window.astype(BF16),
                         preferred_element_type=F32)
    skip = jnp.eye(gw, dtype=F32) * jnp.tile(d_skip.astype(F32).reshape(g_, c_), (1, t_))[:, :, None]
    w_intra = (w_intra.reshape(g_, gw, gw) + skip).reshape(nb, GROUP_BLOCK, gw, gw)

    def pair_rows(bb):
        x = jnp.tile(swap(bb), (1, 1, t_, 1)).reshape(2, npair, 2, gw, p_)
        return jnp.concatenate([jnp.where(first, x, 0.0), jnp.where(first, 0.0, x)], axis=-1).reshape(
            2, npair, 2 * gw, 2 * p_)

    tok2 = jnp.tile(tok, 2)
    e_rows = jnp.stack([t_ - 1 - tok2, tok2])[:, None, :, None]
    pr, pi = cpow(e_rows, x_re.reshape(2, npair, 1, 2 * p_), x_im.reshape(2, npair, 1, 2 * p_))
    bt_re, bt_im = pair_rows(bb_re), pair_rows(bb_im)
    wb_re, wb_im = pr * bt_re - pi * bt_im, pr * bt_im + pi * bt_re
    wb = jnp.concatenate([wb_re[0], wb_im[0], wb_re[1], wb_im[1]], axis=-1)
    wb = wb.reshape(nb, GROUP_BLOCK // 2, 2 * gw, 8 * p_)

    def pair_cols(cc_tiled):
        x = cc_tiled.reshape(2, npair, 2, p_, gw)
        return jnp.concatenate([jnp.where(first, x, 0.0), jnp.where(first, 0.0, x)], axis=-1).reshape(
            2, npair, 2 * p_, 2 * gw)

    e_cols = jnp.stack([tok2 + 1, t_ - tok2])[:, None, None, :]
    qr, qi = cpow(e_cols, x_re.reshape(2, npair, 2 * p_, 1), x_im.reshape(2, npair, 2 * p_, 1))
    ct_re, ct_im = pair_cols(crt), pair_cols(cit)
    wc_re, wc_im = ct_re * qr - ct_im * qi, -(ct_re * qi + ct_im * qr)
    wc = jnp.concatenate([wc_re[0], wc_im[0], wc_re[1], wc_im[1]], axis=1)
    wc = wc.reshape(nb, GROUP_BLOCK // 2, 8 * p_, 2 * gw)

    def table(e):
        re, im = cpow(e, x_re, x_im)
        tab = jnp.stack([re, im], axis=2)
        k = tab.shape[0]
        tab = tab.reshape(k, 2, 2, nb, GROUP_BLOCK // 2, 2 * p_).transpose(3, 4, 1, 2, 0, 5)
        return tab.reshape(nb, (GROUP_BLOCK // 2) * 4, k, 2 * p_)

    step = jnp.asarray([1.0, 2.0, 4.0], F32) * t_
    ap = table(jnp.broadcast_to(step[:, None, None, None], (3, 2, 1, 1)))
    ap = jnp.broadcast_to(ap[:, :, :, None, :], ap.shape[:3] + (8, 2 * p_)).reshape(nb, -1, 8, 2 * p_)
    a8 = table(jnp.full((1, 2, 1, 1), 8.0 * t_, F32))
    a8 = a8.reshape(nb, GROUP_BLOCK // 2, 2, 2, 2 * p_).transpose(0, 3, 1, 2, 4).reshape(nb, -1, 2 * p_)
    j8 = jnp.arange(8, dtype=F32) * t_
    pt = table(jnp.stack([j8, j8[::-1]], axis=1)[:, :, None, None])
    return w_intra.astype(BF16), wb.astype(BF16), wc.astype(BF16), ap, a8, pt


def _resident(a):
    return pl.BlockSpec(a.shape, lambda *_: (0,) * a.ndim, pipeline_mode=pl.Buffered(1))


def _merge_kernel(x_ref, att_ref, z_ref, gate_ref, wp_ref, wa_ref, wb_ref, wo_ref, o_ref, mix_ref,
                  *, tn):
    att = att_ref[...]
    z = z_ref[...]
    for j in range(D_MODEL // tn):
        cs = slice(j * tn, (j + 1) * tn)
        gs = slice(D_MODEL + j * tn, D_MODEL + (j + 1) * tn)
        y_attn = jnp.dot(att, wp_ref[:, cs], preferred_element_type=F32)
        y_ssm = (jnp.dot(z, wa_ref[:, cs], preferred_element_type=F32)
                 * jax.nn.sigmoid(jnp.dot(z, wb_ref[:, cs], preferred_element_type=F32)))
        g_a = jax.nn.sigmoid(gate_ref[:, cs].astype(F32))
        g_s = jax.nn.sigmoid(gate_ref[:, gs].astype(F32))
        mix_ref[:, cs] = (g_a * y_attn + g_s * y_ssm).astype(BF16)
    o_ref[...] = x_ref[...] + jnp.dot(mix_ref[...], wo_ref[...], preferred_element_type=F32)


def _merge(x2, att, z, gates, wp, wa, wb, wo):
    n = att.shape[0]
    tm = min(256, n)
    row = lambda cols: pl.BlockSpec((tm, cols), lambda i: (i, 0))
    return pl.pallas_call(
        functools.partial(_merge_kernel, tn=512),
        grid=(n // tm,),
        in_specs=[row(D_MODEL), row(ATTN_W), row(SSM_W), row(2 * D_MODEL),
                  _resident(wp), _resident(wa), _resident(wb), _resident(wo)],
        out_specs=row(D_MODEL),
        out_shape=jax.ShapeDtypeStruct((n, D_MODEL), F32),
        scratch_shapes=[pltpu.VMEM((tm, D_MODEL), BF16)],
        compiler_params=_params("parallel"),
        name="merge_mix_out",
    )(x2, att, z, gates, wp, wa, wb, wo)


def _norm_mm_kernel(x_ref, g_ref, w_ref, o_ref, xn_ref, *, scale):
    @pl.when(pl.program_id(1) == 0)
    def _():
        xn_ref[...] = _rms(x_ref[...], g_ref[...]).astype(BF16)

    acc = jnp.dot(xn_ref[...], w_ref[...], preferred_element_type=F32)
    o_ref[...] = (acc * scale).astype(BF16)


def _norm_mm(x, g, w, scale):
    n, kdim = x.shape
    cols = w.shape[1]
    tm = min(512, n)
    tn = 1024
    return pl.pallas_call(
        functools.partial(_norm_mm_kernel, scale=scale),
        grid=(n // tm, cols // tn),
        in_specs=[
            pl.BlockSpec((tm, kdim), lambda i, j: (i, 0)),
            pl.BlockSpec((1, kdim), lambda i, j: (0, 0)),
            pl.BlockSpec((kdim, tn), lambda i, j: (0, j)),
        ],
        out_specs=pl.BlockSpec((tm, tn), lambda i, j: (i, j)),
        out_shape=jax.ShapeDtypeStruct((n, cols), BF16),
        scratch_shapes=[pltpu.VMEM((tm, kdim), BF16)],
        compiler_params=_params("parallel", "arbitrary"),
        name="norm_matmul",
    )(x, g, w)


def _cross_kernel(x_ref, g_ref, wq_ref, kv_ref, wo_ref, o_ref, q_ref, oc_ref):
    x = x_ref[0]
    hn = _rms(x, g_ref[...]).astype(BF16)
    q_ref[...] = (jnp.dot(hn, wq_ref[...], preferred_element_type=F32)
                  * X_HEAD_DIM ** -0.5).astype(BF16)
    nt = (((1,), (1,)), ((), ()))
    for h in range(X_HEADS):
        cols = slice(h * X_HEAD_DIM, (h + 1) * X_HEAD_DIM)
        vcols = slice(D_MODEL + h * X_HEAD_DIM, D_MODEL + (h + 1) * X_HEAD_DIM)
        s = lax.dot_general(q_ref[:, cols], kv_ref[0, :, cols], nt, preferred_element_type=F32)
        p = jnp.exp(s - jnp.max(s, axis=-1, keepdims=True))
        l = jnp.sum(p, axis=-1, keepdims=True)
        o = jnp.dot(p.astype(BF16), kv_ref[0, :, vcols], preferred_element_type=F32)
        oc_ref[:, cols] = (o / l).astype(BF16)
    o_ref[0] = x + jnp.dot(oc_ref[...], wo_ref[...], preferred_element_type=F32)


def _cross(x3, g, wq, kv3, wo):
    b, seq, _ = x3.shape
    m = kv3.shape[1]
    tq = min(512, seq)
    blk = pl.BlockSpec((1, tq, D_MODEL), lambda b_, i: (b_, i, 0))
    return pl.pallas_call(
        _cross_kernel,
        grid=(b, seq // tq),
        in_specs=[blk, _resident(g), _resident(wq),
                  pl.BlockSpec((1, m, 2 * D_MODEL), lambda b_, i: (b_, 0, 0)), _resident(wo)],
        out_specs=blk,
        out_shape=jax.ShapeDtypeStruct(x3.shape, F32),
        scratch_shapes=[pltpu.VMEM((tq, D_MODEL), BF16), pltpu.VMEM((tq, D_MODEL), BF16)],
        compiler_params=_params("parallel", "arbitrary"),
        name="cross_attn_block",
    )(x3, g, wq, kv3, wo)


def _mlp_kernel(x_ref, g_ref, wu_ref, wd_ref, gf_ref, o_ref, hn_ref, acc_ref):
    f = pl.program_id(1)

    @pl.when(f == 0)
    def _():
        hn_ref[...] = _rms(x_ref[...], g_ref[...]).astype(BF16)
        acc_ref[...] = jnp.zeros_like(acc_ref)

    h = jnp.maximum(jnp.dot(hn_ref[...], wu_ref[...], preferred_element_type=F32), 0.0)
    acc_ref[...] += jnp.dot((h * h).astype(BF16), wd_ref[...], preferred_element_type=F32)

    @pl.when(f == pl.num_programs(1) - 1)
    def _():
        o_ref[...] = _rms(x_ref[...] + acc_ref[...], gf_ref[...])


def _mlp(x, g, wu, wd, gf):
    n = x.shape[0]
    tm = min(512, n)
    tf = 1024
    return pl.pallas_call(
        _mlp_kernel,
        grid=(n // tm, D_FF // tf),
        in_specs=[
            pl.BlockSpec((tm, D_MODEL), lambda i, f: (i, 0)),
            pl.BlockSpec((1, D_MODEL), lambda i, f: (0, 0)),
            pl.BlockSpec((D_MODEL, tf), lambda i, f: (0, f)),
            pl.BlockSpec((tf, D_MODEL), lambda i, f: (f, 0)),
            pl.BlockSpec((1, D_MODEL), lambda i, f: (0, 0)),
        ],
        out_specs=pl.BlockSpec((tm, D_MODEL), lambda i, f: (i, 0)),
        out_shape=jax.ShapeDtypeStruct((n, D_MODEL), F32),
        scratch_shapes=[pltpu.VMEM((tm, D_MODEL), BF16), pltpu.VMEM((tm, D_MODEL), F32)],
        compiler_params=_params("parallel", "arbitrary"),
        name="mlp_final_norm",
    )(x, g, wu, wd, gf)


def _rope_tables(seq):
    half = HEAD_DIM // 2
    inv = ROPE_THETA ** (-jnp.arange(0, HEAD_DIM, 2, dtype=F32) / HEAD_DIM)
    ang = jnp.arange(seq, dtype=F32)[:, None] * inv[None, :]
    cos, sin = jnp.cos(ang), jnp.sin(ang)
    cos_t = jnp.tile(cos, (1, LANES // half))
    sin_t = jnp.concatenate([-sin, -sin, sin, sin], axis=1)
    return (jnp.stack([cos_t * Q_SCALE, cos_t, jnp.ones_like(cos_t)]),
            jnp.stack([sin_t * Q_SCALE, sin_t, jnp.zeros_like(sin_t)]))


def _qk_shuffle():
    half = HEAD_DIM // 2
    src = jnp.arange(2 * ATTN_W).reshape(2 * N_HEADS, 2, 2, half).transpose(0, 2, 1, 3).reshape(-1)
    return (src[None, :] == jnp.arange(2 * ATTN_W)[:, None]).astype(BF16)


def _prepare(w):
    (norm_mix, w_in, diff_lambda, subln, w_attn_proj,
     lam_re, lam_im, log_dt, b_re, b_im, c_re, c_im, d_skip,
     w_glu_a, w_glu_b, w_mix_out,
     norm_cross, norm_mem, w_q_cross, w_kv_cross, w_o_cross,
     norm_mlp, w_mlp_up, w_mlp_down, norm_final) = w
    w_in0 = w_in[0].astype(BF16)
    w_qk = jnp.dot(w_in0[:, :2 * ATTN_W], _qk_shuffle(), preferred_element_type=BF16)
    w_in_p = jnp.concatenate([w_qk, w_in0[:, 2 * ATTN_W:]], axis=1)
    row = lambda v: v.astype(F32).reshape(1, -1)
    return dict(
        norm_mix=row(norm_mix[0]), w_in=w_in_p, diff_lambda=diff_lambda[0].astype(F32),
        subln=row(subln[0]), w_attn_proj=w_attn_proj[0].astype(BF16),
        ssm=_ssm_operators(lam_re[0], lam_im[0], log_dt[0], b_re[0], b_im[0], c_re[0], c_im[0],
                           d_skip[0]),
        w_glu_a=w_glu_a[0].astype(BF16), w_glu_b=w_glu_b[0].astype(BF16),
        w_mix_out=w_mix_out[0].astype(BF16),
        norm_cross=row(norm_cross[0]), norm_mem=row(norm_mem[0]),
        w_q_cross=w_q_cross[0].astype(BF16), w_kv_cross=w_kv_cross[0].astype(BF16),
        w_o_cross=w_o_cross[0].astype(BF16),
        norm_mlp=row(norm_mlp[0]), w_mlp_up=w_mlp_up[0].astype(BF16),
        w_mlp_down=w_mlp_down[0].astype(BF16), norm_final=row(norm_final),
    )


def _encode(x, mem, p):
    b, seq, _ = x.shape
    assert seq % (8 * CHUNK) == 0
    n = b * seq
    nc = seq // CHUNK
    x2 = x.reshape(n, D_MODEL)
    cos_t, sin_t = _rope_tables(seq)

    proj, u = _in_proj(x2, p["norm_mix"], p["w_in"], cos_t, sin_t, seq)
    att = _diff_attn(proj.reshape(b, seq, -1), p["diff_lambda"], p["subln"])

    z = _ssm(u.reshape(b * nc, CHUNK, SSM_W), b, *p["ssm"]).reshape(n, SSM_W)

    x2 = _merge(x2, att.reshape(n, ATTN_W), z, proj, p["w_attn_proj"], p["w_glu_a"], p["w_glu_b"],
                p["w_mix_out"])

    m_tok = mem.shape[1]
    kv = _norm_mm(mem.reshape(b * m_tok, D_MODEL), p["norm_mem"], p["w_kv_cross"], 1.0)
    x2 = _cross(x2.reshape(b, seq, D_MODEL), p["norm_cross"], p["w_q_cross"],
                kv.reshape(b, m_tok, 2 * D_MODEL), p["w_o_cross"]).reshape(n, D_MODEL)

    out = _mlp(x2, p["norm_mlp"], p["w_mlp_up"], p["w_mlp_down"], p["norm_final"])
    return out.reshape(b, seq, D_MODEL)


def kernel(x_prompt, x_sample, mem_prompt, mem_sample, norm_mix, w_in, diff_lambda, subln, w_attn_proj, ssm_lambda_re, ssm_lambda_im, ssm_log_dt, ssm_b_re, ssm_b_im, ssm_c_re, ssm_c_im, ssm_d, w_glu_a, w_glu_b, w_mix_out, norm_cross, norm_mem, w_q_cross, w_kv_cross, w_o_cross, norm_mlp, w_mlp_up, w_mlp_down, norm_final):
    p = _prepare((norm_mix, w_in, diff_lambda, subln, w_attn_proj,
                  ssm_lambda_re, ssm_lambda_im, ssm_log_dt, ssm_b_re, ssm_b_im, ssm_c_re, ssm_c_im,
                  ssm_d, w_glu_a, w_glu_b, w_mix_out,
                  norm_cross, norm_mem, w_q_cross, w_kv_cross, w_o_cross,
                  norm_mlp, w_mlp_up, w_mlp_down, norm_final))
    return (_encode(x_prompt, mem_prompt, p), _encode(x_sample, mem_sample, p))
```

```python
import functools
import math

import jax
import jax.numpy as jnp
from jax import lax
from jax.experimental import pallas as pl
from jax.experimental.pallas import tpu as pltpu

D_MODEL = 2048
N_HEADS = 8
HEAD_DIM = 64
ATTN_W = N_HEADS * 2 * HEAD_DIM
SSM_W = D_MODEL // 2
GROUP_SIZE = 16
N_GROUPS = SSM_W // GROUP_SIZE
STATE = 64
IN_COLS = 3 * ATTN_W + SSM_W + 2 * D_MODEL
D_FF = 4 * D_MODEL
X_HEADS = 4
X_HEAD_DIM = D_MODEL // X_HEADS
ROPE_THETA = 10000.0
EPS = 1e-6
LAM_INIT = 0.8 - 0.6 * math.exp(-0.3 * 0)
Q_SCALE = HEAD_DIM ** -0.5 * math.log2(math.e)

CHUNK = 16
GROUP_BLOCK = 8
LANES = 128
VMEM_LIMIT = 56 * 1024 * 1024

F32 = jnp.float32
BF16 = jnp.bfloat16
HI = lax.Precision.HIGHEST


def _params(*sem):
    return pltpu.CompilerParams(dimension_semantics=sem, vmem_limit_bytes=VMEM_LIMIT)


def _rms(x, g):
    return x * lax.rsqrt(jnp.mean(x * x, axis=-1, keepdims=True) + EPS) * g


def _in_proj_kernel(x_ref, g_ref, w_ref, cos_ref, sin_ref, o_ref, u_ref, xn_ref, acc_a, acc_b):
    j = pl.program_id(1)

    @pl.when(j == 0)
    def _():
        xn_ref[...] = _rms(x_ref[...], g_ref[...]).astype(BF16)

    def project(cur):
        cur[...] = jnp.dot(xn_ref[...], w_ref[...], preferred_element_type=F32)

    def finish(prev):
        cos = cos_ref[0]
        sin = sin_ref[0]
        for c in range(prev.shape[1] // LANES):
            xc = prev[:, c * LANES:(c + 1) * LANES]
            o_ref[:, c * LANES:(c + 1) * LANES] = (
                xc * cos + pltpu.roll(xc, LANES // 2, axis=1) * sin).astype(BF16)

    @pl.when(j == 0)
    def _():
        project(acc_a)

    @pl.when((j > 0) & (j % 2 == 0))
    def _():
        finish(acc_b)
        project(acc_a)

    @pl.when(j % 2 == 1)
    def _():
        finish(acc_a)
        project(acc_b)

    @pl.when(j == pl.num_programs(1) - 1)
    def _():
        u_ref[...] = acc_b[...]


def _in_proj(x2, g, w, cos_t, sin_t, seq):
    n = x2.shape[0]
    tm = min(1024, seq)
    tn = ATTN_W
    nj = IN_COLS // tn
    assert nj % 2 == 0
    pos_blocks = seq // tm
    table = lambda i, j: (jnp.where(j == 5, 0, jnp.where(j == 6, 1, 2)), i % pos_blocks, 0)
    return pl.pallas_call(
        _in_proj_kernel,
        grid=(n // tm, nj),
        in_specs=[
            pl.BlockSpec((tm, D_MODEL), lambda i, j: (i, 0)),
            pl.BlockSpec((1, D_MODEL), lambda i, j: (0, 0)),
            pl.BlockSpec((D_MODEL, tn), lambda i, j: (0, (j + nj // 2) % nj)),
            pl.BlockSpec((1, tm, LANES), table),
            pl.BlockSpec((1, tm, LANES), table),
        ],
        out_specs=[
            pl.BlockSpec((tm, tn), lambda i, j: (i, jnp.maximum(j - 1, 0))),
            pl.BlockSpec((tm, tn), lambda i, j: (i, 0)),
        ],
        out_shape=[jax.ShapeDtypeStruct((n, (nj - 1) * tn), BF16),
                   jax.ShapeDtypeStruct((n, SSM_W), F32)],
        scratch_shapes=[pltpu.VMEM((tm, D_MODEL), BF16), pltpu.VMEM((tm, tn), F32),
                        pltpu.VMEM((tm, tn), F32)],
        compiler_params=_params("parallel", "arbitrary"),
        name="in_proj",
    )(x2, g, w, cos_t, sin_t)


def _diff_attn_kernel(dl_ref, sub_ref, q_ref, k_ref, v_ref, o_ref, kt_ref, s_ref, mx_ref, m_ref, acc_ref,
                      *, tq, tk, unroll):
    seq = v_ref.shape[1]
    nk = seq // tk
    steps = (seq // tq) * nk
    hw = 2 * HEAD_DIM
    lane = lax.broadcasted_iota(jnp.int32, (tq, hw), 1)
    first = ((lane // (HEAD_DIM // 2)) % 2) == 0
    ones = jnp.ones((tk, hw), BF16)
    dl = dl_ref[...]
    lam = (jnp.exp(jnp.sum(dl[0:1] * dl[1:2], axis=-1, keepdims=True))
           - jnp.exp(jnp.sum(dl[2:3] * dl[3:4], axis=-1, keepdims=True)) + LAM_INIT)

    def scores(f, slot):
        f = jnp.minimum(f, steps - 1)
        q = q_ref[0, pl.ds(pl.multiple_of((f // nk) * tq, tq), tq), :]
        kc = kt_ref[:, pl.ds(pl.multiple_of((f % nk) * tk, tk), tk)]
        zero = jnp.zeros_like(q)
        for comp, qm in enumerate((jnp.where(first, q, zero), jnp.where(first, zero, q))):
            s = jnp.dot(qm, kc, preferred_element_type=F32)
            s_ref[slot, comp] = s
            mx_ref[slot, comp] = jnp.max(s, axis=-1, keepdims=True)

    def accumulate(f, slot, emit):
        c = f % nk
        vx = jnp.concatenate([v_ref[0, pl.ds(pl.multiple_of(c * tk, tk), tk), :], ones], axis=1)
        acc = []
        for comp in range(2):
            m_old = jnp.where(c == 0, -jnp.inf, m_ref[comp])
            m_new = jnp.maximum(m_old, mx_ref[slot, comp])
            m_ref[comp] = m_new
            p = jnp.exp2(s_ref[slot, comp] - m_new).astype(BF16)
            acc.append(jnp.exp2(m_old - m_new) * acc_ref[comp]
                       + jnp.dot(p, vx, preferred_element_type=F32))
            acc_ref[comp] = acc[comp]
        if emit:
            o = acc[0][:, :hw] / acc[0][:, hw:] - lam * (acc[1][:, :hw] / acc[1][:, hw:])
            rows = pl.ds(pl.multiple_of((f // nk) * tq, tq), tq)
            o_ref[0, rows, :] = (_rms(o, sub_ref[...]) * (1.0 - LAM_INIT)).astype(BF16)

    for r in range(0, seq, tq):
        kt_ref[:, r:r + tq] = k_ref[0, r:r + tq, :].T

    m_ref[...] = jnp.full(m_ref.shape, -jnp.inf, F32)
    acc_ref[...] = jnp.zeros(acc_ref.shape, F32)
    scores(0, 0)

    def body(i, _):
        for k in range(unroll):
            f = i * unroll + k
            scores(f + 1, (k + 1) % 2)
            accumulate(f, k % 2, emit=(k + 1) % min(unroll, nk) == 0)
        return 0

    lax.fori_loop(0, steps // unroll, body, 0)


def _diff_attn(proj3, dl, sub):
    b, seq, _ = proj3.shape
    tq = min(256, seq)
    tk = seq if seq <= 2048 else min(4096, seq // 2)
    nk = seq // tk
    steps = (seq // tq) * nk
    unroll = 4 if steps % 4 == 0 else (2 if steps % 2 == 0 else 1)
    assert nk % unroll == 0 or unroll % nk == 0
    hw = 2 * HEAD_DIM
    blk = pl.BlockSpec((1, seq, hw), lambda b_, h: (b_, 0, h))
    col = lambda first: pl.BlockSpec((1, seq, hw), lambda b_, h: (b_, 0, first // hw + h))
    return pl.pallas_call(
        functools.partial(_diff_attn_kernel, tq=tq, tk=tk, unroll=unroll),
        grid=(b, N_HEADS),
        in_specs=[
            pl.BlockSpec((4, HEAD_DIM), lambda b_, h: (0, 0)),
            pl.BlockSpec((1, hw), lambda b_, h: (0, 0)),
            col(2 * D_MODEL), col(2 * D_MODEL + ATTN_W), col(2 * D_MODEL + 2 * ATTN_W),
        ],
        out_specs=blk,
        out_shape=jax.ShapeDtypeStruct((b, seq, ATTN_W), BF16),
        scratch_shapes=[pltpu.VMEM((hw, seq), BF16),
                        pltpu.VMEM((2, 2, tq, tk), F32), pltpu.VMEM((2, 2, tq, 1), F32),
                        pltpu.VMEM((2, tq, 1), F32), pltpu.VMEM((2, tq, 2 * hw), F32)],
        compiler_params=_params("parallel", "arbitrary"),
        name="diff_attn",
    )(dl, sub, proj3, proj3, proj3)


def _slot_swap_matrix():
    src = jnp.arange(8 * LANES).reshape(8, 8, GROUP_SIZE).transpose(1, 0, 2).reshape(-1)
    return (src[None, :] == jnp.arange(8 * LANES)[:, None]).astype(BF16)


def _ssm_kernel(u_ref, perm_ref, wi_ref, wb_ref, wc_ref, ap_ref, a8_ref, pt_ref, z_ref,
                ut_ref, s_ref, zt_ref, e_ref, hc_ref):
    nc = u_ref.shape[0]
    n_tiles = nc // 8
    gw = CHUNK * GROUP_SIZE
    pw = 2 * gw

    for th in range(CHUNK // 8):
        by_token = jnp.concatenate([u_ref[:, th * 8 + tl, :] for tl in range(8)], axis=1)
        by_group = jnp.dot(by_token.astype(BF16), perm_ref[...], preferred_element_type=F32)
        for g in range(GROUP_BLOCK):
            ut_ref[:, g * gw + th * LANES:g * gw + (th + 1) * LANES] = (
                by_group[:, g * LANES:(g + 1) * LANES].astype(BF16))

    for q in range(GROUP_BLOCK // 2):
        cols = slice(q * pw, (q + 1) * pw)
        s_ref[:, cols] = jnp.dot(ut_ref[:, cols], wb_ref[0, q], preferred_element_type=F32)

    row = lax.broadcasted_iota(jnp.int32, (8, LANES), 0)

    def shift_down(x, k):
        return jnp.where(row >= k, pltpu.roll(x, k, axis=0), 0.0)

    def shift_up(x, k):
        return jnp.where(row < 8 - k, pltpu.roll(x, 8 - k, axis=0), 0.0)

    n_chain = 2 * (GROUP_BLOCK // 2)

    def chain_cols(q, d):
        base = q * pw + d * 2 * LANES
        return slice(base, base + LANES), slice(base + LANES, base + 2 * LANES)

    def local_scan(i, _):
        rs = pl.ds(pl.multiple_of(i * 8, 8), 8)
        for q in range(GROUP_BLOCK // 2):
            for d in range(2):
                c_re, c_im = chain_cols(q, d)
                shift = shift_down if d == 0 else shift_up
                ch = q * 2 + d
                x_re, x_im = s_ref[rs, c_re], s_ref[rs, c_im]
                for ki, k in enumerate((1, 2, 4)):
                    a_re, a_im = ap_ref[0, (2 * ch) * 3 + ki], ap_ref[0, (2 * ch + 1) * 3 + ki]
                    y_re, y_im = shift(x_re, k), shift(x_im, k)
                    x_re, x_im = (x_re + a_re * y_re - a_im * y_im,
                                  x_im + a_re * y_im + a_im * y_re)
                s_ref[rs, c_re] = shift(x_re, 1)
                s_ref[rs, c_im] = shift(x_im, 1)
                pos = i if d == 0 else n_tiles - 1 - i
                edge = slice(7, 8) if d == 0 else slice(0, 1)
                e_ref[pos, ch:ch + 1, :] = x_re[edge]
                e_ref[pos, n_chain + ch:n_chain + ch + 1, :] = x_im[edge]
        return 0

    lax.fori_loop(0, n_tiles, local_scan, 0)

    a8_re, a8_im = a8_ref[0, :n_chain], a8_ref[0, n_chain:]

    def carry_in(i, h):
        h_re, h_im = h
        hc_ref[i, :n_chain, :] = h_re
        hc_ref[i, n_chain:, :] = h_im
        e_re, e_im = e_ref[i, :n_chain, :], e_ref[i, n_chain:, :]
        return a8_re * h_re - a8_im * h_im + e_re, a8_re * h_im + a8_im * h_re + e_im

    zero = jnp.zeros((n_chain, LANES), F32)
    lax.fori_loop(0, n_tiles, carry_in, (zero, zero))

    def add_carry(i, _):
        rs = pl.ds(pl.multiple_of(i * 8, 8), 8)
        for q in range(GROUP_BLOCK // 2):
            for d in range(2):
                c_re, c_im = chain_cols(q, d)
                ch = q * 2 + d
                pos = i if d == 0 else n_tiles - 1 - i
                h_re = jnp.broadcast_to(hc_ref[pos, ch:ch + 1, :], (8, LANES))
                h_im = jnp.broadcast_to(hc_ref[pos, n_chain + ch:n_chain + ch + 1, :], (8, LANES))
                p_re, p_im = pt_ref[0, 2 * ch], pt_ref[0, 2 * ch + 1]
                s_ref[rs, c_re] += p_re * h_re - p_im * h_im
                s_ref[rs, c_im] += p_re * h_im + p_im * h_re
        return 0

    lax.fori_loop(0, n_tiles, add_carry, 0)

    for q in range(GROUP_BLOCK // 2):
        cols = slice(q * pw, (q + 1) * pw)
        carried = jnp.dot(s_ref[:, cols].astype(BF16), wc_ref[0, q], preferred_element_type=F32)
        for g2 in range(2):
            g = 2 * q + g2
            gc = slice(g * gw, (g + 1) * gw)
            y = carried[:, g2 * gw:(g2 + 1) * gw] + jnp.dot(
                ut_ref[:, gc], wi_ref[0, g], preferred_element_type=F32)
            z = 0.5 * y * (1.0 + jnp.tanh(0.7978845608028654 * (y + 0.044715 * (y * y * y))))
            ut_ref[:, gc] = z.astype(BF16)

    for th in range(CHUNK // 8):
        by_group = jnp.concatenate(
            [ut_ref[:, g * gw + th * LANES:g * gw + (th + 1) * LANES] for g in range(GROUP_BLOCK)], axis=1)
        by_token = jnp.dot(by_group, perm_ref[...], preferred_element_type=F32)
        for tl in range(8):
            zt_ref[:, th * 8 + tl, :] = by_token[:, tl * LANES:(tl + 1) * LANES]
    z_ref[...] = zt_ref[...].astype(BF16)


def _ssm(u3, b, wi, wb, wc, ap, a8, pt):
    nc = u3.shape[0] // b
    cw = GROUP_BLOCK * CHUNK * GROUP_SIZE
    blk = pl.BlockSpec((nc, CHUNK, LANES), lambda g, b_: (b_, 0, g))
    whole = lambda a: pl.BlockSpec((1,) + a.shape[1:], lambda g, b_: (g,) + (0,) * (a.ndim - 1))
    carry = pltpu.VMEM((nc // 8, 2 * GROUP_BLOCK, LANES), F32)
    perm = _slot_swap_matrix()
    return pl.pallas_call(
        _ssm_kernel,
        grid=(N_GROUPS // GROUP_BLOCK, b),
        in_specs=[blk, pl.BlockSpec(perm.shape, lambda g, b_: (0, 0)),
                  whole(wi), whole(wb), whole(wc), whole(ap), whole(a8), whole(pt)],
        out_specs=blk,
        out_shape=jax.ShapeDtypeStruct(u3.shape, BF16),
        scratch_shapes=[pltpu.VMEM((nc, cw), BF16), pltpu.VMEM((nc, cw), F32),
                        pltpu.VMEM((nc, CHUNK, LANES), F32), carry, carry],
        compiler_params=_params("parallel", "arbitrary"),
        name="s5_scan",
    )(u3, perm, wi, wb, wc, ap, a8, pt)


def _ssm_operators(lam_re, lam_im, log_dt, b_re, b_im, c_re, c_im, d_skip):
    t_ = CHUNK
    g_, p_, c_ = N_GROUPS, STATE, GROUP_SIZE
    nb, npair = g_ // GROUP_BLOCK, g_ // 2
    lr, li = lam_re.astype(F32), lam_im.astype(F32)
    dt = jnp.exp(log_dt.astype(F32))[..., None]
    mag = jnp.exp(lr * dt)
    ab_re, ab_im = mag * jnp.cos(li * dt), mag * jnp.sin(li * dt)
    den = lr * lr + li * li
    n_re, n_im = ab_re - 1.0, ab_im
    k_re = (n_re * lr + n_im * li) / den
    k_im = (n_im * lr - n_re * li) / den
    br, bi = b_re.astype(F32), b_im.astype(F32)
    bb_re = k_re[..., None] * br - k_im[..., None] * bi
    bb_im = k_re[..., None] * bi + k_im[..., None] * br
    cr, ci = c_re.astype(F32), c_im.astype(F32)

    x_re, x_im = lr * dt, li * dt
    gw = t_ * c_

    def cpow(e, xr, xi):
        m = jnp.exp(e * xr)
        return m * jnp.cos(e * xi), m * jnp.sin(e * xi)

    swap = lambda a: a.transpose(0, 1, 3, 2)
    tok = jnp.repeat(jnp.arange(t_, dtype=F32), c_)
    first = (jnp.arange(2) == 0)[None, None, :, None, None]

    pw_r, pw_i = cpow(tok, x_re[..., None], x_im[..., None])
    rep = jnp.tile(jnp.eye(c_, dtype=F32), (1, t_))
    crt = jnp.einsum('dgop,on->dgpn', cr, rep, precision=HI)
    cit = jnp.einsum('dgop,on->dgpn', ci, rep, precision=HI)
    lhs = jnp.concatenate([crt * pw_r - cit * pw_i, -(crt * pw_i + cit * pw_r)], axis=2)
    rhs = jnp.concatenate([swap(bb_re), swap(bb_im)], axis=3)
    kern = jnp.einsum('dgck,dgkn->dgcn', rhs, lhs, precision=HI)
    fwd, bwd = kern[0], kern[1]
    bwd_rev = bwd.reshape(g_, c_, t_, c_)[:, :, ::-1].reshape(g_, c_, gw)
    lagged = jnp.concatenate([bwd_rev[..., :gw - c_], fwd[..., :c_] + bwd[..., :c_], fwd[..., c_:]], axis=-1)
    n_lag = lagged.shape[-1]
    start = (t_ - 1 - jnp.arange(t_)) * c_
    window = (jnp.arange(n_lag)[None, :, None] == start[:, None, None] + jnp.arange(gw)[None, None, :])
    w_intra = jnp.einsum('gck,tkn->gtcn', lagged.astype(BF16), window.astype(BF16),
                         preferred_element_type=F32)
    skip = jnp.eye(gw, dtype=F32) * jnp.tile(d_skip.astype(F32).reshape(g_, c_), (1, t_))[:, :, None]
    w_intra = (w_intra.reshape(g_, gw, gw) + skip).reshape(nb, GROUP_BLOCK, gw, gw)

    def pair_rows(bb):
        x = jnp.tile(swap(bb), (1, 1, t_, 1)).reshape(2, npair, 2, gw, p_)
        return jnp.concatenate([jnp.where(first, x, 0.0), jnp.where(first, 0.0, x)], axis=-1).reshape(
            2, npair, 2 * gw, 2 * p_)

    tok2 = jnp.tile(tok, 2)
    e_rows = jnp.stack([t_ - 1 - tok2, tok2])[:, None, :, None]
    pr, pi = cpow(e_rows, x_re.reshape(2, npair, 1, 2 * p_), x_im.reshape(2, npair, 1, 2 * p_))
    bt_re, bt_im = pair_rows(bb_re), pair_rows(bb_im)
    wb_re, wb_im = pr * bt_re - pi * bt_im, pr * bt_im + pi * bt_re
    wb = jnp.concatenate([wb_re[0], wb_im[0], wb_re[1], wb_im[1]], axis=-1)
    wb = wb.reshape(nb, GROUP_BLOCK // 2, 2 * gw, 8 * p_)

    def pair_cols(cc_tiled):
        x = cc_tiled.reshape(2, npair, 2, p_, gw)
        return jnp.concatenate([jnp.where(first, x, 0.0), jnp.where(first, 0.0, x)], axis=-1).reshape(
            2, npair, 2 * p_, 2 * gw)

    e_cols = jnp.stack([tok2 + 1, t_ - tok2])[:, None, None, :]
    qr, qi = cpow(e_cols, x_re.reshape(2, npair, 2 * p_, 1), x_im.reshape(2, npair, 2 * p_, 1))
    ct_re, ct_im = pair_cols(crt), pair_cols(cit)
    wc_re, wc_im = ct_re * qr - ct_im * qi, -(ct_re * qi + ct_im * qr)
    wc = jnp.concatenate([wc_re[0], wc_im[0], wc_re[1], wc_im[1]], axis=1)
    wc = wc.reshape(nb, GROUP_BLOCK // 2, 8 * p_, 2 * gw)

    def table(e):
        re, im = cpow(e, x_re, x_im)
        tab = jnp.stack([re, im], axis=2)
        k = tab.shape[0]
        tab = tab.reshape(k, 2, 2, nb, GROUP_BLOCK // 2, 2 * p_).transpose(3, 4, 1, 2, 0, 5)
        return tab.reshape(nb, (GROUP_BLOCK // 2) * 4, k, 2 * p_)

    step = jnp.asarray([1.0, 2.0, 4.0], F32) * t_
    ap = table(jnp.broadcast_to(step[:, None, None, None], (3, 2, 1, 1)))
    ap = jnp.broadcast_to(ap[:, :, :, None, :], ap.shape[:3] + (8, 2 * p_)).reshape(nb, -1, 8, 2 * p_)
    a8 = table(jnp.full((1, 2, 1, 1), 8.0 * t_, F32))
    a8 = a8.reshape(nb, GROUP_BLOCK // 2, 2, 2, 2 * p_).transpose(0, 3, 1, 2, 4).reshape(nb, -1, 2 * p_)
    j8 = jnp.arange(8, dtype=F32) * t_
    pt = table(jnp.stack([j8, j8[::-1]], axis=1)[:, :, None, None])
    return w_intra.astype(BF16), wb.astype(BF16), wc.astype(BF16), ap, a8, pt


def _resident(a):
    return pl.BlockSpec(a.shape, lambda *_: (0,) * a.ndim, pipeline_mode=pl.Buffered(1))


def _merge_kernel(x_ref, att_ref, z_ref, gate_ref, wp_ref, wa_ref, wb_ref, wo_ref, o_ref, mix_ref,
                  *, tn):
    att = att_ref[...]
    z = z_ref[...]
    for j in range(D_MODEL // tn):
        cs = slice(j * tn, (j + 1) * tn)
        gs = slice(D_MODEL + j * tn, D_MODEL + (j + 1) * tn)
        y_attn = jnp.dot(att, wp_ref[:, cs], preferred_element_type=F32)
        y_ssm = (jnp.dot(z, wa_ref[:, cs], preferred_element_type=F32)
                 * jax.nn.sigmoid(jnp.dot(z, wb_ref[:, cs], preferred_element_type=F32)))
        g_a = jax.nn.sigmoid(gate_ref[:, cs].astype(F32))
        g_s = jax.nn.sigmoid(gate_ref[:, gs].astype(F32))
        mix_ref[:, cs] = (g_a * y_attn + g_s * y_ssm).astype(BF16)
    o_ref[...] = x_ref[...] + jnp.dot(mix_ref[...], wo_ref[...], preferred_element_type=F32)


def _merge(x2, att, z, gates, wp, wa, wb, wo):
    n = att.shape[0]
    tm = min(256, n)
    row = lambda cols: pl.BlockSpec((tm, cols), lambda i: (i, 0))
    return pl.pallas_call(
        functools.partial(_merge_kernel, tn=512),
        grid=(n // tm,),
        in_specs=[row(D_MODEL), row(ATTN_W), row(SSM_W), row(2 * D_MODEL),
                  _resident(wp), _resident(wa), _resident(wb), _resident(wo)],
        out_specs=row(D_MODEL),
        out_shape=jax.ShapeDtypeStruct((n, D_MODEL), F32),
        scratch_shapes=[pltpu.VMEM((tm, D_MODEL), BF16)],
        compiler_params=_params("parallel"),
        name="merge_mix_out",
    )(x2, att, z, gates, wp, wa, wb, wo)


def _norm_mm_kernel(x_ref, g_ref, w_ref, o_ref, xn_ref, *, scale):
    @pl.when(pl.program_id(1) == 0)
    def _():
        xn_ref[...] = _rms(x_ref[...], g_ref[...]).astype(BF16)

    acc = jnp.dot(xn_ref[...], w_ref[...], preferred_element_type=F32)
    o_ref[...] = (acc * scale).astype(BF16)


def _norm_mm(x, g, w, scale):
    n, kdim = x.shape
    cols = w.shape[1]
    tm = min(512, n)
    tn = 1024
    return pl.pallas_call(
        functools.partial(_norm_mm_kernel, scale=scale),
        grid=(n // tm, cols // tn),
        in_specs=[
            pl.BlockSpec((tm, kdim), lambda i, j: (i, 0)),
            pl.BlockSpec((1, kdim), lambda i, j: (0, 0)),
            pl.BlockSpec((kdim, tn), lambda i, j: (0, j)),
        ],
        out_specs=pl.BlockSpec((tm, tn), lambda i, j: (i, j)),
        out_shape=jax.ShapeDtypeStruct((n, cols), BF16),
        scratch_shapes=[pltpu.VMEM((tm, kdim), BF16)],
        compiler_params=_params("parallel", "arbitrary"),
        name="norm_matmul",
    )(x, g, w)


def _cross_kernel(x_ref, g_ref, wq_ref, kv_ref, wo_ref, o_ref, q_ref, oc_ref):
    x = x_ref[0]
    hn = _rms(x, g_ref[...]).astype(BF16)
    q_ref[...] = (jnp.dot(hn, wq_ref[...], preferred_element_type=F32)
                  * X_HEAD_DIM ** -0.5).astype(BF16)
    nt = (((1,), (1,)), ((), ()))
    for h in range(X_HEADS):
        cols = slice(h * X_HEAD_DIM, (h + 1) * X_HEAD_DIM)
        vcols = slice(D_MODEL + h * X_HEAD_DIM, D_MODEL + (h + 1) * X_HEAD_DIM)
        s = lax.dot_general(q_ref[:, cols], kv_ref[0, :, cols], nt, preferred_element_type=F32)
        p = jnp.exp(s - jnp.max(s, axis=-1, keepdims=True))
        l = jnp.sum(p, axis=-1, keepdims=True)
        o = jnp.dot(p.astype(BF16), kv_ref[0, :, vcols], preferred_element_type=F32)
        oc_ref[:, cols] = (o / l).astype(BF16)
    o_ref[0] = x + jnp.dot(oc_ref[...], wo_ref[...], preferred_element_type=F32)


def _cross(x3, g, wq, kv3, wo):
    b, seq, _ = x3.shape
    m = kv3.shape[1]
    tq = min(512, seq)
    blk = pl.BlockSpec((1, tq, D_MODEL), lambda b_, i: (b_, i, 0))
    return pl.pallas_call(
        _cross_kernel,
        grid=(b, seq // tq),
        in_specs=[blk, _resident(g), _resident(wq),
                  pl.BlockSpec((1, m, 2 * D_MODEL), lambda b_, i: (b_, 0, 0)), _resident(wo)],
        out_specs=blk,
        out_shape=jax.ShapeDtypeStruct(x3.shape, F32),
        scratch_shapes=[pltpu.VMEM((tq, D_MODEL), BF16), pltpu.VMEM((tq, D_MODEL), BF16)],
        compiler_params=_params("parallel", "arbitrary"),
        name="cross_attn_block",
    )(x3, g, wq, kv3, wo)


def _mlp_kernel(x_ref, g_ref, wu_ref, wd_ref, gf_ref, o_ref, hn_ref, acc_ref):
    f = pl.program_id(1)

    @pl.when(f == 0)
    def _():
        hn_ref[...] = _rms(x_ref[...], g_ref[...]).astype(BF16)
        acc_ref[...] = jnp.zeros_like(acc_ref)

    h = jnp.maximum(jnp.dot(hn_ref[...], wu_ref[...], preferred_element_type=F32), 0.0)
    acc_ref[...] += jnp.dot((h * h).astype(BF16), wd_ref[...], preferred_element_type=F32)

    @pl.when(f == pl.num_programs(1) - 1)
    def _():
        o_ref[...] = _rms(x_ref[...] + acc_ref[...], gf_ref[...])


def _mlp(x, g, wu, wd, gf):
    n = x.shape[0]
    tm = min(512, n)
    tf = 1024
    return pl.pallas_call(
        _mlp_kernel,
        grid=(n // tm, D_FF // tf),
        in_specs=[
            pl.BlockSpec((tm, D_MODEL), lambda i, f: (i, 0)),
            pl.BlockSpec((1, D_MODEL), lambda i, f: (0, 0)),
            pl.BlockSpec((D_MODEL, tf), lambda i, f: (0, f)),
            pl.BlockSpec((tf, D_MODEL), lambda i, f: (f, 0)),
            pl.BlockSpec((1, D_MODEL), lambda i, f: (0, 0)),
        ],
        out_specs=pl.BlockSpec((tm, D_MODEL), lambda i, f: (i, 0)),
        out_shape=jax.ShapeDtypeStruct((n, D_MODEL), F32),
        scratch_shapes=[pltpu.VMEM((tm, D_MODEL), BF16), pltpu.VMEM((tm, D_MODEL), F32)],
        compiler_params=_params("parallel", "arbitrary"),
        name="mlp_final_norm",
    )(x, g, wu, wd, gf)


def _rope_tables(seq):
    half = HEAD_DIM // 2
    inv = ROPE_THETA ** (-jnp.arange(0, HEAD_DIM, 2, dtype=F32) / HEAD_DIM)
    ang = jnp.arange(seq, dtype=F32)[:, None] * inv[None, :]
    cos, sin = jnp.cos(ang), jnp.sin(ang)
    cos_t = jnp.tile(cos, (1, LANES // half))
    sin_t = jnp.concatenate([-sin, -sin, sin, sin], axis=1)
    return (jnp.stack([cos_t * Q_SCALE, cos_t, jnp.ones_like(cos_t)]),
            jnp.stack([sin_t * Q_SCALE, sin_t, jnp.zeros_like(sin_t)]))


def _qk_shuffle():
    half = HEAD_DIM // 2
    src = jnp.arange(2 * ATTN_W).reshape(2 * N_HEADS, 2, 2, half).transpose(0, 2, 1, 3).reshape(-1)
    return (src[None, :] == jnp.arange(2 * ATTN_W)[:, None]).astype(BF16)


def _prepare(w):
    (norm_mix, w_in, diff_lambda, subln, w_attn_proj,
     lam_re, lam_im, log_dt, b_re, b_im, c_re, c_im, d_skip,
     w_glu_a, w_glu_b, w_mix_out,
     norm_cross, norm_mem, w_q_cross, w_kv_cross, w_o_cross,
     norm_mlp, w_mlp_up, w_mlp_down, norm_final) = w
    w_in0 = w_in[0].astype(BF16)
    w_qk = jnp.dot(w_in0[:, :2 * ATTN_W], _qk_shuffle(), preferred_element_type=BF16)
    w_in_p = jnp.concatenate([w_qk, w_in0[:, 2 * ATTN_W:]], axis=1)
    row = lambda v: v.astype(F32).reshape(1, -1)
    return dict(
        norm_mix=row(norm_mix[0]), w_in=w_in_p, diff_lambda=diff_lambda[0].astype(F32),
        subln=row(subln[0]), w_attn_proj=w_attn_proj[0].astype(BF16),
        ssm=_ssm_operators(lam_re[0], lam_im[0], log_dt[0], b_re[0], b_im[0], c_re[0], c_im[0],
                           d_skip[0]),
        w_glu_a=w_glu_a[0].astype(BF16), w_glu_b=w_glu_b[0].astype(BF16),
        w_mix_out=w_mix_out[0].astype(BF16),
        norm_cross=row(norm_cross[0]), norm_mem=row(norm_mem[0]),
        w_q_cross=w_q_cross[0].astype(BF16), w_kv_cross=w_kv_cross[0].astype(BF16),
        w_o_cross=w_o_cross[0].astype(BF16),
        norm_mlp=row(norm_mlp[0]), w_mlp_up=w_mlp_up[0].astype(BF16),
        w_mlp_down=w_mlp_down[0].astype(BF16), norm_final=row(norm_final),
    )


def _encode(x, mem, p):
    b, seq, _ = x.shape
    assert seq % (8 * CHUNK) == 0
    n = b * seq
    nc = seq // CHUNK
    x2 = x.reshape(n, D_MODEL)
    cos_t, sin_t = _rope_tables(seq)

    proj, u = _in_proj(x2, p["norm_mix"], p["w_in"], cos_t, sin_t, seq)
    att = _diff_attn(proj.reshape(b, seq, -1), p["diff_lambda"], p["subln"])

    z = _ssm(u.reshape(b * nc, CHUNK, SSM_W), b, *p["ssm"]).reshape(n, SSM_W)

    x2 = _merge(x2, att.reshape(n, ATTN_W), z, proj, p["w_attn_proj"], p["w_glu_a"], p["w_glu_b"],
                p["w_mix_out"])

    m_tok = mem.shape[1]
    kv = _norm_mm(mem.reshape(b * m_tok, D_MODEL), p["norm_mem"], p["w_kv_cross"], 1.0)
    x2 = _cross(x2.reshape(b, seq, D_MODEL), p["norm_cross"], p["w_q_cross"],
                kv.reshape(b, m_tok, 2 * D_MODEL), p["w_o_cross"]).reshape(n, D_MODEL)

    out = _mlp(x2, p["norm_mlp"], p["w_mlp_up"], p["w_mlp_down"], p["norm_final"])
    return out.reshape(b, seq, D_MODEL)


def kernel(x_prompt, x_sample, mem_prompt, mem_sample, norm_mix, w_in, diff_lambda, subln, w_attn_proj, ssm_lambda_re, ssm_lambda_im, ssm_log_dt, ssm_b_re, ssm_b_im, ssm_c_re, ssm_c_im, ssm_d, w_glu_a, w_glu_b, w_mix_out, norm_cross, norm_mem, w_q_cross, w_kv_cross, w_o_cross, norm_mlp, w_mlp_up, w_mlp_down, norm_final):
    p = _prepare((norm_mix, w_in, diff_lambda, subln, w_attn_proj,
                  ssm_lambda_re, ssm_lambda_im, ssm_log_dt, ssm_b_re, ssm_b_im, ssm_c_re, ssm_c_im,
                  ssm_d, w_glu_a, w_glu_b, w_mix_out,
                  norm_cross, norm_mem, w_q_cross, w_kv_cross, w_o_cross,
                  norm_mlp, w_mlp_up, w_mlp_down, norm_final))
    return (_encode(x_prompt, mem_prompt, p), _encode(x_sample, mem_sample, p))
```

```python
import functools
import math

import jax
import jax.numpy as jnp
from jax import lax
from jax.experimental import pallas as pl
from jax.experimental.pallas import tpu as pltpu

D_MODEL = 2048
N_HEADS = 8
HEAD_DIM = 64
ATTN_W = N_HEADS * 2 * HEAD_DIM
SSM_W = D_MODEL // 2
GROUP_SIZE = 16
N_GROUPS = SSM_W // GROUP_SIZE
STATE = 64
IN_COLS = 3 * ATTN_W + SSM_W + 2 * D_MODEL
D_FF = 4 * D_MODEL
X_HEADS = 4
X_HEAD_DIM = D_MODEL // X_HEADS
ROPE_THETA = 10000.0
EPS = 1e-6
LAM_INIT = 0.8 - 0.6 * math.exp(-0.3 * 0)
Q_SCALE = HEAD_DIM ** -0.5 * math.log2(math.e)

CHUNK = 16
GROUP_BLOCK = 8
LANES = 128
VMEM_LIMIT = 56 * 1024 * 1024

F32 = jnp.float32
BF16 = jnp.bfloat16
HI = lax.Precision.HIGHEST


def _params(*sem):
    return pltpu.CompilerParams(dimension_semantics=sem, vmem_limit_bytes=VMEM_LIMIT)


def _rms(x, g):
    return x * lax.rsqrt(jnp.mean(x * x, axis=-1, keepdims=True) + EPS) * g


def _in_proj_kernel(x_ref, g_ref, w_ref, cos_ref, sin_ref, o_ref, u_ref, xn_ref, acc_a, acc_b):
    j = pl.program_id(1)

    @pl.when(j == 0)
    def _():
        xn_ref[...] = _rms(x_ref[...], g_ref[...]).astype(BF16)

    def project(cur):
        cur[...] = jnp.dot(xn_ref[...], w_ref[...], preferred_element_type=F32)

    def finish(prev):
        cos = cos_ref[0]
        sin = sin_ref[0]
        for c in range(prev.shape[1] // LANES):
            xc = prev[:, c * LANES:(c + 1) * LANES]
            o_ref[:, c * LANES:(c + 1) * LANES] = (
                xc * cos + pltpu.roll(xc, LANES // 2, axis=1) * sin).astype(BF16)

    @pl.when(j == 0)
    def _():
        project(acc_a)

    @pl.when((j > 0) & (j % 2 == 0))
    def _():
        finish(acc_b)
        project(acc_a)

    @pl.when(j % 2 == 1)
    def _():
        finish(acc_a)
        project(acc_b)

    @pl.when(j == pl.num_programs(1) - 1)
    def _():
        u_ref[...] = acc_b[...]


def _in_proj(x2, g, w, cos_t, sin_t, seq):
    n = x2.shape[0]
    tm = min(1024, seq)
    tn = ATTN_W
    nj = IN_COLS // tn
    assert nj % 2 == 0
    pos_blocks = seq // tm
    table = lambda i, j: (jnp.where(j == 5, 0, jnp.where(j == 6, 1, 2)), i % pos_blocks, 0)
    return pl.pallas_call(
        _in_proj_kernel,
        grid=(n // tm, nj),
        in_specs=[
            pl.BlockSpec((tm, D_MODEL), lambda i, j: (i, 0)),
            pl.BlockSpec((1, D_MODEL), lambda i, j: (0, 0)),
            pl.BlockSpec((D_MODEL, tn), lambda i, j: (0, (j + nj // 2) % nj)),
            pl.BlockSpec((1, tm, LANES), table),
            pl.BlockSpec((1, tm, LANES), table),
        ],
        out_specs=[
            pl.BlockSpec((tm, tn), lambda i, j: (i, jnp.maximum(j - 1, 0))),
            pl.BlockSpec((tm, tn), lambda i, j: (i, 0)),
        ],
        out_shape=[jax.ShapeDtypeStruct((n, (nj - 1) * tn), BF16),
                   jax.ShapeDtypeStruct((n, SSM_W), F32)],
        scratch_shapes=[pltpu.VMEM((tm, D_MODEL), BF16), pltpu.VMEM((tm, tn), F32),
                        pltpu.VMEM((tm, tn), F32)],
        compiler_params=_params("parallel", "arbitrary"),
        name="in_proj",
    )(x2, g, w, cos_t, sin_t)


def _diff_attn_kernel(dl_ref, sub_ref, q_ref, k_ref, v_ref, o_ref, kt_ref, s_ref, mx_ref, m_ref, acc_ref,
                      *, tq, tk, unroll):
    seq = v_ref.shape[1]
    nk = seq // tk
    steps = (seq // tq) * nk
    hw = 2 * HEAD_DIM
    lane = lax.broadcasted_iota(jnp.int32, (tq, hw), 1)
    first = ((lane // (HEAD_DIM // 2)) % 2) == 0
    ones = jnp.ones((tk, hw), BF16)
    dl = dl_ref[...]
    lam = (jnp.exp(jnp.sum(dl[0:1] * dl[1:2], axis=-1, keepdims=True))
           - jnp.exp(jnp.sum(dl[2:3] * dl[3:4], axis=-1, keepdims=True)) + LAM_INIT)

    def scores(f, slot):
        f = jnp.minimum(f, steps - 1)
        q = q_ref[0, pl.ds(pl.multiple_of((f // nk) * tq, tq), tq), :]
        kc = kt_ref[:, pl.ds(pl.multiple_of((f % nk) * tk, tk), tk)]
        zero = jnp.zeros_like(q)
        for comp, qm in enumerate((jnp.where(first, q, zero), jnp.where(first, zero, q))):
            s = jnp.dot(qm, kc, preferred_element_type=F32)
            s_ref[slot, comp] = s
            mx_ref[slot, comp] = jnp.max(s, axis=-1, keepdims=True)

    def accumulate(f, slot, emit):
        c = f % nk
        vx = jnp.concatenate([v_ref[0, pl.ds(pl.multiple_of(c * tk, tk), tk), :], ones], axis=1)
        acc = []
        for comp in range(2):
            m_old = jnp.where(c == 0, -jnp.inf, m_ref[comp])
            m_new = jnp.maximum(m_old, mx_ref[slot, comp])
            m_ref[comp] = m_new
            p = jnp.exp2(s_ref[slot, comp] - m_new).astype(BF16)
            acc.append(jnp.exp2(m_old - m_new) * acc_ref[comp]
                       + jnp.dot(p, vx, preferred_element_type=F32))
            acc_ref[comp] = acc[comp]
        if emit:
            o = acc[0][:, :hw] / acc[0][:, hw:] - lam * (acc[1][:, :hw] / acc[1][:, hw:])
            rows = pl.ds(pl.multiple_of((f // nk) * tq, tq), tq)
            o_ref[0, rows, :] = (_rms(o, sub_ref[...]) * (1.0 - LAM_INIT)).astype(BF16)

    for r in range(0, seq, tq):
        kt_ref[:, r:r + tq] = k_ref[0, r:r + tq, :].T

    m_ref[...] = jnp.full(m_ref.shape, -jnp.inf, F32)
    acc_ref[...] = jnp.zeros(acc_ref.shape, F32)
    scores(0, 0)

    def body(i, _):
        for k in range(unroll):
            f = i * unroll + k
            scores(f + 1, (k + 1) % 2)
            accumulate(f, k % 2, emit=(k + 1) % min(unroll, nk) == 0)
        return 0

    lax.fori_loop(0, steps // unroll, body, 0)


def _diff_attn(proj3, dl, sub):
    b, seq, _ = proj3.shape
    tq = min(256, seq)
    tk = seq if seq <= 2048 else min(4096, seq // 2)
    nk = seq // tk
    steps = (seq // tq) * nk
    unroll = 4 if steps % 4 == 0 else (2 if steps % 2 == 0 else 1)
    assert nk % unroll == 0 or unroll % nk == 0
    hw = 2 * HEAD_DIM
    blk = pl.BlockSpec((1, seq, hw), lambda b_, h: (b_, 0, h))
    col = lambda first: pl.BlockSpec((1, seq, hw), lambda b_, h: (b_, 0, first // hw + h))
    return pl.pallas_call(
        functools.partial(_diff_attn_kernel, tq=tq, tk=tk, unroll=unroll),
        grid=(b, N_HEADS),
        in_specs=[
            pl.BlockSpec((4, HEAD_DIM), lambda b_, h: (0, 0)),
            pl.BlockSpec((1, hw), lambda b_, h: (0, 0)),
            col(2 * D_MODEL), col(2 * D_MODEL + ATTN_W), col(2 * D_MODEL + 2 * ATTN_W),
        ],
        out_specs=blk,
        out_shape=jax.ShapeDtypeStruct((b, seq, ATTN_W), BF16),
        scratch_shapes=[pltpu.VMEM((hw, seq), BF16),
                        pltpu.VMEM((2, 2, tq, tk), F32), pltpu.VMEM((2, 2, tq, 1), F32),
                        pltpu.VMEM((2, tq, 1), F32), pltpu.VMEM((2, tq, 2 * hw), F32)],
        compiler_params=_params("parallel", "arbitrary"),
        name="diff_attn",
    )(dl, sub, proj3, proj3, proj3)


def _slot_swap_matrix():
    src = jnp.arange(8 * LANES).reshape(8, 8, GROUP_SIZE).transpose(1, 0, 2).reshape(-1)
    return (src[None, :] == jnp.arange(8 * LANES)[:, None]).astype(BF16)


def _ssm_kernel(u_ref, perm_ref, wi_ref, wb_ref, wc_ref, ap_ref, a8_ref, pt_ref, z_ref,
                ut_ref, s_ref, zt_ref, e_ref, hc_ref):
    nc = u_ref.shape[0]
    n_tiles = nc // 8
    gw = CHUNK * GROUP_SIZE
    pw = 2 * gw

    for th in range(CHUNK // 8):
        by_token = jnp.concatenate([u_ref[:, th * 8 + tl, :] for tl in range(8)], axis=1)
        by_group = jnp.dot(by_token.astype(BF16), perm_ref[...], preferred_element_type=F32)
        for g in range(GROUP_BLOCK):
            ut_ref[:, g * gw + th * LANES:g * gw + (th + 1) * LANES] = (
                by_group[:, g * LANES:(g + 1) * LANES].astype(BF16))

    for q in range(GROUP_BLOCK // 2):
        cols = slice(q * pw, (q + 1) * pw)
        s_ref[:, cols] = jnp.dot(ut_ref[:, cols], wb_ref[0, q], preferred_element_type=F32)

    row = lax.broadcasted_iota(jnp.int32, (8, LANES), 0)

    def shift_down(x, k):
        return jnp.where(row >= k, pltpu.roll(x, k, axis=0), 0.0)

    def shift_up(x, k):
        return jnp.where(row < 8 - k, pltpu.roll(x, 8 - k, axis=0), 0.0)

    n_chain = 2 * (GROUP_BLOCK // 2)

    def chain_cols(q, d):
        base = q * pw + d * 2 * LANES
        return slice(base, base + LANES), slice(base + LANES, base + 2 * LANES)

    def local_scan(i, _):
        rs = pl.ds(pl.multiple_of(i * 8, 8), 8)
        for q in range(GROUP_BLOCK // 2):
            for d in range(2):
                c_re, c_im = chain_cols(q, d)
                shift = shift_down if d == 0 else shift_up
                ch = q * 2 + d
                x_re, x_im = s_ref[rs, c_re], s_ref[rs, c_im]
                for ki, k in enumerate((1, 2, 4)):
                    a_re, a_im = ap_ref[0, (2 * ch) * 3 + ki], ap_ref[0, (2 * ch + 1) * 3 + ki]
                    y_re, y_im = shift(x_re, k), shift(x_im, k)
                    x_re, x_im = (x_re + a_re * y_re - a_im * y_im,
                                  x_im + a_re * y_im + a_im * y_re)
                s_ref[rs, c_re] = shift(x_re, 1)
                s_ref[rs, c_im] = shift(x_im, 1)
                pos = i if d == 0 else n_tiles - 1 - i
                edge = slice(7, 8) if d == 0 else slice(0, 1)
                e_ref[pos, ch:ch + 1, :] = x_re[edge]
                e_ref[pos, n_chain + ch:n_chain + ch + 1, :] = x_im[edge]
        return 0

    lax.fori_loop(0, n_tiles, local_scan, 0)

    a8_re, a8_im = a8_ref[0, :n_chain], a8_ref[0, n_chain:]

    def carry_in(i, h):
        h_re, h_im = h
        hc_ref[i, :n_chain, :] = h_re
        hc_ref[i, n_chain:, :] = h_im
        e_re, e_im = e_ref[i, :n_chain, :], e_ref[i, n_chain:, :]
        return a8_re * h_re - a8_im * h_im + e_re, a8_re * h_im + a8_im * h_re + e_im

    zero = jnp.zeros((n_chain, LANES), F32)
    lax.fori_loop(0, n_tiles, carry_in, (zero, zero))

    def add_carry(i, _):
        rs = pl.ds(pl.multiple_of(i * 8, 8), 8)
        for q in range(GROUP_BLOCK // 2):
            for d in range(2):
                c_re, c_im = chain_cols(q, d)
                ch = q * 2 + d
                pos = i if d == 0 else n_tiles - 1 - i
                h_re = jnp.broadcast_to(hc_ref[pos, ch:ch + 1, :], (8, LANES))
                h_im = jnp.broadcast_to(hc_ref[pos, n_chain + ch:n_chain + ch + 1, :], (8, LANES))
                p_re, p_im = pt_ref[0, 2 * ch], pt_ref[0, 2 * ch + 1]
                s_ref[rs, c_re] += p_re * h_re - p_im * h_im
                s_ref[rs, c_im] += p_re * h_im + p_im * h_re
        return 0

    lax.fori_loop(0, n_tiles, add_carry, 0)

    for q in range(GROUP_BLOCK // 2):
        cols = slice(q * pw, (q + 1) * pw)
        carried = jnp.dot(s_ref[:, cols].astype(BF16), wc_ref[0, q], preferred_element_type=F32)
        for g2 in range(2):
            g = 2 * q + g2
            gc = slice(g * gw, (g + 1) * gw)
            y = carried[:, g2 * gw:(g2 + 1) * gw] + jnp.dot(
                ut_ref[:, gc], wi_ref[0, g], preferred_element_type=F32)
            z = 0.5 * y * (1.0 + jnp.tanh(0.7978845608028654 * (y + 0.044715 * (y * y * y))))
            ut_ref[:, gc] = z.astype(BF16)

    for th in range(CHUNK // 8):
        by_group = jnp.concatenate(
            [ut_ref[:, g * gw + th * LANES:g * gw + (th + 1) * LANES] for g in range(GROUP_BLOCK)], axis=1)
        by_token = jnp.dot(by_group, perm_ref[...], preferred_element_type=F32)
        for tl in range(8):
            zt_ref[:, th * 8 + tl, :] = by_token[:, tl * LANES:(tl + 1) * LANES]
    z_ref[...] = zt_ref[...].astype(BF16)


def _ssm(u3, b, wi, wb, wc, ap, a8, pt):
    nc = u3.shape[0] // b
    cw = GROUP_BLOCK * CHUNK * GROUP_SIZE
    blk = pl.BlockSpec((nc, CHUNK, LANES), lambda g, b_: (b_, 0, g))
    whole = lambda a: pl.BlockSpec((1,) + a.shape[1:], lambda g, b_: (g,) + (0,) * (a.ndim - 1))
    carry = pltpu.VMEM((nc // 8, 2 * GROUP_BLOCK, LANES), F32)
    perm = _slot_swap_matrix()
    return pl.pallas_call(
        _ssm_kernel,
        grid=(N_GROUPS // GROUP_BLOCK, b),
        in_specs=[blk, pl.BlockSpec(perm.shape, lambda g, b_: (0, 0)),
                  whole(wi), whole(wb), whole(wc), whole(ap), whole(a8), whole(pt)],
        out_specs=blk,
        out_shape=jax.ShapeDtypeStruct(u3.shape, BF16),
        scratch_shapes=[pltpu.VMEM((nc, cw), BF16), pltpu.VMEM((nc, cw), F32),
                        pltpu.VMEM((nc, CHUNK, LANES), F32), carry, carry],
        compiler_params=_params("parallel", "arbitrary"),
        name="s5_scan",
    )(u3, perm, wi, wb, wc, ap, a8, pt)


def _ssm_operators(lam_re, lam_im, log_dt, b_re, b_im, c_re, c_im, d_skip):
    t_ = CHUNK
    g_, p_, c_ = N_GROUPS, STATE, GROUP_SIZE
    nb, npair = g_ // GROUP_BLOCK, g_ // 2
    lr, li = lam_re.astype(F32), lam_im.astype(F32)
    dt = jnp.exp(log_dt.astype(F32))[..., None]
    mag = jnp.exp(lr * dt)
    ab_re, ab_im = mag * jnp.cos(li * dt), mag * jnp.sin(li * dt)
    den = lr * lr + li * li
    n_re, n_im = ab_re - 1.0, ab_im
    k_re = (n_re * lr + n_im * li) / den
    k_im = (n_im * lr - n_re * li) / den
    br, bi = b_re.astype(F32), b_im.astype(F32)
    bb_re = k_re[..., None] * br - k_im[..., None] * bi
    bb_im = k_re[..., None] * bi + k_im[..., None] * br
    cr, ci = c_re.astype(F32), c_im.astype(F32)

    x_re, x_im = lr * dt, li * dt
    gw = t_ * c_

    def cpow(e, xr, xi):
        m = jnp.exp(e * xr)
        return m * jnp.cos(e * xi), m * jnp.sin(e * xi)

    swap = lambda a: a.transpose(0, 1, 3, 2)
    tok = jnp.repeat(jnp.arange(t_, dtype=F32), c_)
    first = (jnp.arange(2) == 0)[None, None, :, None, None]

    pw_r, pw_i = cpow(tok, x_re[..., None], x_im[..., None])
    rep = jnp.tile(jnp.eye(c_, dtype=F32), (1, t_))
    crt = jnp.einsum('dgop,on->dgpn', cr, rep, precision=HI)
    cit = jnp.einsum('dgop,on->dgpn', ci, rep, precision=HI)
    lhs = jnp.concatenate([crt * pw_r - cit * pw_i, -(crt * pw_i + cit * pw_r)], axis=2)
    rhs = jnp.concatenate([swap(bb_re), swap(bb_im)], axis=3)
    kern = jnp.einsum('dgck,dgkn->dgcn', rhs, lhs, precision=lax.Precision.HIGH)
    fwd, bwd = kern[0], kern[1]
    bwd_rev = bwd.reshape(g_, c_, t_, c_)[:, :, ::-1].reshape(g_, c_, gw)
    lagged = jnp.concatenate([bwd_rev[..., :gw - c_], fwd[..., :c_] + bwd[..., :c_], fwd[..., c_:]], axis=-1)
    n_lag = lagged.shape[-1]
    start = (t_ - 1 - jnp.arange(t_)) * c_
    window = (jnp.arange(n_lag)[None, :, None] == start[:, None, None] + jnp.arange(gw)[None, None, :])
    w_intra = jnp.einsum('gck,tkn->gtcn', lagged.astype(BF16), window.astype(BF16),
                         preferred_element_type=F32)
    skip = jnp.eye(gw, dtype=F32) * jnp.tile(d_skip.astype(F32).reshape(g_, c_), (1, t_))[:, :, None]
    w_intra = (w_intra.reshape(g_, gw, gw) + skip).reshape(nb, GROUP_BLOCK, gw, gw)

    def pair_rows(bb):
        x = jnp.tile(swap(bb), (1, 1, t_, 1)).reshape(2, npair, 2, gw, p_)
        return jnp.concatenate([jnp.where(first, x, 0.0), jnp.where(first, 0.0, x)], axis=-1).reshape(
            2, npair, 2 * gw, 2 * p_)

    steps = jnp.arange(t_, dtype=F32)
    e_rows = jnp.stack([t_ - 1 - steps, steps])[:, None, :, None]
    pr, pi = cpow(e_rows, x_re.reshape(2, npair, 1, 2 * p_), x_im.reshape(2, npair, 1, 2 * p_))
    by_row = lambda a: jnp.broadcast_to(a[:, :, None, :, None, :], (2, npair, 2, t_, c_, 2 * p_)).reshape(
        2, npair, 2 * gw, 2 * p_)
    pr, pi = by_row(pr), by_row(pi)
    bt_re, bt_im = pair_rows(bb_re), pair_rows(bb_im)
    wb_re, wb_im = pr * bt_re - pi * bt_im, pr * bt_im + pi * bt_re
    wb = jnp.concatenate([wb_re[0], wb_im[0], wb_re[1], wb_im[1]], axis=-1)
    wb = wb.reshape(nb, GROUP_BLOCK // 2, 2 * gw, 8 * p_)

    def pair_cols(cc_tiled):
        x = cc_tiled.reshape(2, npair, 2, p_, gw)
        return jnp.concatenate([jnp.where(first, x, 0.0), jnp.where(first, 0.0, x)], axis=-1).reshape(
            2, npair, 2 * p_, 2 * gw)

    e_cols = jnp.stack([steps + 1, t_ - steps])[:, None, None, :]
    qr, qi = cpow(e_cols, x_re.reshape(2, npair, 2 * p_, 1), x_im.reshape(2, npair, 2 * p_, 1))
    spread = jnp.tile(jnp.repeat(jnp.eye(t_, dtype=F32), c_, axis=1), (1, 2))
    qr = jnp.einsum('dqrt,tn->dqrn', qr, spread, precision=HI)
    qi = jnp.einsum('dqrt,tn->dqrn', qi, spread, precision=HI)
    ct_re, ct_im = pair_cols(crt), pair_cols(cit)
    wc_re, wc_im = ct_re * qr - ct_im * qi, -(ct_re * qi + ct_im * qr)
    wc = jnp.concatenate([wc_re[0], wc_im[0], wc_re[1], wc_im[1]], axis=1)
    wc = wc.reshape(nb, GROUP_BLOCK // 2, 8 * p_, 2 * gw)

    def table(e):
        re, im = cpow(e, x_re, x_im)
        tab = jnp.stack([re, im], axis=2)
        k = tab.shape[0]
        tab = tab.reshape(k, 2, 2, nb, GROUP_BLOCK // 2, 2 * p_).transpose(3, 4, 1, 2, 0, 5)
        return tab.reshape(nb, (GROUP_BLOCK // 2) * 4, k, 2 * p_)

    step = jnp.asarray([1.0, 2.0, 4.0], F32) * t_
    ap = table(jnp.broadcast_to(step[:, None, None, None], (3, 2, 1, 1)))
    ap = jnp.broadcast_to(ap[:, :, :, None, :], ap.shape[:3] + (8, 2 * p_)).reshape(nb, -1, 8, 2 * p_)
    a8 = table(jnp.full((1, 2, 1, 1), 8.0 * t_, F32))
    a8 = a8.reshape(nb, GROUP_BLOCK // 2, 2, 2, 2 * p_).transpose(0, 3, 1, 2, 4).reshape(nb, -1, 2 * p_)
    j8 = jnp.arange(8, dtype=F32) * t_
    pt = table(jnp.stack([j8, j8[::-1]], axis=1)[:, :, None, None])
    return w_intra.astype(BF16), wb.astype(BF16), wc.astype(BF16), ap, a8, pt


def _resident(a):
    return pl.BlockSpec(a.shape, lambda *_: (0,) * a.ndim, pipeline_mode=pl.Buffered(1))


def _merge_kernel(x_ref, att_ref, z_ref, gate_ref, wp_ref, wa_ref, wb_ref, wo_ref, o_ref, mix_ref,
                  *, tn):
    att = att_ref[...]
    z = z_ref[...]
    for j in range(D_MODEL // tn):
        cs = slice(j * tn, (j + 1) * tn)
        gs = slice(D_MODEL + j * tn, D_MODEL + (j + 1) * tn)
        y_attn = jnp.dot(att, wp_ref[:, cs], preferred_element_type=F32)
        y_ssm = (jnp.dot(z, wa_ref[:, cs], preferred_element_type=F32)
                 * jax.nn.sigmoid(jnp.dot(z, wb_ref[:, cs], preferred_element_type=F32)))
        g_a = jax.nn.sigmoid(gate_ref[:, cs].astype(F32))
        g_s = jax.nn.sigmoid(gate_ref[:, gs].astype(F32))
        mix_ref[:, cs] = (g_a * y_attn + g_s * y_ssm).astype(BF16)
    o_ref[...] = x_ref[...] + jnp.dot(mix_ref[...], wo_ref[...], preferred_element_type=F32)


def _merge(x2, att, z, gates, wp, wa, wb, wo):
    n = att.shape[0]
    tm = min(256, n)
    row = lambda cols: pl.BlockSpec((tm, cols), lambda i: (i, 0))
    return pl.pallas_call(
        functools.partial(_merge_kernel, tn=512),
        grid=(n // tm,),
        in_specs=[row(D_MODEL), row(ATTN_W), row(SSM_W), row(2 * D_MODEL),
                  _resident(wp), _resident(wa), _resident(wb), _resident(wo)],
        out_specs=row(D_MODEL),
        out_shape=jax.ShapeDtypeStruct((n, D_MODEL), F32),
        scratch_shapes=[pltpu.VMEM((tm, D_MODEL), BF16)],
        compiler_params=_params("parallel"),
        name="merge_mix_out",
    )(x2, att, z, gates, wp, wa, wb, wo)


def _norm_mm_kernel(x_ref, g_ref, w_ref, o_ref, xn_ref, *, scale):
    @pl.when(pl.program_id(1) == 0)
    def _():
        xn_ref[...] = _rms(x_ref[...], g_ref[...]).astype(BF16)

    acc = jnp.dot(xn_ref[...], w_ref[...], preferred_element_type=F32)
    o_ref[...] = (acc * scale).astype(BF16)


def _norm_mm(x, g, w, scale):
    n, kdim = x.shape
    cols = w.shape[1]
    tm = min(512, n)
    tn = 1024
    return pl.pallas_call(
        functools.partial(_norm_mm_kernel, scale=scale),
        grid=(n // tm, cols // tn),
        in_specs=[
            pl.BlockSpec((tm, kdim), lambda i, j: (i, 0)),
            pl.BlockSpec((1, kdim), lambda i, j: (0, 0)),
            pl.BlockSpec((kdim, tn), lambda i, j: (0, j)),
        ],
        out_specs=pl.BlockSpec((tm, tn), lambda i, j: (i, j)),
        out_shape=jax.ShapeDtypeStruct((n, cols), BF16),
        scratch_shapes=[pltpu.VMEM((tm, kdim), BF16)],
        compiler_params=_params("parallel", "arbitrary"),
        name="norm_matmul",
    )(x, g, w)


def _cross_kernel(x_ref, g_ref, wq_ref, kv_ref, wo_ref, o_ref, q_ref, oc_ref):
    x = x_ref[0]
    hn = _rms(x, g_ref[...]).astype(BF16)
    q_ref[...] = (jnp.dot(hn, wq_ref[...], preferred_element_type=F32)
                  * X_HEAD_DIM ** -0.5).astype(BF16)
    nt = (((1,), (1,)), ((), ()))
    for h in range(X_HEADS):
        cols = slice(h * X_HEAD_DIM, (h + 1) * X_HEAD_DIM)
        vcols = slice(D_MODEL + h * X_HEAD_DIM, D_MODEL + (h + 1) * X_HEAD_DIM)
        s = lax.dot_general(q_ref[:, cols], kv_ref[0, :, cols], nt, preferred_element_type=F32)
        p = jnp.exp(s - jnp.max(s, axis=-1, keepdims=True))
        l = jnp.sum(p, axis=-1, keepdims=True)
        o = jnp.dot(p.astype(BF16), kv_ref[0, :, vcols], preferred_element_type=F32)
        oc_ref[:, cols] = (o / l).astype(BF16)
    o_ref[0] = x + jnp.dot(oc_ref[...], wo_ref[...], preferred_element_type=F32)


def _cross(x3, g, wq, kv3, wo):
    b, seq, _ = x3.shape
    m = kv3.shape[1]
    tq = min(512, seq)
    blk = pl.BlockSpec((1, tq, D_MODEL), lambda b_, i: (b_, i, 0))
    return pl.pallas_call(
        _cross_kernel,
        grid=(b, seq // tq),
        in_specs=[blk, _resident(g), _resident(wq),
                  pl.BlockSpec((1, m, 2 * D_MODEL), lambda b_, i: (b_, 0, 0)), _resident(wo)],
        out_specs=blk,
        out_shape=jax.ShapeDtypeStruct(x3.shape, F32),
        scratch_shapes=[pltpu.VMEM((tq, D_MODEL), BF16), pltpu.VMEM((tq, D_MODEL), BF16)],
        compiler_params=_params("parallel", "arbitrary"),
        name="cross_attn_block",
    )(x3, g, wq, kv3, wo)


def _mlp_kernel(x_ref, g_ref, wu_ref, wd_ref, gf_ref, o_ref, hn_ref, acc_ref):
    f = pl.program_id(1)

    @pl.when(f == 0)
    def _():
        hn_ref[...] = _rms(x_ref[...], g_ref[...]).astype(BF16)
        acc_ref[...] = jnp.zeros_like(acc_ref)

    h = jnp.maximum(jnp.dot(hn_ref[...], wu_ref[...], preferred_element_type=F32), 0.0)
    acc_ref[...] += jnp.dot((h * h).astype(BF16), wd_ref[...], preferred_element_type=F32)

    @pl.when(f == pl.num_programs(1) - 1)
    def _():
        o_ref[...] = _rms(x_ref[...] + acc_ref[...], gf_ref[...])


def _mlp(x, g, wu, wd, gf):
    n = x.shape[0]
    tm = min(512, n)
    tf = 1024
    return pl.pallas_call(
        _mlp_kernel,
        grid=(n // tm, D_FF // tf),
        in_specs=[
            pl.BlockSpec((tm, D_MODEL), lambda i, f: (i, 0)),
            pl.BlockSpec((1, D_MODEL), lambda i, f: (0, 0)),
            pl.BlockSpec((D_MODEL, tf), lambda i, f: (0, f)),
            pl.BlockSpec((tf, D_MODEL), lambda i, f: (f, 0)),
            pl.BlockSpec((1, D_MODEL), lambda i, f: (0, 0)),
        ],
        out_specs=pl.BlockSpec((tm, D_MODEL), lambda i, f: (i, 0)),
        out_shape=jax.ShapeDtypeStruct((n, D_MODEL), F32),
        scratch_shapes=[pltpu.VMEM((tm, D_MODEL), BF16), pltpu.VMEM((tm, D_MODEL), F32)],
        compiler_params=_params("parallel", "arbitrary"),
        name="mlp_final_norm",
    )(x, g, wu, wd, gf)


def _rope_tables(seq):
    half = HEAD_DIM // 2
    inv = ROPE_THETA ** (-jnp.arange(0, HEAD_DIM, 2, dtype=F32) / HEAD_DIM)
    ang = jnp.arange(seq, dtype=F32)[:, None] * inv[None, :]
    cos, sin = jnp.cos(ang), jnp.sin(ang)
    cos_t = jnp.tile(cos, (1, LANES // half))
    sin_t = jnp.concatenate([-sin, -sin, sin, sin], axis=1)
    return (jnp.stack([cos_t * Q_SCALE, cos_t, jnp.ones_like(cos_t)]),
            jnp.stack([sin_t * Q_SCALE, sin_t, jnp.zeros_like(sin_t)]))


def _qk_shuffle():
    half = HEAD_DIM // 2
    src = jnp.arange(2 * ATTN_W).reshape(2 * N_HEADS, 2, 2, half).transpose(0, 2, 1, 3).reshape(-1)
    return (src[None, :] == jnp.arange(2 * ATTN_W)[:, None]).astype(BF16)


def _prepare(w):
    (norm_mix, w_in, diff_lambda, subln, w_attn_proj,
     lam_re, lam_im, log_dt, b_re, b_im, c_re, c_im, d_skip,
     w_glu_a, w_glu_b, w_mix_out,
     norm_cross, norm_mem, w_q_cross, w_kv_cross, w_o_cross,
     norm_mlp, w_mlp_up, w_mlp_down, norm_final) = w
    w_qk = jnp.dot(w_in[0, :, :2 * ATTN_W].astype(BF16), _qk_shuffle(), preferred_element_type=BF16)
    w_in_p = jnp.concatenate([w_qk, w_in[0, :, 2 * ATTN_W:].astype(BF16)], axis=1)
    row = lambda v: v.astype(F32).reshape(1, -1)
    return dict(
        norm_mix=row(norm_mix[0]), w_in=w_in_p, diff_lambda=diff_lambda[0].astype(F32),
        subln=row(subln[0]), w_attn_proj=w_attn_proj[0].astype(BF16),
        ssm=_ssm_operators(lam_re[0], lam_im[0], log_dt[0], b_re[0], b_im[0], c_re[0], c_im[0],
                           d_skip[0]),
        w_glu_a=w_glu_a[0].astype(BF16), w_glu_b=w_glu_b[0].astype(BF16),
        w_mix_out=w_mix_out[0].astype(BF16),
        norm_cross=row(norm_cross[0]), norm_mem=row(norm_mem[0]),
        w_q_cross=w_q_cross[0].astype(BF16), w_kv_cross=w_kv_cross[0].astype(BF16),
        w_o_cross=w_o_cross[0].astype(BF16),
        norm_mlp=row(norm_mlp[0]), w_mlp_up=w_mlp_up[0].astype(BF16),
        w_mlp_down=w_mlp_down[0].astype(BF16), norm_final=row(norm_final),
    )


def _encode(x, mem, p, tables):
    b, seq, _ = x.shape
    assert seq % (8 * CHUNK) == 0
    n = b * seq
    nc = seq // CHUNK
    x2 = x.reshape(n, D_MODEL)
    cos_t, sin_t = (t[:, :seq] for t in tables)

    proj, u = _in_proj(x2, p["norm_mix"], p["w_in"], cos_t, sin_t, seq)
    att = _diff_attn(proj.reshape(b, seq, -1), p["diff_lambda"], p["subln"])

    z = _ssm(u.reshape(b * nc, CHUNK, SSM_W), b, *p["ssm"]).reshape(n, SSM_W)

    x2 = _merge(x2, att.reshape(n, ATTN_W), z, proj, p["w_attn_proj"], p["w_glu_a"], p["w_glu_b"],
                p["w_mix_out"])

    m_tok = mem.shape[1]
    kv = _norm_mm(mem.reshape(b * m_tok, D_MODEL), p["norm_mem"], p["w_kv_cross"], 1.0)
    x2 = _cross(x2.reshape(b, seq, D_MODEL), p["norm_cross"], p["w_q_cross"],
                kv.reshape(b, m_tok, 2 * D_MODEL), p["w_o_cross"]).reshape(n, D_MODEL)

    out = _mlp(x2, p["norm_mlp"], p["w_mlp_up"], p["w_mlp_down"], p["norm_final"])
    return out.reshape(b, seq, D_MODEL)


def kernel(x_prompt, x_sample, mem_prompt, mem_sample, norm_mix, w_in, diff_lambda, subln, w_attn_proj, ssm_lambda_re, ssm_lambda_im, ssm_log_dt, ssm_b_re, ssm_b_im, ssm_c_re, ssm_c_im, ssm_d, w_glu_a, w_glu_b, w_mix_out, norm_cross, norm_mem, w_q_cross, w_kv_cross, w_o_cross, norm_mlp, w_mlp_up, w_mlp_down, norm_final):
    p = _prepare((norm_mix, w_in, diff_lambda, subln, w_attn_proj,
                  ssm_lambda_re, ssm_lambda_im, ssm_log_dt, ssm_b_re, ssm_b_im, ssm_c_re, ssm_c_im,
                  ssm_d, w_glu_a, w_glu_b, w_mix_out,
                  norm_cross, norm_mem, w_q_cross, w_kv_cross, w_o_cross,
                  norm_mlp, w_mlp_up, w_mlp_down, norm_final))
    tables = _rope_tables(max(x_prompt.shape[1], x_sample.shape[1]))
    return (_encode(x_prompt, mem_prompt, p, tables), _encode(x_sample, mem_sample, p, tables))
```

```python
import functools
import math

import jax
import jax.numpy as jnp
from jax import lax
from jax.experimental import pallas as pl
from jax.experimental.pallas import tpu as pltpu

D_MODEL = 2048
N_HEADS = 8
HEAD_DIM = 64
ATTN_W = N_HEADS * 2 * HEAD_DIM
SSM_W = D_MODEL // 2
GROUP_SIZE = 16
N_GROUPS = SSM_W // GROUP_SIZE
STATE = 64
IN_COLS = 3 * ATTN_W + SSM_W + 2 * D_MODEL
D_FF = 4 * D_MODEL
X_HEADS = 4
X_HEAD_DIM = D_MODEL // X_HEADS
ROPE_THETA = 10000.0
EPS = 1e-6
LAM_INIT = 0.8 - 0.6 * math.exp(-0.3 * 0)
Q_SCALE = HEAD_DIM ** -0.5 * math.log2(math.e)

CHUNK = 16
GROUP_BLOCK = 8
LANES = 128
VMEM_LIMIT = 56 * 1024 * 1024

F32 = jnp.float32
BF16 = jnp.bfloat16
HI = lax.Precision.HIGHEST


def _params(*sem):
    return pltpu.CompilerParams(dimension_semantics=sem, vmem_limit_bytes=VMEM_LIMIT)


def _rms(x, g):
    return x * lax.rsqrt(jnp.mean(x * x, axis=-1, keepdims=True) + EPS) * g


def _in_proj_kernel(x_ref, g_ref, w_ref, cos_ref, sin_ref, o_ref, u_ref, xn_ref, acc_a, acc_b):
    j = pl.program_id(1)

    @pl.when(j == 0)
    def _():
        xn_ref[...] = _rms(x_ref[...], g_ref[...]).astype(BF16)

    def project(cur):
        cur[...] = jnp.dot(xn_ref[...], w_ref[...], preferred_element_type=F32)

    def finish(prev):
        cos = cos_ref[0]
        sin = sin_ref[0]
        for c in range(prev.shape[1] // LANES):
            xc = prev[:, c * LANES:(c + 1) * LANES]
            o_ref[:, c * LANES:(c + 1) * LANES] = (
                xc * cos + pltpu.roll(xc, LANES // 2, axis=1) * sin).astype(BF16)

    @pl.when(j == 0)
    def _():
        project(acc_a)

    @pl.when((j > 0) & (j % 2 == 0))
    def _():
        finish(acc_b)
        project(acc_a)

    @pl.when(j % 2 == 1)
    def _():
        finish(acc_a)
        project(acc_b)

    @pl.when(j == pl.num_programs(1) - 1)
    def _():
        u_ref[...] = acc_b[...]


def _in_proj(x2, g, w, cos_t, sin_t, seq):
    n = x2.shape[0]
    tm = min(1024, seq)
    tn = ATTN_W
    nj = IN_COLS // tn
    assert nj % 2 == 0
    pos_blocks = seq // tm
    table = lambda i, j: (jnp.where(j == 5, 0, jnp.where(j == 6, 1, 2)), i % pos_blocks, 0)
    return pl.pallas_call(
        _in_proj_kernel,
        grid=(n // tm, nj),
        in_specs=[
            pl.BlockSpec((tm, D_MODEL), lambda i, j: (i, 0)),
            pl.BlockSpec((1, D_MODEL), lambda i, j: (0, 0)),
            pl.BlockSpec((D_MODEL, tn), lambda i, j: (0, (j + nj // 2) % nj)),
            pl.BlockSpec((1, tm, LANES), table),
            pl.BlockSpec((1, tm, LANES), table),
        ],
        out_specs=[
            pl.BlockSpec((tm, tn), lambda i, j: (i, jnp.maximum(j - 1, 0))),
            pl.BlockSpec((tm, tn), lambda i, j: (i, 0)),
        ],
        out_shape=[jax.ShapeDtypeStruct((n, (nj - 1) * tn), BF16),
                   jax.ShapeDtypeStruct((n, SSM_W), F32)],
        scratch_shapes=[pltpu.VMEM((tm, D_MODEL), BF16), pltpu.VMEM((tm, tn), F32),
                        pltpu.VMEM((tm, tn), F32)],
        compiler_params=_params("parallel", "arbitrary"),
        name="in_proj",
    )(x2, g, w, cos_t, sin_t)


def _diff_attn_kernel(dl_ref, sub_ref, q_ref, k_ref, v_ref, o_ref, kt_ref, s_ref, mx_ref, m_ref, acc_ref,
                      *, tq, tk, unroll):
    seq = v_ref.shape[1]
    nk = seq // tk
    steps = (seq // tq) * nk
    hw = 2 * HEAD_DIM
    lane = lax.broadcasted_iota(jnp.int32, (tq, hw), 1)
    first = ((lane // (HEAD_DIM // 2)) % 2) == 0
    ones = jnp.ones((tk, hw), BF16)
    dl = dl_ref[...]
    lam = (jnp.exp(jnp.sum(dl[0:1] * dl[1:2], axis=-1, keepdims=True))
           - jnp.exp(jnp.sum(dl[2:3] * dl[3:4], axis=-1, keepdims=True)) + LAM_INIT)

    def scores(f, slot):
        f = jnp.minimum(f, steps - 1)
        q = q_ref[0, pl.ds(pl.multiple_of((f // nk) * tq, tq), tq), :]
        kc = kt_ref[:, pl.ds(pl.multiple_of((f % nk) * tk, tk), tk)]
        zero = jnp.zeros_like(q)
        for comp, qm in enumerate((jnp.where(first, q, zero), jnp.where(first, zero, q))):
            s = jnp.dot(qm, kc, preferred_element_type=F32)
            s_ref[slot, comp] = s
            mx_ref[slot, comp] = jnp.max(s, axis=-1, keepdims=True)

    def accumulate(f, slot, emit):
        c = f % nk
        vx = jnp.concatenate([v_ref[0, pl.ds(pl.multiple_of(c * tk, tk), tk), :], ones], axis=1)
        acc = []
        for comp in range(2):
            m_old = jnp.where(c == 0, -jnp.inf, m_ref[comp])
            m_new = jnp.maximum(m_old, mx_ref[slot, comp])
            m_ref[comp] = m_new
            p = jnp.exp2(s_ref[slot, comp] - m_new).astype(BF16)
            acc.append(jnp.exp2(m_old - m_new) * acc_ref[comp]
                       + jnp.dot(p, vx, preferred_element_type=F32))
            acc_ref[comp] = acc[comp]
        if emit:
            o = acc[0][:, :hw] / acc[0][:, hw:] - lam * (acc[1][:, :hw] / acc[1][:, hw:])
            rows = pl.ds(pl.multiple_of((f // nk) * tq, tq), tq)
            o_ref[0, rows, :] = (_rms(o, sub_ref[...]) * (1.0 - LAM_INIT)).astype(BF16)

    for r in range(0, seq, tq):
        kt_ref[:, r:r + tq] = k_ref[0, r:r + tq, :].T

    m_ref[...] = jnp.full(m_ref.shape, -jnp.inf, F32)
    acc_ref[...] = jnp.zeros(acc_ref.shape, F32)
    scores(0, 0)

    def body(i, _):
        for k in range(unroll):
            f = i * unroll + k
            scores(f + 1, (k + 1) % 2)
            accumulate(f, k % 2, emit=(k + 1) % min(unroll, nk) == 0)
        return 0

    lax.fori_loop(0, steps // unroll, body, 0)


def _diff_attn(proj3, dl, sub):
    b, seq, _ = proj3.shape
    tq = min(256, seq) if seq <= 2048 else 128
    tk = seq
    nk = seq // tk
    steps = (seq // tq) * nk
    unroll = 4 if steps % 4 == 0 else (2 if steps % 2 == 0 else 1)
    assert nk % unroll == 0 or unroll % nk == 0
    hw = 2 * HEAD_DIM
    blk = pl.BlockSpec((1, seq, hw), lambda b_, h: (b_, 0, h))
    col = lambda first: pl.BlockSpec((1, seq, hw), lambda b_, h: (b_, 0, first // hw + h))
    return pl.pallas_call(
        functools.partial(_diff_attn_kernel, tq=tq, tk=tk, unroll=unroll),
        grid=(b, N_HEADS),
        in_specs=[
            pl.BlockSpec((4, HEAD_DIM), lambda b_, h: (0, 0)),
            pl.BlockSpec((1, hw), lambda b_, h: (0, 0)),
            col(2 * D_MODEL), col(2 * D_MODEL + ATTN_W), col(2 * D_MODEL + 2 * ATTN_W),
        ],
        out_specs=blk,
        out_shape=jax.ShapeDtypeStruct((b, seq, ATTN_W), BF16),
        scratch_shapes=[pltpu.VMEM((hw, seq), BF16),
                        pltpu.VMEM((2, 2, tq, tk), F32), pltpu.VMEM((2, 2, tq, 1), F32),
                        pltpu.VMEM((2, tq, 1), F32), pltpu.VMEM((2, tq, 2 * hw), F32)],
        compiler_params=_params("parallel", "arbitrary"),
        name="diff_attn",
    )(dl, sub, proj3, proj3, proj3)


def _slot_swap_matrix():
    src = jnp.arange(8 * LANES).reshape(8, 8, GROUP_SIZE).transpose(1, 0, 2).reshape(-1)
    return (src[None, :] == jnp.arange(8 * LANES)[:, None]).astype(BF16)


def _ssm_kernel(u_ref, perm_ref, wi_ref, wb_ref, wc_ref, ap_ref, a8_ref, pt_ref, z_ref,
                ut_ref, s_ref, zt_ref, e_ref, hc_ref):
    nc = u_ref.shape[0]
    n_tiles = nc // 8
    gw = CHUNK * GROUP_SIZE
    pw = 2 * gw

    for th in range(CHUNK // 8):
        by_token = jnp.concatenate([u_ref[:, th * 8 + tl, :] for tl in range(8)], axis=1)
        by_group = jnp.dot(by_token.astype(BF16), perm_ref[...], preferred_element_type=F32)
        for g in range(GROUP_BLOCK):
            ut_ref[:, g * gw + th * LANES:g * gw + (th + 1) * LANES] = (
                by_group[:, g * LANES:(g + 1) * LANES].astype(BF16))

    for q in range(GROUP_BLOCK // 2):
        cols = slice(q * pw, (q + 1) * pw)
        s_ref[:, cols] = jnp.dot(ut_ref[:, cols], wb_ref[0, q], preferred_element_type=F32)

    row = lax.broadcasted_iota(jnp.int32, (8, LANES), 0)

    def shift_down(x, k):
        return jnp.where(row >= k, pltpu.roll(x, k, axis=0), 0.0)

    def shift_up(x, k):
        return jnp.where(row < 8 - k, pltpu.roll(x, 8 - k, axis=0), 0.0)

    n_chain = 2 * (GROUP_BLOCK // 2)

    def chain_cols(q, d):
        base = q * pw + d * 2 * LANES
        return slice(base, base + LANES), slice(base + LANES, base + 2 * LANES)

    def local_scan(i, _):
        rs = pl.ds(pl.multiple_of(i * 8, 8), 8)
        for q in range(GROUP_BLOCK // 2):
            for d in range(2):
                c_re, c_im = chain_cols(q, d)
                shift = shift_down if d == 0 else shift_up
                ch = q * 2 + d
                x_re, x_im = s_ref[rs, c_re], s_ref[rs, c_im]
                for ki, k in enumerate((1, 2, 4)):
                    a_re, a_im = ap_ref[0, (2 * ch) * 3 + ki], ap_ref[0, (2 * ch + 1) * 3 + ki]
                    y_re, y_im = shift(x_re, k), shift(x_im, k)
                    x_re, x_im = (x_re + a_re * y_re - a_im * y_im,
                                  x_im + a_re * y_im + a_im * y_re)
                s_ref[rs, c_re] = shift(x_re, 1)
                s_ref[rs, c_im] = shift(x_im, 1)
                pos = i if d == 0 else n_tiles - 1 - i
                edge = slice(7, 8) if d == 0 else slice(0, 1)
                e_ref[pos, ch:ch + 1, :] = x_re[edge]
                e_ref[pos, n_chain + ch:n_chain + ch + 1, :] = x_im[edge]
        return 0

    lax.fori_loop(0, n_tiles, local_scan, 0)

    a8_re, a8_im = a8_ref[0, :n_chain], a8_ref[0, n_chain:]

    def carry_in(i, h):
        h_re, h_im = h
        hc_ref[i, :n_chain, :] = h_re
        hc_ref[i, n_chain:, :] = h_im
        e_re, e_im = e_ref[i, :n_chain, :], e_ref[i, n_chain:, :]
        return a8_re * h_re - a8_im * h_im + e_re, a8_re * h_im + a8_im * h_re + e_im

    zero = jnp.zeros((n_chain, LANES), F32)
    lax.fori_loop(0, n_tiles, carry_in, (zero, zero))

    def add_carry(i, _):
        rs = pl.ds(pl.multiple_of(i * 8, 8), 8)
        for q in range(GROUP_BLOCK // 2):
            for d in range(2):
                c_re, c_im = chain_cols(q, d)
                ch = q * 2 + d
                pos = i if d == 0 else n_tiles - 1 - i
                h_re = jnp.broadcast_to(hc_ref[pos, ch:ch + 1, :], (8, LANES))
                h_im = jnp.broadcast_to(hc_ref[pos, n_chain + ch:n_chain + ch + 1, :], (8, LANES))
                p_re, p_im = pt_ref[0, 2 * ch], pt_ref[0, 2 * ch + 1]
                s_ref[rs, c_re] += p_re * h_re - p_im * h_im
                s_ref[rs, c_im] += p_re * h_im + p_im * h_re
        return 0

    lax.fori_loop(0, n_tiles, add_carry, 0)

    for q in range(GROUP_BLOCK // 2):
        cols = slice(q * pw, (q + 1) * pw)
        carried = jnp.dot(s_ref[:, cols].astype(BF16), wc_ref[0, q], preferred_element_type=F32)
        for g2 in range(2):
            g = 2 * q + g2
            gc = slice(g * gw, (g + 1) * gw)
            y = carried[:, g2 * gw:(g2 + 1) * gw] + jnp.dot(
                ut_ref[:, gc], wi_ref[0, g], preferred_element_type=F32)
            z = 0.5 * y * (1.0 + jnp.tanh(0.7978845608028654 * (y + 0.044715 * (y * y * y))))
            ut_ref[:, gc] = z.astype(BF16)

    for th in range(CHUNK // 8):
        by_group = jnp.concatenate(
            [ut_ref[:, g * gw + th * LANES:g * gw + (th + 1) * LANES] for g in range(GROUP_BLOCK)], axis=1)
        by_token = jnp.dot(by_group, perm_ref[...], preferred_element_type=F32)
        for tl in range(8):
            zt_ref[:, th * 8 + tl, :] = by_token[:, tl * LANES:(tl + 1) * LANES]
    z_ref[...] = zt_ref[...].astype(BF16)


def _ssm(u3, b, wi, wb, wc, ap, a8, pt):
    nc = u3.shape[0] // b
    cw = GROUP_BLOCK * CHUNK * GROUP_SIZE
    blk = pl.BlockSpec((nc, CHUNK, LANES), lambda g, b_: (b_, 0, g))
    whole = lambda a: pl.BlockSpec((1,) + a.shape[1:], lambda g, b_: (g,) + (0,) * (a.ndim - 1))
    carry = pltpu.VMEM((nc // 8, 2 * GROUP_BLOCK, LANES), F32)
    perm = _slot_swap_matrix()
    return pl.pallas_call(
        _ssm_kernel,
        grid=(N_GROUPS // GROUP_BLOCK, b),
        in_specs=[blk, pl.BlockSpec(perm.shape, lambda g, b_: (0, 0)),
                  whole(wi), whole(wb), whole(wc), whole(ap), whole(a8), whole(pt)],
        out_specs=blk,
        out_shape=jax.ShapeDtypeStruct(u3.shape, BF16),
        scratch_shapes=[pltpu.VMEM((nc, cw), BF16), pltpu.VMEM((nc, cw), F32),
                        pltpu.VMEM((nc, CHUNK, LANES), F32), carry, carry],
        compiler_params=_params("parallel", "arbitrary"),
        name="s5_scan",
    )(u3, perm, wi, wb, wc, ap, a8, pt)


def _ssm_operators(lam_re, lam_im, log_dt, b_re, b_im, c_re, c_im, d_skip):
    t_ = CHUNK
    g_, p_, c_ = N_GROUPS, STATE, GROUP_SIZE
    nb, npair = g_ // GROUP_BLOCK, g_ // 2
    lr, li = lam_re.astype(F32), lam_im.astype(F32)
    dt = jnp.exp(log_dt.astype(F32))[..., None]
    mag = jnp.exp(lr * dt)
    ab_re, ab_im = mag * jnp.cos(li * dt), mag * jnp.sin(li * dt)
    den = lr * lr + li * li
    n_re, n_im = ab_re - 1.0, ab_im
    k_re = (n_re * lr + n_im * li) / den
    k_im = (n_im * lr - n_re * li) / den
    br, bi = b_re.astype(F32), b_im.astype(F32)
    bb_re = k_re[..., None] * br - k_im[..., None] * bi
    bb_im = k_re[..., None] * bi + k_im[..., None] * br
    cr, ci = c_re.astype(F32), c_im.astype(F32)

    x_re, x_im = lr * dt, li * dt
    gw = t_ * c_

    def cpow(e, xr, xi):
        m = jnp.exp(e * xr)
        return m * jnp.cos(e * xi), m * jnp.sin(e * xi)

    swap = lambda a: a.transpose(0, 1, 3, 2)
    tok = jnp.repeat(jnp.arange(t_, dtype=F32), c_)
    first = (jnp.arange(2) == 0)[None, None, :, None, None]

    pw_r, pw_i = cpow(tok, x_re[..., None], x_im[..., None])
    rep = jnp.tile(jnp.eye(c_, dtype=F32), (1, t_))
    crt = jnp.einsum('dgop,on->dgpn', cr, rep, precision=HI)
    cit = jnp.einsum('dgop,on->dgpn', ci, rep, precision=HI)
    lhs = jnp.concatenate([crt * pw_r - cit * pw_i, -(crt * pw_i + cit * pw_r)], axis=2)
    rhs = jnp.concatenate([swap(bb_re), swap(bb_im)], axis=3)
    kern = jnp.einsum('dgck,dgkn->dgcn', rhs, lhs, precision=lax.Precision.HIGH)
    fwd, bwd = kern[0], kern[1]
    bwd_rev = bwd.reshape(g_, c_, t_, c_)[:, :, ::-1].reshape(g_, c_, gw)
    lagged = jnp.concatenate([bwd_rev[..., :gw - c_], fwd[..., :c_] + bwd[..., :c_], fwd[..., c_:]], axis=-1)
    n_lag = lagged.shape[-1]
    start = (t_ - 1 - jnp.arange(t_)) * c_
    window = (jnp.arange(n_lag)[None, :, None] == start[:, None, None] + jnp.arange(gw)[None, None, :])
    w_intra = jnp.einsum('gck,tkn->gtcn', lagged.astype(BF16), window.astype(BF16),
                         preferred_element_type=F32)
    skip = jnp.eye(gw, dtype=F32) * jnp.tile(d_skip.astype(F32).reshape(g_, c_), (1, t_))[:, :, None]
    w_intra = (w_intra.reshape(g_, gw, gw) + skip).reshape(nb, GROUP_BLOCK, gw, gw)

    def pair_rows(bb):
        x = jnp.tile(swap(bb), (1, 1, t_, 1)).reshape(2, npair, 2, gw, p_)
        return jnp.concatenate([jnp.where(first, x, 0.0), jnp.where(first, 0.0, x)], axis=-1).reshape(
            2, npair, 2 * gw, 2 * p_)

    steps = jnp.arange(t_, dtype=F32)
    e_rows = jnp.stack([t_ - 1 - steps, steps])[:, None, :, None]
    pr, pi = cpow(e_rows, x_re.reshape(2, npair, 1, 2 * p_), x_im.reshape(2, npair, 1, 2 * p_))
    by_row = lambda a: jnp.broadcast_to(a[:, :, None, :, None, :], (2, npair, 2, t_, c_, 2 * p_)).reshape(
        2, npair, 2 * gw, 2 * p_)
    pr, pi = by_row(pr), by_row(pi)
    bt_re, bt_im = pair_rows(bb_re), pair_rows(bb_im)
    wb_re, wb_im = pr * bt_re - pi * bt_im, pr * bt_im + pi * bt_re
    wb = jnp.concatenate([wb_re[0], wb_im[0], wb_re[1], wb_im[1]], axis=-1)
    wb = wb.reshape(nb, GROUP_BLOCK // 2, 2 * gw, 8 * p_)

    def pair_cols(cc_tiled):
        x = cc_tiled.reshape(2, npair, 2, p_, gw)
        return jnp.concatenate([jnp.where(first, x, 0.0), jnp.where(first, 0.0, x)], axis=-1).reshape(
            2, npair, 2 * p_, 2 * gw)

    e_cols = jnp.stack([steps + 1, t_ - steps])[:, None, None, :]
    qr, qi = cpow(e_cols, x_re.reshape(2, npair, 2 * p_, 1), x_im.reshape(2, npair, 2 * p_, 1))
    spread = jnp.tile(jnp.repeat(jnp.eye(t_, dtype=F32), c_, axis=1), (1, 2))
    qr = jnp.einsum('dqrt,tn->dqrn', qr, spread, precision=HI)
    qi = jnp.einsum('dqrt,tn->dqrn', qi, spread, precision=HI)
    ct_re, ct_im = pair_cols(crt), pair_cols(cit)
    wc_re, wc_im = ct_re * qr - ct_im * qi, -(ct_re * qi + ct_im * qr)
    wc = jnp.concatenate([wc_re[0], wc_im[0], wc_re[1], wc_im[1]], axis=1)
    wc = wc.reshape(nb, GROUP_BLOCK // 2, 8 * p_, 2 * gw)

    def table(e):
        re, im = cpow(e, x_re, x_im)
        tab = jnp.stack([re, im], axis=2)
        k = tab.shape[0]
        tab = tab.reshape(k, 2, 2, nb, GROUP_BLOCK // 2, 2 * p_).transpose(3, 4, 1, 2, 0, 5)
        return tab.reshape(nb, (GROUP_BLOCK // 2) * 4, k, 2 * p_)

    step = jnp.asarray([1.0, 2.0, 4.0], F32) * t_
    ap = table(jnp.broadcast_to(step[:, None, None, None], (3, 2, 1, 1)))
    ap = jnp.broadcast_to(ap[:, :, :, None, :], ap.shape[:3] + (8, 2 * p_)).reshape(nb, -1, 8, 2 * p_)
    a8 = table(jnp.full((1, 2, 1, 1), 8.0 * t_, F32))
    a8 = a8.reshape(nb, GROUP_BLOCK // 2, 2, 2, 2 * p_).transpose(0, 3, 1, 2, 4).reshape(nb, -1, 2 * p_)
    j8 = jnp.arange(8, dtype=F32) * t_
    pt = table(jnp.stack([j8, j8[::-1]], axis=1)[:, :, None, None])
    return w_intra.astype(BF16), wb.astype(BF16), wc.astype(BF16), ap, a8, pt


def _resident(a):
    return pl.BlockSpec(a.shape, lambda *_: (0,) * a.ndim, pipeline_mode=pl.Buffered(1))


def _merge_kernel(x_ref, att_ref, z_ref, gate_ref, wp_ref, wa_ref, wb_ref, wo_ref, o_ref, mix_ref,
                  *, tn):
    att = att_ref[...]
    z = z_ref[...]
    for j in range(D_MODEL // tn):
        cs = slice(j * tn, (j + 1) * tn)
        gs = slice(D_MODEL + j * tn, D_MODEL + (j + 1) * tn)
        y_attn = jnp.dot(att, wp_ref[:, cs], preferred_element_type=F32)
        y_ssm = (jnp.dot(z, wa_ref[:, cs], preferred_element_type=F32)
                 * jax.nn.sigmoid(jnp.dot(z, wb_ref[:, cs], preferred_element_type=F32)))
        g_a = jax.nn.sigmoid(gate_ref[:, cs].astype(F32))
        g_s = jax.nn.sigmoid(gate_ref[:, gs].astype(F32))
        mix_ref[:, cs] = (g_a * y_attn + g_s * y_ssm).astype(BF16)
    o_ref[...] = x_ref[...] + jnp.dot(mix_ref[...], wo_ref[...], preferred_element_type=F32)


def _merge(x2, att, z, gates, wp, wa, wb, wo):
    n = att.shape[0]
    tm = min(256, n)
    row = lambda cols: pl.BlockSpec((tm, cols), lambda i: (i, 0))
    return pl.pallas_call(
        functools.partial(_merge_kernel, tn=512),
        grid=(n // tm,),
        in_specs=[row(D_MODEL), row(ATTN_W), row(SSM_W), row(2 * D_MODEL),
                  _resident(wp), _resident(wa), _resident(wb), _resident(wo)],
        out_specs=row(D_MODEL),
        out_shape=jax.ShapeDtypeStruct((n, D_MODEL), F32),
        scratch_shapes=[pltpu.VMEM((tm, D_MODEL), BF16)],
        compiler_params=_params("parallel"),
        name="merge_mix_out",
    )(x2, att, z, gates, wp, wa, wb, wo)


def _norm_mm_kernel(x_ref, g_ref, w_ref, o_ref, xn_ref, *, scale):
    @pl.when(pl.program_id(1) == 0)
    def _():
        xn_ref[...] = _rms(x_ref[...], g_ref[...]).astype(BF16)

    acc = jnp.dot(xn_ref[...], w_ref[...], preferred_element_type=F32)
    o_ref[...] = (acc * scale).astype(BF16)


def _norm_mm(x, g, w, scale):
    n, kdim = x.shape
    cols = w.shape[1]
    tm = min(512, n)
    tn = 1024
    return pl.pallas_call(
        functools.partial(_norm_mm_kernel, scale=scale),
        grid=(n // tm, cols // tn),
        in_specs=[
            pl.BlockSpec((tm, kdim), lambda i, j: (i, 0)),
            pl.BlockSpec((1, kdim), lambda i, j: (0, 0)),
            pl.BlockSpec((kdim, tn), lambda i, j: (0, j)),
        ],
        out_specs=pl.BlockSpec((tm, tn), lambda i, j: (i, j)),
        out_shape=jax.ShapeDtypeStruct((n, cols), BF16),
        scratch_shapes=[pltpu.VMEM((tm, kdim), BF16)],
        compiler_params=_params("parallel", "arbitrary"),
        name="norm_matmul",
    )(x, g, w)


def _cross_kernel(x_ref, g_ref, wq_ref, kv_ref, wo_ref, o_ref, q_ref, oc_ref):
    x = x_ref[0]
    hn = _rms(x, g_ref[...]).astype(BF16)
    q_ref[...] = (jnp.dot(hn, wq_ref[...], preferred_element_type=F32)
                  * X_HEAD_DIM ** -0.5).astype(BF16)
    nt = (((1,), (1,)), ((), ()))
    for h in range(X_HEADS):
        cols = slice(h * X_HEAD_DIM, (h + 1) * X_HEAD_DIM)
        vcols = slice(D_MODEL + h * X_HEAD_DIM, D_MODEL + (h + 1) * X_HEAD_DIM)
        s = lax.dot_general(q_ref[:, cols], kv_ref[0, :, cols], nt, preferred_element_type=F32)
        p = jnp.exp(s - jnp.max(s, axis=-1, keepdims=True))
        l = jnp.sum(p, axis=-1, keepdims=True)
        o = jnp.dot(p.astype(BF16), kv_ref[0, :, vcols], preferred_element_type=F32)
        oc_ref[:, cols] = (o / l).astype(BF16)
    o_ref[0] = x + jnp.dot(oc_ref[...], wo_ref[...], preferred_element_type=F32)


def _cross(x3, g, wq, kv3, wo):
    b, seq, _ = x3.shape
    m = kv3.shape[1]
    tq = min(512, seq)
    blk = pl.BlockSpec((1, tq, D_MODEL), lambda b_, i: (b_, i, 0))
    return pl.pallas_call(
        _cross_kernel,
        grid=(b, seq // tq),
        in_specs=[blk, _resident(g), _resident(wq),
                  pl.BlockSpec((1, m, 2 * D_MODEL), lambda b_, i: (b_, 0, 0)), _resident(wo)],
        out_specs=blk,
        out_shape=jax.ShapeDtypeStruct(x3.shape, F32),
        scratch_shapes=[pltpu.VMEM((tq, D_MODEL), BF16), pltpu.VMEM((tq, D_MODEL), BF16)],
        compiler_params=_params("parallel", "arbitrary"),
        name="cross_attn_block",
    )(x3, g, wq, kv3, wo)


def _mlp_kernel(x_ref, g_ref, wu_ref, wd_ref, gf_ref, o_ref, hn_ref, acc_ref):
    f = pl.program_id(1)

    @pl.when(f == 0)
    def _():
        hn_ref[...] = _rms(x_ref[...], g_ref[...]).astype(BF16)
        acc_ref[...] = jnp.zeros_like(acc_ref)

    h = jnp.maximum(jnp.dot(hn_ref[...], wu_ref[...], preferred_element_type=F32), 0.0)
    acc_ref[...] += jnp.dot((h * h).astype(BF16), wd_ref[...], preferred_element_type=F32)

    @pl.when(f == pl.num_programs(1) - 1)
    def _():
        o_ref[...] = _rms(x_ref[...] + acc_ref[...], gf_ref[...])


def _mlp(x, g, wu, wd, gf):
    n = x.shape[0]
    tm = min(512, n)
    tf = 1024
    return pl.pallas_call(
        _mlp_kernel,
        grid=(n // tm, D_FF // tf),
        in_specs=[
            pl.BlockSpec((tm, D_MODEL), lambda i, f: (i, 0)),
            pl.BlockSpec((1, D_MODEL), lambda i, f: (0, 0)),
            pl.BlockSpec((D_MODEL, tf), lambda i, f: (0, f)),
            pl.BlockSpec((tf, D_MODEL), lambda i, f: (f, 0)),
            pl.BlockSpec((1, D_MODEL), lambda i, f: (0, 0)),
        ],
        out_specs=pl.BlockSpec((tm, D_MODEL), lambda i, f: (i, 0)),
        out_shape=jax.ShapeDtypeStruct((n, D_MODEL), F32),
        scratch_shapes=[pltpu.VMEM((tm, D_MODEL), BF16), pltpu.VMEM((tm, D_MODEL), F32)],
        compiler_params=_params("parallel", "arbitrary"),
        name="mlp_final_norm",
    )(x, g, wu, wd, gf)


def _rope_tables(seq):
    half = HEAD_DIM // 2
    inv = ROPE_THETA ** (-jnp.arange(0, HEAD_DIM, 2, dtype=F32) / HEAD_DIM)
    ang = jnp.arange(seq, dtype=F32)[:, None] * inv[None, :]
    cos, sin = jnp.cos(ang), jnp.sin(ang)
    cos_t = jnp.tile(cos, (1, LANES // half))
    sin_t = jnp.concatenate([-sin, -sin, sin, sin], axis=1)
    return (jnp.stack([cos_t * Q_SCALE, cos_t, jnp.ones_like(cos_t)]),
            jnp.stack([sin_t * Q_SCALE, sin_t, jnp.zeros_like(sin_t)]))


def _qk_shuffle():
    half = HEAD_DIM // 2
    src = jnp.arange(2 * ATTN_W).reshape(2 * N_HEADS, 2, 2, half).transpose(0, 2, 1, 3).reshape(-1)
    return (src[None, :] == jnp.arange(2 * ATTN_W)[:, None]).astype(BF16)


def _prepare(w):
    (norm_mix, w_in, diff_lambda, subln, w_attn_proj,
     lam_re, lam_im, log_dt, b_re, b_im, c_re, c_im, d_skip,
     w_glu_a, w_glu_b, w_mix_out,
     norm_cross, norm_mem, w_q_cross, w_kv_cross, w_o_cross,
     norm_mlp, w_mlp_up, w_mlp_down, norm_final) = w
    w_qk = jnp.dot(w_in[0, :, :2 * ATTN_W].astype(BF16), _qk_shuffle(), preferred_element_type=BF16)
    w_in_p = jnp.concatenate([w_qk, w_in[0, :, 2 * ATTN_W:].astype(BF16)], axis=1)
    row = lambda v: v.astype(F32).reshape(1, -1)
    return dict(
        norm_mix=row(norm_mix[0]), w_in=w_in_p, diff_lambda=diff_lambda[0].astype(F32),
        subln=row(subln[0]), w_attn_proj=w_attn_proj[0].astype(BF16),
        ssm=_ssm_operators(lam_re[0], lam_im[0], log_dt[0], b_re[0], b_im[0], c_re[0], c_im[0],
                           d_skip[0]),
        w_glu_a=w_glu_a[0].astype(BF16), w_glu_b=w_glu_b[0].astype(BF16),
        w_mix_out=w_mix_out[0].astype(BF16),
        norm_cross=row(norm_cross[0]), norm_mem=row(norm_mem[0]),
        w_q_cross=w_q_cross[0].astype(BF16), w_kv_cross=w_kv_cross[0].astype(BF16),
        w_o_cross=w_o_cross[0].astype(BF16),
        norm_mlp=row(norm_mlp[0]), w_mlp_up=w_mlp_up[0].astype(BF16),
        w_mlp_down=w_mlp_down[0].astype(BF16), norm_final=row(norm_final),
    )


def _encode(x, mem, p, tables):
    b, seq, _ = x.shape
    assert seq % (8 * CHUNK) == 0
    n = b * seq
    nc = seq // CHUNK
    x2 = x.reshape(n, D_MODEL)
    cos_t, sin_t = (t[:, :seq] for t in tables)

    proj, u = _in_proj(x2, p["norm_mix"], p["w_in"], cos_t, sin_t, seq)
    att = _diff_attn(proj.reshape(b, seq, -1), p["diff_lambda"], p["subln"])

    z = _ssm(u.reshape(b * nc, CHUNK, SSM_W), b, *p["ssm"]).reshape(n, SSM_W)

    x2 = _merge(x2, att.reshape(n, ATTN_W), z, proj, p["w_attn_proj"], p["w_glu_a"], p["w_glu_b"],
                p["w_mix_out"])

    m_tok = mem.shape[1]
    kv = _norm_mm(mem.reshape(b * m_tok, D_MODEL), p["norm_mem"], p["w_kv_cross"], 1.0)
    x2 = _cross(x2.reshape(b, seq, D_MODEL), p["norm_cross"], p["w_q_cross"],
                kv.reshape(b, m_tok, 2 * D_MODEL), p["w_o_cross"]).reshape(n, D_MODEL)

    out = _mlp(x2, p["norm_mlp"], p["w_mlp_up"], p["w_mlp_down"], p["norm_final"])
    return out.reshape(b, seq, D_MODEL)


def kernel(x_prompt, x_sample, mem_prompt, mem_sample, norm_mix, w_in, diff_lambda, subln, w_attn_proj, ssm_lambda_re, ssm_lambda_im, ssm_log_dt, ssm_b_re, ssm_b_im, ssm_c_re, ssm_c_im, ssm_d, w_glu_a, w_glu_b, w_mix_out, norm_cross, norm_mem, w_q_cross, w_kv_cross, w_o_cross, norm_mlp, w_mlp_up, w_mlp_down, norm_final):
    p = _prepare((norm_mix, w_in, diff_lambda, subln, w_attn_proj,
                  ssm_lambda_re, ssm_lambda_im, ssm_log_dt, ssm_b_re, ssm_b_im, ssm_c_re, ssm_c_im,
                  ssm_d, w_glu_a, w_glu_b, w_mix_out,
                  norm_cross, norm_mem, w_q_cross, w_kv_cross, w_o_cross,
                  norm_mlp, w_mlp_up, w_mlp_down, norm_final))
    tables = _rope_tables(max(x_prompt.shape[1], x_sample.shape[1]))
    return (_encode(x_prompt, mem_prompt, p, tables), _encode(x_sample, mem_sample, p, tables))
```

```python
import functools
import math

import jax
import jax.numpy as jnp
from jax import lax
from jax.experimental import pallas as pl
from jax.experimental.pallas import tpu as pltpu

D_MODEL = 2048
N_HEADS = 8
HEAD_DIM = 64
ATTN_W = N_HEADS * 2 * HEAD_DIM
SSM_W = D_MODEL // 2
GROUP_SIZE = 16
N_GROUPS = SSM_W // GROUP_SIZE
STATE = 64
IN_COLS = 3 * ATTN_W + SSM_W + 2 * D_MODEL
D_FF = 4 * D_MODEL
X_HEADS = 4
X_HEAD_DIM = D_MODEL // X_HEADS
ROPE_THETA = 10000.0
EPS = 1e-6
LAM_INIT = 0.8 - 0.6 * math.exp(-0.3 * 0)
Q_SCALE = HEAD_DIM ** -0.5 * math.log2(math.e)

CHUNK = 16
GROUP_BLOCK = 8
LANES = 128
VMEM_LIMIT = 56 * 1024 * 1024

F32 = jnp.float32
BF16 = jnp.bfloat16
HI = lax.Precision.HIGHEST


def _params(*sem):
    return pltpu.CompilerParams(dimension_semantics=sem, vmem_limit_bytes=VMEM_LIMIT)


def _rms(x, g):
    return x * lax.rsqrt(jnp.mean(x * x, axis=-1, keepdims=True) + EPS) * g


def _in_proj_kernel(x_ref, g_ref, w_ref, cos_ref, sin_ref, o_ref, u_ref, xn_ref, acc_a, acc_b):
    j = pl.program_id(1)

    @pl.when(j == 0)
    def _():
        xn_ref[...] = _rms(x_ref[...], g_ref[...]).astype(BF16)

    def project(cur):
        cur[...] = jnp.dot(xn_ref[...], w_ref[...], preferred_element_type=F32)

    def finish(prev):
        cos = cos_ref[0]
        sin = sin_ref[0]
        for c in range(prev.shape[1] // LANES):
            xc = prev[:, c * LANES:(c + 1) * LANES]
            o_ref[:, c * LANES:(c + 1) * LANES] = (
                xc * cos + pltpu.roll(xc, LANES // 2, axis=1) * sin).astype(BF16)

    @pl.when(j == 0)
    def _():
        project(acc_a)

    @pl.when((j > 0) & (j % 2 == 0))
    def _():
        finish(acc_b)
        project(acc_a)

    @pl.when(j % 2 == 1)
    def _():
        finish(acc_a)
        project(acc_b)

    @pl.when(j == pl.num_programs(1) - 1)
    def _():
        u_ref[...] = acc_b[...]


def _in_proj(x2, g, w, cos_t, sin_t, seq):
    n = x2.shape[0]
    tm = min(1024, seq)
    tn = ATTN_W
    nj = IN_COLS // tn
    assert nj % 2 == 0
    pos_blocks = seq // tm
    table = lambda i, j: (jnp.where(j == 5, 0, jnp.where(j == 6, 1, 2)), i % pos_blocks, 0)
    return pl.pallas_call(
        _in_proj_kernel,
        grid=(n // tm, nj),
        in_specs=[
            pl.BlockSpec((tm, D_MODEL), lambda i, j: (i, 0)),
            pl.BlockSpec((1, D_MODEL), lambda i, j: (0, 0)),
            pl.BlockSpec((D_MODEL, tn), lambda i, j: (0, (j + nj // 2) % nj)),
            pl.BlockSpec((1, tm, LANES), table),
            pl.BlockSpec((1, tm, LANES), table),
        ],
        out_specs=[
            pl.BlockSpec((tm, tn), lambda i, j: (i, jnp.maximum(j - 1, 0))),
            pl.BlockSpec((tm, tn), lambda i, j: (i, 0)),
        ],
        out_shape=[jax.ShapeDtypeStruct((n, (nj - 1) * tn), BF16),
                   jax.ShapeDtypeStruct((n, SSM_W), F32)],
        scratch_shapes=[pltpu.VMEM((tm, D_MODEL), BF16), pltpu.VMEM((tm, tn), F32),
                        pltpu.VMEM((tm, tn), F32)],
        compiler_params=_params("parallel", "arbitrary"),
        name="in_proj",
    )(x2, g, w, cos_t, sin_t)


def _diff_attn_kernel(dl_ref, sub_ref, q_ref, k_ref, v_ref, o_ref, kt_ref, s_ref, mx_ref, *, tq, unroll):
    seq = v_ref.shape[1]
    steps = seq // tq
    hw = 2 * HEAD_DIM
    lane = lax.broadcasted_iota(jnp.int32, (tq, hw), 1)
    first = ((lane // (HEAD_DIM // 2)) % 2) == 0
    dl = dl_ref[...]
    lam = (jnp.exp(jnp.sum(dl[0:1] * dl[1:2], axis=-1, keepdims=True))
           - jnp.exp(jnp.sum(dl[2:3] * dl[3:4], axis=-1, keepdims=True)) + LAM_INIT)

    def rows(f):
        return pl.ds(pl.multiple_of(f * tq, tq), tq)

    def scores(f, slot):
        f = jnp.minimum(f, steps - 1)
        q = q_ref[0, rows(f), :]
        zero = jnp.zeros_like(q)
        for comp, qm in enumerate((jnp.where(first, q, zero), jnp.where(first, zero, q))):
            s = jnp.dot(qm, kt_ref[...], preferred_element_type=F32)
            s_ref[slot, comp] = s
            mx_ref[slot, comp] = jnp.max(s, axis=-1, keepdims=True)

    def reduce(f, slot):
        vx = jnp.concatenate([v_ref[0], jnp.ones((seq, hw), BF16)], axis=1)
        acc = []
        for comp in range(2):
            p = jnp.exp2(s_ref[slot, comp] - mx_ref[slot, comp]).astype(BF16)
            acc.append(jnp.dot(p, vx, preferred_element_type=F32))
        o = acc[0][:, :hw] / acc[0][:, hw:] - lam * (acc[1][:, :hw] / acc[1][:, hw:])
        o_ref[0, rows(f), :] = (_rms(o, sub_ref[...]) * (1.0 - LAM_INIT)).astype(BF16)

    for r in range(0, seq, tq):
        kt_ref[:, r:r + tq] = k_ref[0, r:r + tq, :].T

    scores(0, 0)

    def body(i, _):
        for k in range(unroll):
            f = i * unroll + k
            scores(f + 1, (k + 1) % 2)
            reduce(f, k % 2)
        return 0

    lax.fori_loop(0, steps // unroll, body, 0)


def _diff_attn(proj3, dl, sub):
    b, seq, _ = proj3.shape
    tq = min(256, seq) if seq <= 2048 else 128
    steps = seq // tq
    unroll = 4 if steps % 4 == 0 else (2 if steps % 2 == 0 else 1)
    hw = 2 * HEAD_DIM
    blk = pl.BlockSpec((1, seq, hw), lambda b_, h: (b_, 0, h))
    col = lambda first: pl.BlockSpec((1, seq, hw), lambda b_, h: (b_, 0, first // hw + h))
    return pl.pallas_call(
        functools.partial(_diff_attn_kernel, tq=tq, unroll=unroll),
        grid=(b, N_HEADS),
        in_specs=[
            pl.BlockSpec((4, HEAD_DIM), lambda b_, h: (0, 0)),
            pl.BlockSpec((1, hw), lambda b_, h: (0, 0)),
            col(2 * D_MODEL), col(2 * D_MODEL + ATTN_W), col(2 * D_MODEL + 2 * ATTN_W),
        ],
        out_specs=blk,
        out_shape=jax.ShapeDtypeStruct((b, seq, ATTN_W), BF16),
        scratch_shapes=[pltpu.VMEM((hw, seq), BF16),
                        pltpu.VMEM((2, 2, tq, seq), F32), pltpu.VMEM((2, 2, tq, 1), F32)],
        compiler_params=_params("parallel", "arbitrary"),
        name="diff_attn",
    )(dl, sub, proj3, proj3, proj3)


def _slot_swap_matrix():
    src = jnp.arange(8 * LANES).reshape(8, 8, GROUP_SIZE).transpose(1, 0, 2).reshape(-1)
    return (src[None, :] == jnp.arange(8 * LANES)[:, None]).astype(BF16)


def _ssm_kernel(u_ref, perm_ref, wi_ref, wb_ref, wc_ref, ap_ref, a8_ref, pt_ref, z_ref,
                ut_ref, s_ref, zt_ref, e_ref, hc_ref):
    nc = u_ref.shape[0]
    n_tiles = nc // 8
    gw = CHUNK * GROUP_SIZE
    pw = 2 * gw

    for th in range(CHUNK // 8):
        by_token = jnp.concatenate([u_ref[:, th * 8 + tl, :] for tl in range(8)], axis=1)
        by_group = jnp.dot(by_token.astype(BF16), perm_ref[...], preferred_element_type=F32)
        for g in range(GROUP_BLOCK):
            ut_ref[:, g * gw + th * LANES:g * gw + (th + 1) * LANES] = (
                by_group[:, g * LANES:(g + 1) * LANES].astype(BF16))

    for q in range(GROUP_BLOCK // 2):
        cols = slice(q * pw, (q + 1) * pw)
        s_ref[:, cols] = jnp.dot(ut_ref[:, cols], wb_ref[0, q], preferred_element_type=F32)

    row = lax.broadcasted_iota(jnp.int32, (8, LANES), 0)

    def shift_down(x, k):
        return jnp.where(row >= k, pltpu.roll(x, k, axis=0), 0.0)

    def shift_up(x, k):
        return jnp.where(row < 8 - k, pltpu.roll(x, 8 - k, axis=0), 0.0)

    n_chain = 2 * (GROUP_BLOCK // 2)

    def chain_cols(q, d):
        base = q * pw + d * 2 * LANES
        return slice(base, base + LANES), slice(base + LANES, base + 2 * LANES)

    def local_scan(i, _):
        rs = pl.ds(pl.multiple_of(i * 8, 8), 8)
        for q in range(GROUP_BLOCK // 2):
            for d in range(2):
                c_re, c_im = chain_cols(q, d)
                shift = shift_down if d == 0 else shift_up
                ch = q * 2 + d
                x_re, x_im = s_ref[rs, c_re], s_ref[rs, c_im]
                for ki, k in enumerate((1, 2, 4)):
                    a_re, a_im = ap_ref[0, (2 * ch) * 3 + ki], ap_ref[0, (2 * ch + 1) * 3 + ki]
                    y_re, y_im = shift(x_re, k), shift(x_im, k)
                    x_re, x_im = (x_re + a_re * y_re - a_im * y_im,
                                  x_im + a_re * y_im + a_im * y_re)
                s_ref[rs, c_re] = shift(x_re, 1)
                s_ref[rs, c_im] = shift(x_im, 1)
                pos = i if d == 0 else n_tiles - 1 - i
                edge = slice(7, 8) if d == 0 else slice(0, 1)
                e_ref[pos, ch:ch + 1, :] = x_re[edge]
                e_ref[pos, n_chain + ch:n_chain + ch + 1, :] = x_im[edge]
        return 0

    lax.fori_loop(0, n_tiles, local_scan, 0)

    a8_re, a8_im = a8_ref[0, :n_chain], a8_ref[0, n_chain:]

    def carry_in(i, h):
        h_re, h_im = h
        hc_ref[i, :n_chain, :] = h_re
        hc_ref[i, n_chain:, :] = h_im
        e_re, e_im = e_ref[i, :n_chain, :], e_ref[i, n_chain:, :]
        return a8_re * h_re - a8_im * h_im + e_re, a8_re * h_im + a8_im * h_re + e_im

    zero = jnp.zeros((n_chain, LANES), F32)
    lax.fori_loop(0, n_tiles, carry_in, (zero, zero))

    def add_carry(i, _):
        rs = pl.ds(pl.multiple_of(i * 8, 8), 8)
        for q in range(GROUP_BLOCK // 2):
            for d in range(2):
                c_re, c_im = chain_cols(q, d)
                ch = q * 2 + d
                pos = i if d == 0 else n_tiles - 1 - i
                h_re = jnp.broadcast_to(hc_ref[pos, ch:ch + 1, :], (8, LANES))
                h_im = jnp.broadcast_to(hc_ref[pos, n_chain + ch:n_chain + ch + 1, :], (8, LANES))
                p_re, p_im = pt_ref[0, 2 * ch], pt_ref[0, 2 * ch + 1]
                s_ref[rs, c_re] += p_re * h_re - p_im * h_im
                s_ref[rs, c_im] += p_re * h_im + p_im * h_re
        return 0

    lax.fori_loop(0, n_tiles, add_carry, 0)

    for q in range(GROUP_BLOCK // 2):
        cols = slice(q * pw, (q + 1) * pw)
        carried = jnp.dot(s_ref[:, cols].astype(BF16), wc_ref[0, q], preferred_element_type=F32)
        for g2 in range(2):
            g = 2 * q + g2
            gc = slice(g * gw, (g + 1) * gw)
            y = carried[:, g2 * gw:(g2 + 1) * gw] + jnp.dot(
                ut_ref[:, gc], wi_ref[0, g], preferred_element_type=F32)
            z = 0.5 * y * (1.0 + jnp.tanh(0.7978845608028654 * (y + 0.044715 * (y * y * y))))
            ut_ref[:, gc] = z.astype(BF16)

    for th in range(CHUNK // 8):
        by_group = jnp.concatenate(
            [ut_ref[:, g * gw + th * LANES:g * gw + (th + 1) * LANES] for g in range(GROUP_BLOCK)], axis=1)
        by_token = jnp.dot(by_group, perm_ref[...], preferred_element_type=F32)
        for tl in range(8):
            zt_ref[:, th * 8 + tl, :] = by_token[:, tl * LANES:(tl + 1) * LANES]
    z_ref[...] = zt_ref[...].astype(BF16)


def _ssm(u3, b, wi, wb, wc, ap, a8, pt):
    nc = u3.shape[0] // b
    cw = GROUP_BLOCK * CHUNK * GROUP_SIZE
    blk = pl.BlockSpec((nc, CHUNK, LANES), lambda g, b_: (b_, 0, g))
    whole = lambda a: pl.BlockSpec((1,) + a.shape[1:], lambda g, b_: (g,) + (0,) * (a.ndim - 1))
    carry = pltpu.VMEM((nc // 8, 2 * GROUP_BLOCK, LANES), F32)
    perm = _slot_swap_matrix()
    return pl.pallas_call(
        _ssm_kernel,
        grid=(N_GROUPS // GROUP_BLOCK, b),
        in_specs=[blk, pl.BlockSpec(perm.shape, lambda g, b_: (0, 0)),
                  whole(wi), whole(wb), whole(wc), whole(ap), whole(a8), whole(pt)],
        out_specs=blk,
        out_shape=jax.ShapeDtypeStruct(u3.shape, BF16),
        scratch_shapes=[pltpu.VMEM((nc, cw), BF16), pltpu.VMEM((nc, cw), F32),
                        pltpu.VMEM((nc, CHUNK, LANES), F32), carry, carry],
        compiler_params=_params("parallel", "arbitrary"),
        name="s5_scan",
    )(u3, perm, wi, wb, wc, ap, a8, pt)


def _ssm_operators(lam_re, lam_im, log_dt, b_re, b_im, c_re, c_im, d_skip):
    t_ = CHUNK
    g_, p_, c_ = N_GROUPS, STATE, GROUP_SIZE
    nb, npair = g_ // GROUP_BLOCK, g_ // 2
    lr, li = lam_re.astype(F32), lam_im.astype(F32)
    dt = jnp.exp(log_dt.astype(F32))[..., None]
    mag = jnp.exp(lr * dt)
    ab_re, ab_im = mag * jnp.cos(li * dt), mag * jnp.sin(li * dt)
    den = lr * lr + li * li
    n_re, n_im = ab_re - 1.0, ab_im
    k_re = (n_re * lr + n_im * li) / den
    k_im = (n_im * lr - n_re * li) / den
    br, bi = b_re.astype(F32), b_im.astype(F32)
    bb_re = k_re[..., None] * br - k_im[..., None] * bi
    bb_im = k_re[..., None] * bi + k_im[..., None] * br
    cr, ci = c_re.astype(F32), c_im.astype(F32)

    x_re, x_im = lr * dt, li * dt
    gw = t_ * c_

    def cpow(e, xr, xi):
        m = jnp.exp(e * xr)
        return m * jnp.cos(e * xi), m * jnp.sin(e * xi)

    swap = lambda a: a.transpose(0, 1, 3, 2)
    tok = jnp.repeat(jnp.arange(t_, dtype=F32), c_)
    first = (jnp.arange(2) == 0)[None, None, :, None, None]

    pw_r, pw_i = cpow(tok, x_re[..., None], x_im[..., None])
    rep = jnp.tile(jnp.eye(c_, dtype=F32), (1, t_))
    crt = jnp.einsum('dgop,on->dgpn', cr, rep, precision=HI)
    cit = jnp.einsum('dgop,on->dgpn', ci, rep, precision=HI)
    lhs = jnp.concatenate([crt * pw_r - cit * pw_i, -(crt * pw_i + cit * pw_r)], axis=2)
    rhs = jnp.concatenate([swap(bb_re), swap(bb_im)], axis=3)
    kern = jnp.einsum('dgck,dgkn->dgcn', rhs, lhs, precision=lax.Precision.HIGH)
    fwd, bwd = kern[0], kern[1]
    bwd_rev = bwd.reshape(g_, c_, t_, c_)[:, :, ::-1].reshape(g_, c_, gw)
    lagged = jnp.concatenate([bwd_rev[..., :gw - c_], fwd[..., :c_] + bwd[..., :c_], fwd[..., c_:]], axis=-1)
    n_lag = lagged.shape[-1]
    start = (t_ - 1 - jnp.arange(t_)) * c_
    window = (jnp.arange(n_lag)[None, :, None] == start[:, None, None] + jnp.arange(gw)[None, None, :])
    w_intra = jnp.einsum('gck,tkn->gtcn', lagged.astype(BF16), window.astype(BF16),
                         preferred_element_type=F32)
    skip = jnp.eye(gw, dtype=F32) * jnp.tile(d_skip.astype(F32).reshape(g_, c_), (1, t_))[:, :, None]
    w_intra = (w_intra.reshape(g_, gw, gw) + skip).reshape(nb, GROUP_BLOCK, gw, gw)

    def pair_rows(bb):
        x = jnp.tile(swap(bb), (1, 1, t_, 1)).reshape(2, npair, 2, gw, p_)
        return jnp.concatenate([jnp.where(first, x, 0.0), jnp.where(first, 0.0, x)], axis=-1).reshape(
            2, npair, 2 * gw, 2 * p_)

    steps = jnp.arange(t_, dtype=F32)
    e_rows = jnp.stack([t_ - 1 - steps, steps])[:, None, :, None]
    pr, pi = cpow(e_rows, x_re.reshape(2, npair, 1, 2 * p_), x_im.reshape(2, npair, 1, 2 * p_))
    by_row = lambda a: jnp.broadcast_to(a[:, :, None, :, None, :], (2, npair, 2, t_, c_, 2 * p_)).reshape(
        2, npair, 2 * gw, 2 * p_)
    pr, pi = by_row(pr), by_row(pi)
    bt_re, bt_im = pair_rows(bb_re), pair_rows(bb_im)
    wb_re, wb_im = pr * bt_re - pi * bt_im, pr * bt_im + pi * bt_re
    wb = jnp.concatenate([wb_re[0], wb_im[0], wb_re[1], wb_im[1]], axis=-1)
    wb = wb.reshape(nb, GROUP_BLOCK // 2, 2 * gw, 8 * p_)

    def pair_cols(cc_tiled):
        x = cc_tiled.reshape(2, npair, 2, p_, gw)
        return jnp.concatenate([jnp.where(first, x, 0.0), jnp.where(first, 0.0, x)], axis=-1).reshape(
            2, npair, 2 * p_, 2 * gw)

    e_cols = jnp.stack([steps + 1, t_ - steps])[:, None, None, :]
    qr, qi = cpow(e_cols, x_re.reshape(2, npair, 2 * p_, 1), x_im.reshape(2, npair, 2 * p_, 1))
    spread = jnp.tile(jnp.repeat(jnp.eye(t_, dtype=F32), c_, axis=1), (1, 2))
    qr = jnp.einsum('dqrt,tn->dqrn', qr, spread, precision=HI)
    qi = jnp.einsum('dqrt,tn->dqrn', qi, spread, precision=HI)
    ct_re, ct_im = pair_cols(crt), pair_cols(cit)
    wc_re, wc_im = ct_re * qr - ct_im * qi, -(ct_re * qi + ct_im * qr)
    wc = jnp.concatenate([wc_re[0], wc_im[0], wc_re[1], wc_im[1]], axis=1)
    wc = wc.reshape(nb, GROUP_BLOCK // 2, 8 * p_, 2 * gw)

    def table(e):
        re, im = cpow(e, x_re, x_im)
        tab = jnp.stack([re, im], axis=2)
        k = tab.shape[0]
        tab = tab.reshape(k, 2, 2, nb, GROUP_BLOCK // 2, 2 * p_).transpose(3, 4, 1, 2, 0, 5)
        return tab.reshape(nb, (GROUP_BLOCK // 2) * 4, k, 2 * p_)

    step = jnp.asarray([1.0, 2.0, 4.0], F32) * t_
    ap = table(jnp.broadcast_to(step[:, None, None, None], (3, 2, 1, 1)))
    ap = jnp.broadcast_to(ap[:, :, :, None, :], ap.shape[:3] + (8, 2 * p_)).reshape(nb, -1, 8, 2 * p_)
    a8 = table(jnp.full((1, 2, 1, 1), 8.0 * t_, F32))
    a8 = a8.reshape(nb, GROUP_BLOCK // 2, 2, 2, 2 * p_).transpose(0, 3, 1, 2, 4).reshape(nb, -1, 2 * p_)
    j8 = jnp.arange(8, dtype=F32) * t_
    pt = table(jnp.stack([j8, j8[::-1]], axis=1)[:, :, None, None])
    return w_intra.astype(BF16), wb.astype(BF16), wc.astype(BF16), ap, a8, pt


def _resident(a):
    return pl.BlockSpec(a.shape, lambda *_: (0,) * a.ndim, pipeline_mode=pl.Buffered(1))


def _merge_kernel(x_ref, att_ref, z_ref, gate_ref, wp_ref, wa_ref, wb_ref, wo_ref, o_ref, mix_ref,
                  *, tn):
    att = att_ref[...]
    z = z_ref[...]
    for j in range(D_MODEL // tn):
        cs = slice(j * tn, (j + 1) * tn)
        gs = slice(D_MODEL + j * tn, D_MODEL + (j + 1) * tn)
        y_attn = jnp.dot(att, wp_ref[:, cs], preferred_element_type=F32)
        y_ssm = (jnp.dot(z, wa_ref[:, cs], preferred_element_type=F32)
                 * jax.nn.sigmoid(jnp.dot(z, wb_ref[:, cs], preferred_element_type=F32)))
        g_a = jax.nn.sigmoid(gate_ref[:, cs].astype(F32))
        g_s = jax.nn.sigmoid(gate_ref[:, gs].astype(F32))
        mix_ref[:, cs] = (g_a * y_attn + g_s * y_ssm).astype(BF16)
    o_ref[...] = x_ref[...] + jnp.dot(mix_ref[...], wo_ref[...], preferred_element_type=F32)


def _merge(x2, att, z, gates, wp, wa, wb, wo):
    n = att.shape[0]
    tm = min(256, n)
    row = lambda cols: pl.BlockSpec((tm, cols), lambda i: (i, 0))
    return pl.pallas_call(
        functools.partial(_merge_kernel, tn=512),
        grid=(n // tm,),
        in_specs=[row(D_MODEL), row(ATTN_W), row(SSM_W), row(2 * D_MODEL),
                  _resident(wp), _resident(wa), _resident(wb), _resident(wo)],
        out_specs=row(D_MODEL),
        out_shape=jax.ShapeDtypeStruct((n, D_MODEL), F32),
        scratch_shapes=[pltpu.VMEM((tm, D_MODEL), BF16)],
        compiler_params=_params("parallel"),
        name="merge_mix_out",
    )(x2, att, z, gates, wp, wa, wb, wo)


def _norm_mm_kernel(x_ref, g_ref, w_ref, o_ref, xn_ref, *, scale):
    @pl.when(pl.program_id(1) == 0)
    def _():
        xn_ref[...] = _rms(x_ref[...], g_ref[...]).astype(BF16)

    acc = jnp.dot(xn_ref[...], w_ref[...], preferred_element_type=F32)
    o_ref[...] = (acc * scale).astype(BF16)


def _norm_mm(x, g, w, scale):
    n, kdim = x.shape
    cols = w.shape[1]
    tm = min(512, n)
    tn = 1024
    return pl.pallas_call(
        functools.partial(_norm_mm_kernel, scale=scale),
        grid=(n // tm, cols // tn),
        in_specs=[
            pl.BlockSpec((tm, kdim), lambda i, j: (i, 0)),
            pl.BlockSpec((1, kdim), lambda i, j: (0, 0)),
            pl.BlockSpec((kdim, tn), lambda i, j: (0, j)),
        ],
        out_specs=pl.BlockSpec((tm, tn), lambda i, j: (i, j)),
        out_shape=jax.ShapeDtypeStruct((n, cols), BF16),
        scratch_shapes=[pltpu.VMEM((tm, kdim), BF16)],
        compiler_params=_params("parallel", "arbitrary"),
        name="norm_matmul",
    )(x, g, w)


def _cross_kernel(x_ref, g_ref, wq_ref, kv_ref, wo_ref, o_ref, q_ref, oc_ref):
    x = x_ref[0]
    hn = _rms(x, g_ref[...]).astype(BF16)
    q_ref[...] = (jnp.dot(hn, wq_ref[...], preferred_element_type=F32)
                  * X_HEAD_DIM ** -0.5).astype(BF16)
    nt = (((1,), (1,)), ((), ()))
    for h in range(X_HEADS):
        cols = slice(h * X_HEAD_DIM, (h + 1) * X_HEAD_DIM)
        vcols = slice(D_MODEL + h * X_HEAD_DIM, D_MODEL + (h + 1) * X_HEAD_DIM)
        s = lax.dot_general(q_ref[:, cols], kv_ref[0, :, cols], nt, preferred_element_type=F32)
        p = jnp.exp(s - jnp.max(s, axis=-1, keepdims=True))
        l = jnp.sum(p, axis=-1, keepdims=True)
        o = jnp.dot(p.astype(BF16), kv_ref[0, :, vcols], preferred_element_type=F32)
        oc_ref[:, cols] = (o / l).astype(BF16)
    o_ref[0] = x + jnp.dot(oc_ref[...], wo_ref[...], preferred_element_type=F32)


def _cross(x3, g, wq, kv3, wo):
    b, seq, _ = x3.shape
    m = kv3.shape[1]
    tq = min(512, seq)
    blk = pl.BlockSpec((1, tq, D_MODEL), lambda b_, i: (b_, i, 0))
    return pl.pallas_call(
        _cross_kernel,
        grid=(b, seq // tq),
        in_specs=[blk, _resident(g), _resident(wq),
                  pl.BlockSpec((1, m, 2 * D_MODEL), lambda b_, i: (b_, 0, 0)), _resident(wo)],
        out_specs=blk,
        out_shape=jax.ShapeDtypeStruct(x3.shape, F32),
        scratch_shapes=[pltpu.VMEM((tq, D_MODEL), BF16), pltpu.VMEM((tq, D_MODEL), BF16)],
        compiler_params=_params("parallel", "arbitrary"),
        name="cross_attn_block",
    )(x3, g, wq, kv3, wo)


def _mlp_kernel(x_ref, g_ref, wu_ref, wd_ref, gf_ref, o_ref, hn_ref, acc_ref):
    f = pl.program_id(1)

    @pl.when(f == 0)
    def _():
        hn_ref[...] = _rms(x_ref[...], g_ref[...]).astype(BF16)
        acc_ref[...] = jnp.zeros_like(acc_ref)

    h = jnp.maximum(jnp.dot(hn_ref[...], wu_ref[...], preferred_element_type=F32), 0.0)
    acc_ref[...] += jnp.dot((h * h).astype(BF16), wd_ref[...], preferred_element_type=F32)

    @pl.when(f == pl.num_programs(1) - 1)
    def _():
        o_ref[...] = _rms(x_ref[...] + acc_ref[...], gf_ref[...])


def _mlp(x, g, wu, wd, gf):
    n = x.shape[0]
    tm = min(512, n)
    tf = 1024
    return pl.pallas_call(
        _mlp_kernel,
        grid=(n // tm, D_FF // tf),
        in_specs=[
            pl.BlockSpec((tm, D_MODEL), lambda i, f: (i, 0)),
            pl.BlockSpec((1, D_MODEL), lambda i, f: (0, 0)),
            pl.BlockSpec((D_MODEL, tf), lambda i, f: (0, f)),
            pl.BlockSpec((tf, D_MODEL), lambda i, f: (f, 0)),
            pl.BlockSpec((1, D_MODEL), lambda i, f: (0, 0)),
        ],
        out_specs=pl.BlockSpec((tm, D_MODEL), lambda i, f: (i, 0)),
        out_shape=jax.ShapeDtypeStruct((n, D_MODEL), F32),
        scratch_shapes=[pltpu.VMEM((tm, D_MODEL), BF16), pltpu.VMEM((tm, D_MODEL), F32)],
        compiler_params=_params("parallel", "arbitrary"),
        name="mlp_final_norm",
    )(x, g, wu, wd, gf)


def _rope_tables(seq):
    half = HEAD_DIM // 2
    inv = ROPE_THETA ** (-jnp.arange(0, HEAD_DIM, 2, dtype=F32) / HEAD_DIM)
    ang = jnp.arange(seq, dtype=F32)[:, None] * inv[None, :]
    cos, sin = jnp.cos(ang), jnp.sin(ang)
    cos_t = jnp.tile(cos, (1, LANES // half))
    sin_t = jnp.concatenate([-sin, -sin, sin, sin], axis=1)
    return (jnp.stack([cos_t * Q_SCALE, cos_t, jnp.ones_like(cos_t)]),
            jnp.stack([sin_t * Q_SCALE, sin_t, jnp.zeros_like(sin_t)]))


def _qk_shuffle():
    half = HEAD_DIM // 2
    src = jnp.arange(2 * ATTN_W).reshape(2 * N_HEADS, 2, 2, half).transpose(0, 2, 1, 3).reshape(-1)
    return (src[None, :] == jnp.arange(2 * ATTN_W)[:, None]).astype(BF16)


def _prepare(w):
    (norm_mix, w_in, diff_lambda, subln, w_attn_proj,
     lam_re, lam_im, log_dt, b_re, b_im, c_re, c_im, d_skip,
     w_glu_a, w_glu_b, w_mix_out,
     norm_cross, norm_mem, w_q_cross, w_kv_cross, w_o_cross,
     norm_mlp, w_mlp_up, w_mlp_down, norm_final) = w
    w_qk = jnp.dot(w_in[0, :, :2 * ATTN_W].astype(BF16), _qk_shuffle(), preferred_element_type=BF16)
    w_in_p = jnp.concatenate([w_qk, w_in[0, :, 2 * ATTN_W:].astype(BF16)], axis=1)
    row = lambda v: v.astype(F32).reshape(1, -1)
    return dict(
        norm_mix=row(norm_mix[0]), w_in=w_in_p, diff_lambda=diff_lambda[0].astype(F32),
        subln=row(subln[0]), w_attn_proj=w_attn_proj[0].astype(BF16),
        ssm=_ssm_operators(lam_re[0], lam_im[0], log_dt[0], b_re[0], b_im[0], c_re[0], c_im[0],
                           d_skip[0]),
        w_glu_a=w_glu_a[0].astype(BF16), w_glu_b=w_glu_b[0].astype(BF16),
        w_mix_out=w_mix_out[0].astype(BF16),
        norm_cross=row(norm_cross[0]), norm_mem=row(norm_mem[0]),
        w_q_cross=w_q_cross[0].astype(BF16), w_kv_cross=w_kv_cross[0].astype(BF16),
        w_o_cross=w_o_cross[0].astype(BF16),
        norm_mlp=row(norm_mlp[0]), w_mlp_up=w_mlp_up[0].astype(BF16),
        w_mlp_down=w_mlp_down[0].astype(BF16), norm_final=row(norm_final),
    )


def _encode(x, mem, p, tables):
    b, seq, _ = x.shape
    assert seq % (8 * CHUNK) == 0
    n = b * seq
    nc = seq // CHUNK
    x2 = x.reshape(n, D_MODEL)
    cos_t, sin_t = (t[:, :seq] for t in tables)

    proj, u = _in_proj(x2, p["norm_mix"], p["w_in"], cos_t, sin_t, seq)
    att = _diff_attn(proj.reshape(b, seq, -1), p["diff_lambda"], p["subln"])

    z = _ssm(u.reshape(b * nc, CHUNK, SSM_W), b, *p["ssm"]).reshape(n, SSM_W)

    x2 = _merge(x2, att.reshape(n, ATTN_W), z, proj, p["w_attn_proj"], p["w_glu_a"], p["w_glu_b"],
                p["w_mix_out"])

    m_tok = mem.shape[1]
    kv = _norm_mm(mem.reshape(b * m_tok, D_MODEL), p["norm_mem"], p["w_kv_cross"], 1.0)
    x2 = _cross(x2.reshape(b, seq, D_MODEL), p["norm_cross"], p["w_q_cross"],
                kv.reshape(b, m_tok, 2 * D_MODEL), p["w_o_cross"]).reshape(n, D_MODEL)

    out = _mlp(x2, p["norm_mlp"], p["w_mlp_up"], p["w_mlp_down"], p["norm_final"])
    return out.reshape(b, seq, D_MODEL)


def kernel(x_prompt, x_sample, mem_prompt, mem_sample, norm_mix, w_in, diff_lambda, subln, w_attn_proj, ssm_lambda_re, ssm_lambda_im, ssm_log_dt, ssm_b_re, ssm_b_im, ssm_c_re, ssm_c_im, ssm_d, w_glu_a, w_glu_b, w_mix_out, norm_cross, norm_mem, w_q_cross, w_kv_cross, w_o_cross, norm_mlp, w_mlp_up, w_mlp_down, norm_final):
    p = _prepare((norm_mix, w_in, diff_lambda, subln, w_attn_proj,
                  ssm_lambda_re, ssm_lambda_im, ssm_log_dt, ssm_b_re, ssm_b_im, ssm_c_re, ssm_c_im,
                  ssm_d, w_glu_a, w_glu_b, w_mix_out,
                  norm_cross, norm_mem, w_q_cross, w_kv_cross, w_o_cross,
                  norm_mlp, w_mlp_up, w_mlp_down, norm_final))
    tables = _rope_tables(max(x_prompt.shape[1], x_sample.shape[1]))
    return (_encode(x_prompt, mem_prompt, p, tables), _encode(x_sample, mem_sample, p, tables))
```
